```python
import jax, jax.numpy as jnp
from jax import lax
import numpy as np

D_MODEL = 2048
BATCH = 2
SEQ = 4096
DEPTH = 1

D_CONV = D_MODEL // 2
CONV_GROUPS = 8
CONV_WIDTH = 3
GLA_HEADS = 4
D_GLA_V = D_MODEL // 2
D_GLA_K = D_GLA_V // 2
HEAD_K = D_GLA_K // GLA_HEADS
HEAD_V = D_GLA_V // GLA_HEADS
GATE_RANK = 16
GATE_TAU = 16.0
CHUNK = 64
D_FF = 4 * D_MODEL
LN_EPS = 1e-5
RMS_EPS = 1e-6
DN_ALPHA = (2.0 * DEPTH) ** 0.25
DN_BETA = (8.0 * DEPTH) ** -0.25

PROJ_SIZES = (D_CONV, D_CONV, D_CONV, D_GLA_K, D_GLA_K, D_GLA_V, D_GLA_V, GATE_RANK)
D_IN_PROJ = sum(PROJ_SIZES)
PROJ_SPLITS = tuple(int(s) for s in np.cumsum(PROJ_SIZES)[:-1])

kernel_name = "hymba_conv_gla_deepnorm_block"


def layer_norm(x, g, b):
    xf = x.astype(jnp.float32)
    mu = jnp.mean(xf, axis=-1, keepdims=True)
    var = jnp.mean(jnp.square(xf - mu), axis=-1, keepdims=True)
    y = (xf - mu) * lax.rsqrt(var + LN_EPS)
    return (y * g.astype(jnp.float32) + b.astype(jnp.float32)).astype(x.dtype)


def group_rms_norm(x, g, groups):
    shp = x.shape
    xf = x.astype(jnp.float32).reshape(shp[:-1] + (groups, shp[-1] // groups))
    xf = xf * lax.rsqrt(jnp.mean(jnp.square(xf), axis=-1, keepdims=True) + RMS_EPS)
    return (xf.reshape(shp) * g.astype(jnp.float32)).astype(x.dtype)


def short_gated_conv(b_gate, c_gate, u, conv_w):
    h = c_gate * u
    y = lax.conv_general_dilated(
        h, conv_w[:, None, :].astype(h.dtype), window_strides=(1,),
        padding=[(CONV_WIDTH - 1, 0)], dimension_numbers=("NWC", "WIO", "NWC"),
        feature_group_count=h.shape[-1])
    return b_gate * y


def gla_chunked(q, k, v, log_a):
    bsz, seq = q.shape[0], q.shape[1]
    n_chunks = seq // CHUNK

    def to_chunks(t):
        return t.reshape(bsz, n_chunks, CHUNK, GLA_HEADS, t.shape[-1]).transpose(0, 3, 1, 2, 4).astype(jnp.float32)

    qc = to_chunks(q) * (HEAD_K ** -0.5)
    kc, vc, gc = to_chunks(k), to_chunks(v), to_chunks(log_a)
    bcum = jnp.cumsum(gc, axis=3)
    b_last = bcum[:, :, :, -1:, :]
    q_dec = qc * jnp.exp(bcum)
    k_inv = kc * jnp.exp(-bcum)
    k_end = kc * jnp.exp(b_last - bcum)

    causal = jnp.tril(jnp.ones((CHUNK, CHUNK), dtype=bool))
    scores = jnp.where(causal, jnp.einsum("bhncd,bhnsd->bhncs", q_dec, k_inv), 0.0)
    o_intra = jnp.einsum("bhncs,bhnse->bhnce", scores, vc)

    delta = jnp.einsum("bhncd,bhnce->bhnde", k_end, vc)
    decay = jnp.exp(b_last[:, :, :, 0, :])

    def step(state, inp):
        dec, dlt = inp
        return dec[..., None] * state + dlt, state

    init = jnp.zeros((bsz, GLA_HEADS, HEAD_K, HEAD_V), jnp.float32)
    _, states = lax.scan(step, init, (jnp.moveaxis(decay, 2, 0), jnp.moveaxis(delta, 2, 0)))
    states = jnp.moveaxis(states, 0, 2)
    o_inter = jnp.einsum("bhncd,bhnde->bhnce", q_dec, states)
    o = o_intra + o_inter
    return o.transpose(0, 2, 3, 1, 4).reshape(bsz, seq, GLA_HEADS * HEAD_V)


def setup_inputs(seed: int = 0) -> dict:
    key = jax.random.key(seed)
    ks = jax.random.split(key, 16)
    f32 = jnp.float32
    x = jax.random.normal(ks[0], (BATCH, SEQ, D_MODEL), f32)
    col_scale = jnp.concatenate([
        jnp.full((D_CONV,), 1.0, f32), jnp.full((D_CONV,), 1.0, f32), jnp.full((D_CONV,), DN_BETA, f32),
        jnp.full((D_GLA_K,), 1.0, f32), jnp.full((D_GLA_K,), 1.0, f32), jnp.full((D_GLA_V,), DN_BETA, f32),
        jnp.full((D_GLA_V,), 1.0, f32), jnp.full((GATE_RANK,), 1.0, f32)])
    w_in = jax.random.normal(ks[1], (DEPTH, D_MODEL, D_IN_PROJ), f32) * (D_MODEL ** -0.5) * col_scale
    conv_w = jax.random.normal(ks[2], (DEPTH, CONV_WIDTH, D_CONV), f32) * (CONV_WIDTH ** -0.5)
    conv_norm_g = 1.0 + 0.02 * jax.random.normal(ks[3], (DEPTH, D_CONV), f32)
    w_gate_up = jax.random.normal(ks[4], (DEPTH, GATE_RANK, D_GLA_K), f32) * (GATE_RANK ** -0.5)
    gate_bias = 0.1 * jax.random.normal(ks[5], (DEPTH, D_GLA_K), f32)
    gla_norm_g = 1.0 + 0.02 * jax.random.normal(ks[6], (DEPTH, D_GLA_V), f32)
    w_out = jax.random.normal(ks[7], (DEPTH, D_CONV + D_GLA_V, D_MODEL), f32) * ((D_CONV + D_GLA_V) ** -0.5) * DN_BETA
    ln1_g = 1.0 + 0.02 * jax.random.normal(ks[8], (DEPTH, D_MODEL), f32)
    ln1_b = 0.02 * jax.random.normal(ks[9], (DEPTH, D_MODEL), f32)
    w_ff_up = jax.random.normal(ks[10], (DEPTH, D_MODEL, D_FF), f32) * (D_MODEL ** -0.5) * DN_BETA
    w_ff_down = jax.random.normal(ks[11], (DEPTH, D_FF, D_MODEL), f32) * (D_FF ** -0.5) * DN_BETA
    ln2_g = 1.0 + 0.02 * jax.random.normal(ks[12], (DEPTH, D_MODEL), f32)
    ln2_b = 0.02 * jax.random.normal(ks[13], (DEPTH, D_MODEL), f32)
    return {"x": x, "w_in": w_in, "conv_w": conv_w, "conv_norm_g": conv_norm_g,
            "w_gate_up": w_gate_up, "gate_bias": gate_bias, "gla_norm_g": gla_norm_g,
            "w_out": w_out, "ln1_g": ln1_g, "ln1_b": ln1_b, "w_ff_up": w_ff_up,
            "w_ff_down": w_ff_down, "ln2_g": ln2_g, "ln2_b": ln2_b}


def reference(x, w_in, conv_w, conv_norm_g, w_gate_up, gate_bias, gla_norm_g, w_out,
              ln1_g, ln1_b, w_ff_up, w_ff_down, ln2_g, ln2_b):
    bsz, seq = x.shape[0], x.shape[1]
    for l in range(DEPTH):
        proj = x @ w_in[l]
        b_gate, c_gate, u, q, k, v, r, z_low = jnp.split(proj, PROJ_SPLITS, axis=-1)

        y_conv = short_gated_conv(b_gate, c_gate, u, conv_w[l])
        y_conv = group_rms_norm(y_conv, conv_norm_g[l], CONV_GROUPS)

        z = (z_low @ w_gate_up[l] + gate_bias[l]).astype(jnp.float32)
        log_a = jax.nn.log_sigmoid(z) / GATE_TAU
        hk = (bsz, seq, GLA_HEADS, HEAD_K)
        o = gla_chunked(q.reshape(hk), k.reshape(hk), v.reshape(bsz, seq, GLA_HEADS, HEAD_V), log_a.reshape(hk))
        o = group_rms_norm(o.astype(x.dtype), gla_norm_g[l], GLA_HEADS)
        y_gla = o * jax.nn.silu(r)

        mix = jnp.concatenate([y_conv, y_gla], axis=-1) @ w_out[l]
        x = layer_norm(DN_ALPHA * x + mix, ln1_g[l], ln1_b[l])

        ff = jnp.square(jax.nn.relu(x @ w_ff_up[l])) @ w_ff_down[l]
        x = layer_norm(DN_ALPHA * x + ff, ln2_g[l], ln2_b[l])
    return x
```

```python
import functools

import jax
import jax.numpy as jnp
from jax import lax
from jax.experimental import pallas as pl
from jax.experimental.pallas import tpu as pltpu

F32 = jnp.float32
BF16 = jnp.bfloat16

D_MODEL = 2048
D_CONV = 1024
CONV_GROUPS = 8
CONV_GROUP_WIDTH = D_CONV // CONV_GROUPS
GLA_HEADS = 4
HEAD_K = 128
HEAD_V = 256
D_GLA_K = GLA_HEADS * HEAD_K
D_GLA_V = GLA_HEADS * HEAD_V
GATE_RANK = 16
GATE_TAU = 16.0
CHUNK = 64
D_FF = 4 * D_MODEL
LN_EPS = 1e-5
RMS_EPS = 1e-6
DEPTH = 1
DN_ALPHA = (2.0 * DEPTH) ** 0.25

LANE = 128
D_PROJ_MAIN = 3 * D_CONV + 2 * D_GLA_K + 2 * D_GLA_V
D_PROJ_PAD = D_PROJ_MAIN + LANE

Q_BLK0 = (3 * D_CONV) // HEAD_K
K_BLK0 = (3 * D_CONV + D_GLA_K) // HEAD_K
V_BLK0 = (3 * D_CONV + 2 * D_GLA_K) // HEAD_V
R_BLK0 = (3 * D_CONV + 2 * D_GLA_K + D_GLA_V) // HEAD_V
Z_BLK = D_PROJ_MAIN // LANE

VMEM_LIMIT = 56 * 1024 * 1024

PROJ_TM = 1024
PROJ_TN = 896
CONV_TS = 512
GLA_T = 512
OUT_TM = 512
FFN_TM = 512
FFN_TF = 512


def _layer_norm(y, g, b):
    mu = jnp.mean(y, axis=-1, keepdims=True)
    yc = y - mu
    var = jnp.mean(yc * yc, axis=-1, keepdims=True)
    return yc * lax.rsqrt(var + LN_EPS) * g + b


def _in_proj_kernel(x_ref, w_ref, o_ref):
    o_ref[...] = jnp.dot(x_ref[...], w_ref[...], preferred_element_type=F32).astype(o_ref.dtype)


def _in_proj(xb, w_in_b):
    m = xb.shape[0]
    return pl.pallas_call(
        _in_proj_kernel,
        grid=(m // PROJ_TM, D_PROJ_PAD // PROJ_TN),
        in_specs=[pl.BlockSpec((PROJ_TM, D_MODEL), lambda i, j: (i, 0)),
                  pl.BlockSpec((D_MODEL, PROJ_TN), lambda i, j: (0, j))],
        out_specs=pl.BlockSpec((PROJ_TM, PROJ_TN), lambda i, j: (i, j)),
        out_shape=jax.ShapeDtypeStruct((m, D_PROJ_PAD), BF16),
        compiler_params=pltpu.CompilerParams(
            dimension_semantics=("parallel", "arbitrary"), vmem_limit_bytes=VMEM_LIMIT),
        name="in_proj",
    )(xb, w_in_b)


def _conv_kernel(b_ref, c_ref, u_ref, w_ref, g_ref, o_ref, carry_ref):
    @pl.when(pl.program_id(1) == 0)
    def _():
        carry_ref[...] = jnp.zeros_like(carry_ref)

    ts = b_ref.shape[0]
    h = c_ref[...].astype(F32) * u_ref[...].astype(F32)
    prev = carry_ref[...]
    c1 = prev[7:8, :]
    c2 = prev[6:7, :]
    row = lax.broadcasted_iota(jnp.int32, h.shape, 0)
    h1 = jnp.where(row == 0, c1, pltpu.roll(h, 1, axis=0))
    h2 = jnp.where(row == 0, c2, jnp.where(row == 1, c1, pltpu.roll(h, 2, axis=0)))
    carry_ref[...] = h[ts - 8:, :]
    w = w_ref[...]
    y = b_ref[...].astype(F32) * (w[2:3, :] * h + w[1:2, :] * h1 + w[0:1, :] * h2)
    g = g_ref[...]
    for grp in range(CONV_GROUPS):
        sl = slice(grp * CONV_GROUP_WIDTH, (grp + 1) * CONV_GROUP_WIDTH)
        yg = y[:, sl]
        ms = jnp.mean(yg * yg, axis=-1, keepdims=True)
        o_ref[:, sl] = (yg * lax.rsqrt(ms + RMS_EPS) * g[:, sl]).astype(o_ref.dtype)


def _conv_branch(proj, conv_w8, conv_g, bsz, seq):
    nblk = seq // CONV_TS
    row_blk = lambda b, t: b * nblk + t
    return pl.pallas_call(
        _conv_kernel,
        grid=(bsz, nblk),
        in_specs=[pl.BlockSpec((CONV_TS, D_CONV), lambda b, t: (row_blk(b, t), 0)),
                  pl.BlockSpec((CONV_TS, D_CONV), lambda b, t: (row_blk(b, t), 1)),
                  pl.BlockSpec((CONV_TS, D_CONV), lambda b, t: (row_blk(b, t), 2)),
                  pl.BlockSpec((8, D_CONV), lambda b, t: (0, 0)),
                  pl.BlockSpec((1, D_CONV), lambda b, t: (0, 0))],
        out_specs=pl.BlockSpec((CONV_TS, D_CONV), lambda b, t: (row_blk(b, t), 0)),
        out_shape=jax.ShapeDtypeStruct((bsz * seq, D_CONV), BF16),
        scratch_shapes=[pltpu.VMEM((8, D_CONV), F32)],
        compiler_params=pltpu.CompilerParams(
            dimension_semantics=("parallel", "arbitrary"), vmem_limit_bytes=VMEM_LIMIT),
        name="conv_branch",
    )(proj, proj, proj, conv_w8, conv_g)


def _gla_kernel(q_ref, k_ref, v_ref, r_ref, zl_ref, wg_ref, gb_ref, g_ref, o_ref, state_ref):
    @pl.when(pl.program_id(2) == 0)
    def _():
        state_ref[...] = jnp.zeros_like(state_ref)

    t_blk = q_ref.shape[0]
    n_chunks = t_blk // CHUNK

    z = jnp.dot(zl_ref[...], wg_ref[0], preferred_element_type=F32) + gb_ref[0]
    log_a = jax.nn.log_sigmoid(z) * (1.0 / GATE_TAU)

    pos = lax.broadcasted_iota(jnp.int32, log_a.shape, 0) & (CHUNK - 1)
    bcum = log_a
    shift = 1
    while shift < CHUNK:
        bcum = bcum + jnp.where(pos >= shift, pltpu.roll(bcum, shift, axis=0), 0.0)
        shift *= 2

    q_dec = ((q_ref[...].astype(F32) * (HEAD_K ** -0.5)) * jnp.exp(bcum)).astype(BF16)
    kf = k_ref[...].astype(F32)
    k_inv = (kf * jnp.exp(-bcum)).astype(BF16)

    b_last_rows = [bcum[c * CHUNK + CHUNK - 1:c * CHUNK + CHUNK, :] for c in range(n_chunks)]
    pad = jnp.zeros((HEAD_K - n_chunks, HEAD_K), F32)
    decay_t = jnp.transpose(jnp.concatenate([jnp.exp(b) for b in b_last_rows] + [pad], axis=0))

    causal = (lax.broadcasted_iota(jnp.int32, (CHUNK, CHUNK), 0)
              >= lax.broadcasted_iota(jnp.int32, (CHUNK, CHUNK), 1))
    gain = g_ref[...]
    state = state_ref[...]
    for c in range(n_chunks):
        rows = slice(c * CHUNK, (c + 1) * CHUNK)
        qd = q_dec[rows]
        v_c = v_ref[rows, :]
        k_end = (kf[rows] * jnp.exp(b_last_rows[c] - bcum[rows])).astype(BF16)
        scores = lax.dot_general(qd, k_inv[rows], (((1,), (1,)), ((), ())),
                                 preferred_element_type=F32)
        scores = jnp.where(causal, scores, 0.0).astype(BF16)
        o = (jnp.dot(scores, v_c, preferred_element_type=F32)
             + jnp.dot(qd, state.astype(BF16), preferred_element_type=F32))
        delta = lax.dot_general(k_end, v_c, (((0,), (0,)), ((), ())),
                                preferred_element_type=F32)
        state = decay_t[:, c:c + 1] * state + delta
        ms = jnp.mean(o * o, axis=-1, keepdims=True)
        o_n = o * lax.rsqrt(ms + RMS_EPS) * gain
        r_c = r_ref[rows, :].astype(F32)
        o_ref[rows, :] = (o_n * (r_c * jax.nn.sigmoid(r_c))).astype(o_ref.dtype)
    state_ref[...] = state


def _gla_branch(proj, wg_b, gate_bias, gla_g, bsz, seq):
    nblk = seq // GLA_T
    row_blk = lambda b, h, t: b * nblk + t
    return pl.pallas_call(
        _gla_kernel,
        grid=(bsz, GLA_HEADS, nblk),
        in_specs=[pl.BlockSpec((GLA_T, HEAD_K), lambda b, h, t: (row_blk(b, h, t), Q_BLK0 + h)),
                  pl.BlockSpec((GLA_T, HEAD_K), lambda b, h, t: (row_blk(b, h, t), K_BLK0 + h)),
                  pl.BlockSpec((GLA_T, HEAD_V), lambda b, h, t: (row_blk(b, h, t), V_BLK0 + h)),
                  pl.BlockSpec((GLA_T, HEAD_V), lambda b, h, t: (row_blk(b, h, t), R_BLK0 + h)),
                  pl.BlockSpec((GLA_T, LANE), lambda b, h, t: (row_blk(b, h, t), Z_BLK)),
                  pl.BlockSpec((1, LANE, HEAD_K), lambda b, h, t: (h, 0, 0)),
                  pl.BlockSpec((1, 1, HEAD_K), lambda b, h, t: (h, 0, 0)),
                  pl.BlockSpec((1, HEAD_V), lambda b, h, t: (0, h))],
        out_specs=pl.BlockSpec((GLA_T, HEAD_V), lambda b, h, t: (row_blk(b, h, t), h)),
        out_shape=jax.ShapeDtypeStruct((bsz * seq, D_GLA_V), BF16),
        scratch_shapes=[pltpu.VMEM((HEAD_K, HEAD_V), F32)],
        compiler_params=pltpu.CompilerParams(
            dimension_semantics=("parallel", "parallel", "arbitrary"), vmem_limit_bytes=VMEM_LIMIT),
        name="gla_branch",
    )(proj, proj, proj, proj, proj, wg_b, gate_bias, gla_g)


def _out_ln1_kernel(yc_ref, yg_ref, wc_ref, wg_ref, x_ref, g_ref, b_ref, o_ref):
    mix = (jnp.dot(yc_ref[...], wc_ref[...], preferred_element_type=F32)
           + jnp.dot(yg_ref[...], wg_ref[...], preferred_element_type=F32))
    o_ref[...] = _layer_norm(DN_ALPHA * x_ref[...] + mix, g_ref[...], b_ref[...])


def _out_ln1(y_conv, y_gla, w_out_b, x2, ln_g, ln_b):
    m = x2.shape[0]
    return pl.pallas_call(
        _out_ln1_kernel,
        grid=(m // OUT_TM,),
        in_specs=[pl.BlockSpec((OUT_TM, D_CONV), lambda i: (i, 0)),
                  pl.BlockSpec((OUT_TM, D_GLA_V), lambda i: (i, 0)),
                  pl.BlockSpec((D_CONV, D_MODEL), lambda i: (0, 0)),
                  pl.BlockSpec((D_GLA_V, D_MODEL), lambda i: (1, 0)),
                  pl.BlockSpec((OUT_TM, D_MODEL), lambda i: (i, 0)),
                  pl.BlockSpec((1, D_MODEL), lambda i: (0, 0)),
                  pl.BlockSpec((1, D_MODEL), lambda i: (0, 0))],
        out_specs=pl.BlockSpec((OUT_TM, D_MODEL), lambda i: (i, 0)),
        out_shape=jax.ShapeDtypeStruct((m, D_MODEL), F32),
        compiler_params=pltpu.CompilerParams(
            dimension_semantics=("parallel",), vmem_limit_bytes=VMEM_LIMIT),
        name="out_ln1",
    )(y_conv, y_gla, w_out_b, w_out_b, x2, ln_g, ln_b)


def _ffn_ln2_kernel(x_ref, wu_ref, wd_ref, g_ref, b_ref, o_ref, acc_ref, xb_ref):
    f = pl.program_id(1)

    @pl.when(f == 0)
    def _():
        xb_ref[...] = x_ref[...].astype(BF16)
        acc_ref[...] = jnp.zeros_like(acc_ref)

    h = jnp.dot(xb_ref[...], wu_ref[...], preferred_element_type=F32)
    h = jnp.maximum(h, 0.0)
    h = (h * h).astype(BF16)
    acc_ref[...] += jnp.dot(h, wd_ref[...], preferred_element_type=F32)

    @pl.when(f == pl.num_programs(1) - 1)
    def _():
        o_ref[...] = _layer_norm(DN_ALPHA * x_ref[...] + acc_ref[...], g_ref[...], b_ref[...])


def _ffn_ln2(x1, w_up_b, w_down_b, ln_g, ln_b):
    m = x1.shape[0]
    return pl.pallas_call(
        _ffn_ln2_kernel,
        grid=(m // FFN_TM, D_FF // FFN_TF),
        in_specs=[pl.BlockSpec((FFN_TM, D_MODEL), lambda i, f: (i, 0)),
                  pl.BlockSpec((D_MODEL, FFN_TF), lambda i, f: (0, f)),
                  pl.BlockSpec((FFN_TF, D_MODEL), lambda i, f: (f, 0)),
                  pl.BlockSpec((1, D_MODEL), lambda i, f: (0, 0)),
                  pl.BlockSpec((1, D_MODEL), lambda i, f: (0, 0))],
        out_specs=pl.BlockSpec((FFN_TM, D_MODEL), lambda i, f: (i, 0)),
        out_shape=jax.ShapeDtypeStruct((m, D_MODEL), F32),
        scratch_shapes=[pltpu.VMEM((FFN_TM, D_MODEL), F32),
                        pltpu.VMEM((FFN_TM, D_MODEL), BF16)],
        compiler_params=pltpu.CompilerParams(
            dimension_semantics=("parallel", "arbitrary"), vmem_limit_bytes=VMEM_LIMIT),
        name="ffn_ln2",
    )(x1, w_up_b, w_down_b, ln_g, ln_b)


def kernel(x, w_in, conv_w, conv_norm_g, w_gate_up, gate_bias, gla_norm_g, w_out,
           ln1_g, ln1_b, w_ff_up, w_ff_down, ln2_g, ln2_b):
    bsz, seq, _ = x.shape
    assert seq % GLA_T == 0 and seq % CONV_TS == 0 and GLA_T % CHUNK == 0
    x2 = x.reshape(bsz * seq, D_MODEL)
    for l in range(DEPTH):
        w_in_b = jnp.pad(w_in[l].astype(BF16), ((0, 0), (0, D_PROJ_PAD - w_in.shape[-1])))
        wg_b = jnp.pad(w_gate_up[l].astype(BF16), ((0, LANE - GATE_RANK), (0, 0)))
        wg_b = wg_b.reshape(LANE, GLA_HEADS, HEAD_K).transpose(1, 0, 2)
        gb = gate_bias[l].reshape(GLA_HEADS, 1, HEAD_K)
        conv_w8 = jnp.pad(conv_w[l], ((0, 8 - conv_w.shape[1]), (0, 0)))

        proj = _in_proj(x2.astype(BF16), w_in_b)
        y_conv = _conv_branch(proj, conv_w8, conv_norm_g[l].reshape(1, D_CONV), bsz, seq)
        y_gla = _gla_branch(proj, wg_b, gb, gla_norm_g[l].reshape(1, D_GLA_V), bsz, seq)
        x1 = _out_ln1(y_conv, y_gla, w_out[l].astype(BF16), x2,
                      ln1_g[l].reshape(1, D_MODEL), ln1_b[l].reshape(1, D_MODEL))
        x2 = _ffn_ln2(x1, w_ff_up[l].astype(BF16), w_ff_down[l].astype(BF16),
                      ln2_g[l].reshape(1, D_MODEL), ln2_b[l].reshape(1, D_MODEL))
    return x2.reshape(bsz, seq, D_MODEL)
```

```python
import functools

import jax
import jax.numpy as jnp
from jax import lax
from jax.experimental import pallas as pl
from jax.experimental.pallas import tpu as pltpu

F32 = jnp.float32
BF16 = jnp.bfloat16

D_MODEL = 2048
D_CONV = 1024
CONV_GROUPS = 8
CONV_GROUP_WIDTH = D_CONV // CONV_GROUPS
GLA_HEADS = 4
HEAD_K = 128
HEAD_V = 256
D_GLA_K = GLA_HEADS * HEAD_K
D_GLA_V = GLA_HEADS * HEAD_V
GATE_RANK = 16
GATE_TAU = 16.0
CHUNK = 64
D_FF = 4 * D_MODEL
LN_EPS = 1e-5
RMS_EPS = 1e-6
DEPTH = 1
DN_ALPHA = (2.0 * DEPTH) ** 0.25

LANE = 128
D_PROJ_MAIN = 3 * D_CONV + 2 * D_GLA_K + 2 * D_GLA_V

Q_BLK0 = (3 * D_CONV) // HEAD_K
K_BLK0 = (3 * D_CONV + D_GLA_K) // HEAD_K
V_BLK0 = (3 * D_CONV + 2 * D_GLA_K) // HEAD_V
R_BLK0 = (3 * D_CONV + 2 * D_GLA_K + D_GLA_V) // HEAD_V

VMEM_LIMIT = 56 * 1024 * 1024

PROJ_TM = 1024
PROJ_TN = 768
CONV_TS = 512
GLA_T = 512
OUT_TM = 512
FFN_TM = 1024
FFN_TF = 1024


def _layer_norm(y, g, b):
    mu = jnp.mean(y, axis=-1, keepdims=True)
    yc = y - mu
    var = jnp.mean(yc * yc, axis=-1, keepdims=True)
    return yc * lax.rsqrt(var + LN_EPS) * g + b


def _in_proj_kernel(x_ref, w_ref, wz_ref, o_ref, z_ref, xb_ref):
    @pl.when(pl.program_id(1) == 0)
    def _():
        xb_ref[...] = x_ref[...].astype(BF16)
        z_ref[...] = jnp.dot(xb_ref[...], wz_ref[...], preferred_element_type=F32).astype(z_ref.dtype)

    o_ref[...] = jnp.dot(xb_ref[...], w_ref[...].astype(BF16),
                         preferred_element_type=F32).astype(o_ref.dtype)


def _in_proj(x2, w_in, wz_b):
    m = x2.shape[0]
    return pl.pallas_call(
        _in_proj_kernel,
        grid=(m // PROJ_TM, D_PROJ_MAIN // PROJ_TN),
        in_specs=[pl.BlockSpec((PROJ_TM, D_MODEL), lambda i, j: (i, 0)),
                  pl.BlockSpec((D_MODEL, PROJ_TN), lambda i, j: (0, j)),
                  pl.BlockSpec((D_MODEL, LANE), lambda i, j: (0, 0))],
        out_specs=[pl.BlockSpec((PROJ_TM, PROJ_TN), lambda i, j: (i, j)),
                   pl.BlockSpec((PROJ_TM, LANE), lambda i, j: (i, 0))],
        out_shape=[jax.ShapeDtypeStruct((m, D_PROJ_MAIN), BF16),
                   jax.ShapeDtypeStruct((m, LANE), BF16)],
        scratch_shapes=[pltpu.VMEM((PROJ_TM, D_MODEL), BF16)],
        compiler_params=pltpu.CompilerParams(
            dimension_semantics=("parallel", "arbitrary"), vmem_limit_bytes=VMEM_LIMIT),
        name="in_proj",
    )(x2, w_in, wz_b)


def _conv_kernel(b_ref, c_ref, u_ref, w_ref, g_ref, o_ref, carry_ref):
    @pl.when(pl.program_id(1) == 0)
    def _():
        carry_ref[...] = jnp.zeros_like(carry_ref)

    ts = b_ref.shape[0]
    h = c_ref[...].astype(F32) * u_ref[...].astype(F32)
    prev = carry_ref[...]
    c1 = prev[7:8, :]
    c2 = prev[6:7, :]
    row = lax.broadcasted_iota(jnp.int32, h.shape, 0)
    h1 = jnp.where(row == 0, c1, pltpu.roll(h, 1, axis=0))
    h2 = jnp.where(row == 0, c2, jnp.where(row == 1, c1, pltpu.roll(h, 2, axis=0)))
    carry_ref[...] = h[ts - 8:, :]
    w = w_ref[...]
    y = b_ref[...].astype(F32) * (w[2:3, :] * h + w[1:2, :] * h1 + w[0:1, :] * h2)
    g = g_ref[...]
    for grp in range(CONV_GROUPS):
        sl = slice(grp * CONV_GROUP_WIDTH, (grp + 1) * CONV_GROUP_WIDTH)
        yg = y[:, sl]
        ms = jnp.mean(yg * yg, axis=-1, keepdims=True)
        o_ref[:, sl] = (yg * lax.rsqrt(ms + RMS_EPS) * g[:, sl]).astype(o_ref.dtype)


def _conv_branch(proj, conv_w8, conv_g, bsz, seq):
    nblk = seq // CONV_TS
    row_blk = lambda b, t: b * nblk + t
    return pl.pallas_call(
        _conv_kernel,
        grid=(bsz, nblk),
        in_specs=[pl.BlockSpec((CONV_TS, D_CONV), lambda b, t: (row_blk(b, t), 0)),
                  pl.BlockSpec((CONV_TS, D_CONV), lambda b, t: (row_blk(b, t), 1)),
                  pl.BlockSpec((CONV_TS, D_CONV), lambda b, t: (row_blk(b, t), 2)),
                  pl.BlockSpec((8, D_CONV), lambda b, t: (0, 0)),
                  pl.BlockSpec((1, D_CONV), lambda b, t: (0, 0))],
        out_specs=pl.BlockSpec((CONV_TS, D_CONV), lambda b, t: (row_blk(b, t), 0)),
        out_shape=jax.ShapeDtypeStruct((bsz * seq, D_CONV), BF16),
        scratch_shapes=[pltpu.VMEM((8, D_CONV), F32)],
        compiler_params=pltpu.CompilerParams(
            dimension_semantics=("parallel", "arbitrary"), vmem_limit_bytes=VMEM_LIMIT),
        name="conv_branch",
    )(proj, proj, proj, conv_w8, conv_g)


def _gla_kernel(q_ref, k_ref, v_ref, r_ref, zl_ref, wg_ref, gb_ref, g_ref, o_ref, state_ref):
    @pl.when(pl.program_id(2) == 0)
    def _():
        state_ref[...] = jnp.zeros_like(state_ref)

    t_blk = q_ref.shape[0]
    n_chunks = t_blk // CHUNK

    z = jnp.dot(zl_ref[...], wg_ref[0], preferred_element_type=F32) + gb_ref[0]
    log_a = jax.nn.log_sigmoid(z) * (1.0 / GATE_TAU)

    pos = lax.broadcasted_iota(jnp.int32, log_a.shape, 0) & (CHUNK - 1)
    bcum = log_a
    shift = 1
    while shift < CHUNK:
        bcum = bcum + jnp.where(pos >= shift, pltpu.roll(bcum, shift, axis=0), 0.0)
        shift *= 2

    q_dec = ((q_ref[...].astype(F32) * (HEAD_K ** -0.5)) * jnp.exp(bcum)).astype(BF16)
    kf = k_ref[...].astype(F32)
    k_inv = (kf * jnp.exp(-bcum)).astype(BF16)

    b_last_rows = [bcum[c * CHUNK + CHUNK - 1:c * CHUNK + CHUNK, :] for c in range(n_chunks)]
    pad = jnp.zeros((HEAD_K - n_chunks, HEAD_K), F32)
    decay_t = jnp.transpose(jnp.concatenate([jnp.exp(b) for b in b_last_rows] + [pad], axis=0))

    causal = (lax.broadcasted_iota(jnp.int32, (CHUNK, CHUNK), 0)
              >= lax.broadcasted_iota(jnp.int32, (CHUNK, CHUNK), 1))
    gain = g_ref[...]
    state = state_ref[...]
    for c in range(n_chunks):
        rows = slice(c * CHUNK, (c + 1) * CHUNK)
        qd = q_dec[rows]
        v_c = v_ref[rows, :]
        k_end = (kf[rows] * jnp.exp(b_last_rows[c] - bcum[rows])).astype(BF16)
        scores = lax.dot_general(qd, k_inv[rows], (((1,), (1,)), ((), ())),
                                 preferred_element_type=F32)
        scores = jnp.where(causal, scores, 0.0).astype(BF16)
        o = (jnp.dot(scores, v_c, preferred_element_type=F32)
             + jnp.dot(qd, state.astype(BF16), preferred_element_type=F32))
        delta = lax.dot_general(k_end, v_c, (((0,), (0,)), ((), ())),
                                preferred_element_type=F32)
        state = decay_t[:, c:c + 1] * state + delta
        ms = jnp.mean(o * o, axis=-1, keepdims=True)
        o_n = o * lax.rsqrt(ms + RMS_EPS) * gain
        r_c = r_ref[rows, :].astype(F32)
        o_ref[rows, :] = (o_n * (r_c * jax.nn.sigmoid(r_c))).astype(o_ref.dtype)
    state_ref[...] = state


def _gla_branch(proj, z_low, wg_b, gate_bias, gla_g, bsz, seq):
    nblk = seq // GLA_T
    row_blk = lambda b, h, t: b * nblk + t
    return pl.pallas_call(
        _gla_kernel,
        grid=(bsz, GLA_HEADS, nblk),
        in_specs=[pl.BlockSpec((GLA_T, HEAD_K), lambda b, h, t: (row_blk(b, h, t), Q_BLK0 + h)),
                  pl.BlockSpec((GLA_T, HEAD_K), lambda b, h, t: (row_blk(b, h, t), K_BLK0 + h)),
                  pl.BlockSpec((GLA_T, HEAD_V), lambda b, h, t: (row_blk(b, h, t), V_BLK0 + h)),
                  pl.BlockSpec((GLA_T, HEAD_V), lambda b, h, t: (row_blk(b, h, t), R_BLK0 + h)),
                  pl.BlockSpec((GLA_T, LANE), lambda b, h, t: (row_blk(b, h, t), 0)),
                  pl.BlockSpec((1, LANE, HEAD_K), lambda b, h, t: (h, 0, 0)),
                  pl.BlockSpec((1, 1, HEAD_K), lambda b, h, t: (h, 0, 0)),
                  pl.BlockSpec((1, HEAD_V), lambda b, h, t: (0, h))],
        out_specs=pl.BlockSpec((GLA_T, HEAD_V), lambda b, h, t: (row_blk(b, h, t), h)),
        out_shape=jax.ShapeDtypeStruct((bsz * seq, D_GLA_V), BF16),
        scratch_shapes=[pltpu.VMEM((HEAD_K, HEAD_V), F32)],
        compiler_params=pltpu.CompilerParams(
            dimension_semantics=("parallel", "parallel", "arbitrary"), vmem_limit_bytes=VMEM_LIMIT),
        name="gla_branch",
    )(proj, proj, proj, proj, z_low, wg_b, gate_bias, gla_g)


def _out_ln1_kernel(yc_ref, yg_ref, wc_ref, wg_ref, x_ref, g_ref, b_ref, o_ref):
    mix = (jnp.dot(yc_ref[...], wc_ref[...], preferred_element_type=F32)
           + jnp.dot(yg_ref[...], wg_ref[...], preferred_element_type=F32))
    o_ref[...] = _layer_norm(DN_ALPHA * x_ref[...] + mix, g_ref[...], b_ref[...]).astype(o_ref.dtype)


def _out_ln1(y_conv, y_gla, w_out_b, x2, ln_g, ln_b):
    m = x2.shape[0]
    return pl.pallas_call(
        _out_ln1_kernel,
        grid=(m // OUT_TM,),
        in_specs=[pl.BlockSpec((OUT_TM, D_CONV), lambda i: (i, 0)),
                  pl.BlockSpec((OUT_TM, D_GLA_V), lambda i: (i, 0)),
                  pl.BlockSpec((D_CONV, D_MODEL), lambda i: (0, 0)),
                  pl.BlockSpec((D_GLA_V, D_MODEL), lambda i: (1, 0)),
                  pl.BlockSpec((OUT_TM, D_MODEL), lambda i: (i, 0)),
                  pl.BlockSpec((1, D_MODEL), lambda i: (0, 0)),
                  pl.BlockSpec((1, D_MODEL), lambda i: (0, 0))],
        out_specs=pl.BlockSpec((OUT_TM, D_MODEL), lambda i: (i, 0)),
        out_shape=jax.ShapeDtypeStruct((m, D_MODEL), BF16),
        compiler_params=pltpu.CompilerParams(
            dimension_semantics=("parallel",), vmem_limit_bytes=VMEM_LIMIT),
        name="out_ln1",
    )(y_conv, y_gla, w_out_b, w_out_b, x2, ln_g, ln_b)


def _ffn_ln2_kernel(x_ref, wu_ref, wd_ref, g_ref, b_ref, o_ref):
    f = pl.program_id(1)
    h = jnp.dot(x_ref[...], wu_ref[...], preferred_element_type=F32)
    h = jnp.maximum(h, 0.0)
    h = (h * h).astype(BF16)

    @pl.when(f == 0)
    def _():
        o_ref[...] = jnp.dot(h, wd_ref[...], preferred_element_type=F32)

    @pl.when(f > 0)
    def _():
        o_ref[...] += jnp.dot(h, wd_ref[...], preferred_element_type=F32)

    @pl.when(f == pl.num_programs(1) - 1)
    def _():
        o_ref[...] = _layer_norm(DN_ALPHA * x_ref[...].astype(F32) + o_ref[...],
                                 g_ref[...], b_ref[...])


def _ffn_ln2(x1, w_up_b, w_down_b, ln_g, ln_b):
    m = x1.shape[0]
    return pl.pallas_call(
        _ffn_ln2_kernel,
        grid=(m // FFN_TM, D_FF // FFN_TF),
        in_specs=[pl.BlockSpec((FFN_TM, D_MODEL), lambda i, f: (i, 0)),
                  pl.BlockSpec((D_MODEL, FFN_TF), lambda i, f: (0, f)),
                  pl.BlockSpec((FFN_TF, D_MODEL), lambda i, f: (f, 0)),
                  pl.BlockSpec((1, D_MODEL), lambda i, f: (0, 0)),
                  pl.BlockSpec((1, D_MODEL), lambda i, f: (0, 0))],
        out_specs=pl.BlockSpec((FFN_TM, D_MODEL), lambda i, f: (i, 0)),
        out_shape=jax.ShapeDtypeStruct((m, D_MODEL), F32),
        compiler_params=pltpu.CompilerParams(
            dimension_semantics=("parallel", "arbitrary"), vmem_limit_bytes=VMEM_LIMIT),
        name="ffn_ln2",
    )(x1, w_up_b, w_down_b, ln_g, ln_b)


def kernel(x, w_in, conv_w, conv_norm_g, w_gate_up, gate_bias, gla_norm_g, w_out,
           ln1_g, ln1_b, w_ff_up, w_ff_down, ln2_g, ln2_b):
    bsz, seq, _ = x.shape
    assert seq % GLA_T == 0 and seq % CONV_TS == 0 and GLA_T % CHUNK == 0
    x2 = x.reshape(bsz * seq, D_MODEL)
    for l in range(DEPTH):
        wz_b = jnp.pad(w_in[l][:, D_PROJ_MAIN:].astype(BF16), ((0, 0), (0, LANE - GATE_RANK)))
        wg_b = jnp.pad(w_gate_up[l].astype(BF16), ((0, LANE - GATE_RANK), (0, 0)))
        wg_b = wg_b.reshape(LANE, GLA_HEADS, HEAD_K).transpose(1, 0, 2)
        gb = gate_bias[l].reshape(GLA_HEADS, 1, HEAD_K)
        conv_w8 = jnp.pad(conv_w[l], ((0, 8 - conv_w.shape[1]), (0, 0)))

        proj, z_low = _in_proj(x2, w_in[l], wz_b)
        y_conv = _conv_branch(proj, conv_w8, conv_norm_g[l].reshape(1, D_CONV), bsz, seq)
        y_gla = _gla_branch(proj, z_low, wg_b, gb, gla_norm_g[l].reshape(1, D_GLA_V), bsz, seq)
        x1 = _out_ln1(y_conv, y_gla, w_out[l].astype(BF16), x2,
                      ln1_g[l].reshape(1, D_MODEL), ln1_b[l].reshape(1, D_MODEL))
        x2 = _ffn_ln2(x1, w_ff_up[l].astype(BF16), w_ff_down[l].astype(BF16),
                      ln2_g[l].reshape(1, D_MODEL), ln2_b[l].reshape(1, D_MODEL))
    return x2.reshape(bsz, seq, D_MODEL)
```

```python
import functools

import jax
import jax.numpy as jnp
from jax import lax
from jax.experimental import pallas as pl
from jax.experimental.pallas import tpu as pltpu

F32 = jnp.float32
BF16 = jnp.bfloat16

D_MODEL = 2048
D_CONV = 1024
CONV_GROUPS = 8
CONV_GROUP_WIDTH = D_CONV // CONV_GROUPS
GLA_HEADS = 4
HEAD_K = 128
HEAD_V = 256
D_GLA_K = GLA_HEADS * HEAD_K
D_GLA_V = GLA_HEADS * HEAD_V
GATE_RANK = 16
GATE_TAU = 16.0
CHUNK = 64
D_FF = 4 * D_MODEL
LN_EPS = 1e-5
RMS_EPS = 1e-6
DEPTH = 1
DN_ALPHA = (2.0 * DEPTH) ** 0.25

LANE = 128
D_PROJ_MAIN = 3 * D_CONV + 2 * D_GLA_K + 2 * D_GLA_V

Q_BLK0 = (3 * D_CONV) // HEAD_K
K_BLK0 = (3 * D_CONV + D_GLA_K) // HEAD_K
V_BLK0 = (3 * D_CONV + 2 * D_GLA_K) // HEAD_V
R_BLK0 = (3 * D_CONV + 2 * D_GLA_K + D_GLA_V) // HEAD_V

VMEM_LIMIT = 56 * 1024 * 1024

PROJ_TM = 1024
PROJ_TN = 768
CONV_TS = 512
GLA_T = 512
OUT_TM = 512
FFN_TM = 1024
FFN_TF = 1024


def _layer_norm(y, g, b):
    mu = jnp.mean(y, axis=-1, keepdims=True)
    yc = y - mu
    var = jnp.mean(yc * yc, axis=-1, keepdims=True)
    return yc * lax.rsqrt(var + LN_EPS) * g + b


_NT_DIMS = (((1,), (1,)), ((), ()))


def _in_proj_kernel(x_ref, w_ref, wz_ref, o_ref, z_ref, xb_ref):
    @pl.when(pl.program_id(1) == 0)
    def _():
        xb_ref[...] = x_ref[...].astype(BF16)
        z_ref[...] = lax.dot_general(xb_ref[...], wz_ref[...].astype(BF16), _NT_DIMS,
                                     preferred_element_type=F32).astype(z_ref.dtype)

    o_ref[...] = lax.dot_general(xb_ref[...], w_ref[...].astype(BF16), _NT_DIMS,
                                 preferred_element_type=F32).astype(o_ref.dtype)


def _in_proj(x2, w_in_t, layer):
    m = x2.shape[0]
    return pl.pallas_call(
        _in_proj_kernel,
        grid=(m // PROJ_TM, D_PROJ_MAIN // PROJ_TN),
        in_specs=[pl.BlockSpec((PROJ_TM, D_MODEL), lambda i, j: (i, 0)),
                  pl.BlockSpec((None, PROJ_TN, D_MODEL), lambda i, j: (layer, j, 0)),
                  pl.BlockSpec((None, GATE_RANK, D_MODEL),
                               lambda i, j: (layer, D_PROJ_MAIN // GATE_RANK, 0))],
        out_specs=[pl.BlockSpec((PROJ_TM, PROJ_TN), lambda i, j: (i, j)),
                   pl.BlockSpec((PROJ_TM, GATE_RANK), lambda i, j: (i, 0))],
        out_shape=[jax.ShapeDtypeStruct((m, D_PROJ_MAIN), BF16),
                   jax.ShapeDtypeStruct((m, GATE_RANK), BF16)],
        scratch_shapes=[pltpu.VMEM((PROJ_TM, D_MODEL), BF16)],
        compiler_params=pltpu.CompilerParams(
            dimension_semantics=("parallel", "arbitrary"), vmem_limit_bytes=VMEM_LIMIT),
        name="in_proj",
    )(x2, w_in_t, w_in_t)


def _conv_kernel(b_ref, c_ref, u_ref, w_ref, g_ref, o_ref, carry_ref):
    @pl.when(pl.program_id(1) == 0)
    def _():
        carry_ref[...] = jnp.zeros_like(carry_ref)

    ts = b_ref.shape[0]
    h = c_ref[...].astype(F32) * u_ref[...].astype(F32)
    prev = carry_ref[...]
    c1 = prev[7:8, :]
    c2 = prev[6:7, :]
    row = lax.broadcasted_iota(jnp.int32, h.shape, 0)
    h1 = jnp.where(row == 0, c1, pltpu.roll(h, 1, axis=0))
    h2 = jnp.where(row == 0, c2, jnp.where(row == 1, c1, pltpu.roll(h, 2, axis=0)))
    carry_ref[...] = h[ts - 8:, :]
    w = w_ref[...]
    y = b_ref[...].astype(F32) * (w[2:3, :] * h + w[1:2, :] * h1 + w[0:1, :] * h2)
    g = g_ref[...]
    for grp in range(CONV_GROUPS):
        sl = slice(grp * CONV_GROUP_WIDTH, (grp + 1) * CONV_GROUP_WIDTH)
        yg = y[:, sl]
        ms = jnp.mean(yg * yg, axis=-1, keepdims=True)
        o_ref[:, sl] = (yg * lax.rsqrt(ms + RMS_EPS) * g[:, sl]).astype(o_ref.dtype)


def _conv_branch(proj, conv_w8, conv_g, bsz, seq):
    nblk = seq // CONV_TS
    row_blk = lambda b, t: b * nblk + t
    return pl.pallas_call(
        _conv_kernel,
        grid=(bsz, nblk),
        in_specs=[pl.BlockSpec((CONV_TS, D_CONV), lambda b, t: (row_blk(b, t), 0)),
                  pl.BlockSpec((CONV_TS, D_CONV), lambda b, t: (row_blk(b, t), 1)),
                  pl.BlockSpec((CONV_TS, D_CONV), lambda b, t: (row_blk(b, t), 2)),
                  pl.BlockSpec((8, D_CONV), lambda b, t: (0, 0)),
                  pl.BlockSpec((1, D_CONV), lambda b, t: (0, 0))],
        out_specs=pl.BlockSpec((CONV_TS, D_CONV), lambda b, t: (row_blk(b, t), 0)),
        out_shape=jax.ShapeDtypeStruct((bsz * seq, D_CONV), BF16),
        scratch_shapes=[pltpu.VMEM((8, D_CONV), F32)],
        compiler_params=pltpu.CompilerParams(
            dimension_semantics=("parallel", "arbitrary"), vmem_limit_bytes=VMEM_LIMIT),
        name="conv_branch",
    )(proj, proj, proj, conv_w8, conv_g)


def _gla_kernel(q_ref, k_ref, v_ref, r_ref, zl_ref, wg_ref, gb_ref, g_ref,
                wo_ref, wu_ref, wd_ref, o_ref, wo_b_ref, wu_b_ref, wd_b_ref, state_ref):
    @pl.when(pl.program_id(2) == 0)
    def _():
        state_ref[...] = jnp.zeros_like(state_ref)

    wo_b_ref[...] = wo_ref[...].astype(BF16)
    wu_b_ref[...] = wu_ref[...].astype(BF16)
    wd_b_ref[...] = wd_ref[...].astype(BF16)

    t_blk = q_ref.shape[0]
    n_chunks = t_blk // CHUNK

    z = jnp.dot(zl_ref[...], wg_ref[0], preferred_element_type=F32) + gb_ref[0]
    log_a = jax.nn.log_sigmoid(z) * (1.0 / GATE_TAU)

    pos = lax.broadcasted_iota(jnp.int32, log_a.shape, 0) & (CHUNK - 1)
    bcum = log_a
    shift = 1
    while shift < CHUNK:
        bcum = bcum + jnp.where(pos >= shift, pltpu.roll(bcum, shift, axis=0), 0.0)
        shift *= 2

    q_dec = ((q_ref[...].astype(F32) * (HEAD_K ** -0.5)) * jnp.exp(bcum)).astype(BF16)
    kf = k_ref[...].astype(F32)
    k_inv = (kf * jnp.exp(-bcum)).astype(BF16)

    b_last_rows = [bcum[c * CHUNK + CHUNK - 1:c * CHUNK + CHUNK, :] for c in range(n_chunks)]
    pad = jnp.zeros((HEAD_K - n_chunks, HEAD_K), F32)
    decay_t = jnp.transpose(jnp.concatenate([jnp.exp(b) for b in b_last_rows] + [pad], axis=0))

    causal = (lax.broadcasted_iota(jnp.int32, (CHUNK, CHUNK), 0)
              >= lax.broadcasted_iota(jnp.int32, (CHUNK, CHUNK), 1))
    gain = g_ref[...]
    state = state_ref[...]
    for c in range(n_chunks):
        rows = slice(c * CHUNK, (c + 1) * CHUNK)
        qd = q_dec[rows]
        v_c = v_ref[rows, :]
        k_end = (kf[rows] * jnp.exp(b_last_rows[c] - bcum[rows])).astype(BF16)
        scores = lax.dot_general(qd, k_inv[rows], (((1,), (1,)), ((), ())),
                                 preferred_element_type=F32)
        scores = jnp.where(causal, scores, 0.0).astype(BF16)
        o = (jnp.dot(scores, v_c, preferred_element_type=F32)
             + jnp.dot(qd, state.astype(BF16), preferred_element_type=F32))
        delta = lax.dot_general(k_end, v_c, (((0,), (0,)), ((), ())),
                                preferred_element_type=F32)
        state = decay_t[:, c:c + 1] * state + delta
        ms = jnp.mean(o * o, axis=-1, keepdims=True)
        o_n = o * lax.rsqrt(ms + RMS_EPS) * gain
        r_c = r_ref[rows, :].astype(F32)
        o_ref[rows, :] = (o_n * (r_c * jax.nn.sigmoid(r_c))).astype(o_ref.dtype)
    state_ref[...] = state


def _gla_branch(proj, z_low, wg_b, gate_bias, gla_g, w_out, w_ff_up, w_ff_down, layer, bsz, seq):
    nblk = seq // GLA_T
    n_steps = bsz * GLA_HEADS * nblk
    row_blk = lambda b, h, t: b * nblk + t
    slab = lambda b, h, t: (layer, (b * GLA_HEADS + h) * nblk + t, 0)
    out_rows, up_rows, down_rows = D_MODEL // n_steps, D_MODEL // n_steps, D_FF // n_steps
    return pl.pallas_call(
        _gla_kernel,
        grid=(bsz, GLA_HEADS, nblk),
        in_specs=[pl.BlockSpec((GLA_T, HEAD_K), lambda b, h, t: (row_blk(b, h, t), Q_BLK0 + h)),
                  pl.BlockSpec((GLA_T, HEAD_K), lambda b, h, t: (row_blk(b, h, t), K_BLK0 + h)),
                  pl.BlockSpec((GLA_T, HEAD_V), lambda b, h, t: (row_blk(b, h, t), V_BLK0 + h)),
                  pl.BlockSpec((GLA_T, HEAD_V), lambda b, h, t: (row_blk(b, h, t), R_BLK0 + h)),
                  pl.BlockSpec((GLA_T, GATE_RANK), lambda b, h, t: (row_blk(b, h, t), 0)),
                  pl.BlockSpec((1, GATE_RANK, HEAD_K), lambda b, h, t: (h, 0, 0)),
                  pl.BlockSpec((1, 1, HEAD_K), lambda b, h, t: (h, 0, 0)),
                  pl.BlockSpec((1, HEAD_V), lambda b, h, t: (0, h)),
                  pl.BlockSpec((None, out_rows, D_MODEL), slab),
                  pl.BlockSpec((None, up_rows, D_FF), slab),
                  pl.BlockSpec((None, down_rows, D_MODEL), slab)],
        out_specs=[pl.BlockSpec((GLA_T, HEAD_V), lambda b, h, t: (row_blk(b, h, t), h)),
                   pl.BlockSpec((out_rows, D_MODEL), lambda b, h, t: slab(b, h, t)[1:]),
                   pl.BlockSpec((up_rows, D_FF), lambda b, h, t: slab(b, h, t)[1:]),
                   pl.BlockSpec((down_rows, D_MODEL), lambda b, h, t: slab(b, h, t)[1:])],
        out_shape=[jax.ShapeDtypeStruct((bsz * seq, D_GLA_V), BF16),
                   jax.ShapeDtypeStruct((D_MODEL, D_MODEL), BF16),
                   jax.ShapeDtypeStruct((D_MODEL, D_FF), BF16),
                   jax.ShapeDtypeStruct((D_FF, D_MODEL), BF16)],
        scratch_shapes=[pltpu.VMEM((HEAD_K, HEAD_V), F32)],
        compiler_params=pltpu.CompilerParams(
            dimension_semantics=("parallel", "parallel", "arbitrary"), vmem_limit_bytes=VMEM_LIMIT),
        name="gla_branch",
    )(proj, proj, proj, proj, z_low, wg_b, gate_bias, gla_g, w_out, w_ff_up, w_ff_down)


def _out_ln1_kernel(yc_ref, yg_ref, wc_ref, wg_ref, x_ref, g_ref, b_ref, o_ref):
    mix = (jnp.dot(yc_ref[...], wc_ref[...], preferred_element_type=F32)
           + jnp.dot(yg_ref[...], wg_ref[...], preferred_element_type=F32))
    o_ref[...] = _layer_norm(DN_ALPHA * x_ref[...] + mix, g_ref[...], b_ref[...]).astype(o_ref.dtype)


def _out_ln1(y_conv, y_gla, w_out_b, x2, ln_g, ln_b):
    m = x2.shape[0]
    return pl.pallas_call(
        _out_ln1_kernel,
        grid=(m // OUT_TM,),
        in_specs=[pl.BlockSpec((OUT_TM, D_CONV), lambda i: (i, 0)),
                  pl.BlockSpec((OUT_TM, D_GLA_V), lambda i: (i, 0)),
                  pl.BlockSpec((D_CONV, D_MODEL), lambda i: (0, 0)),
                  pl.BlockSpec((D_GLA_V, D_MODEL), lambda i: (1, 0)),
                  pl.BlockSpec((OUT_TM, D_MODEL), lambda i: (i, 0)),
                  pl.BlockSpec((1, D_MODEL), lambda i: (0, 0)),
                  pl.BlockSpec((1, D_MODEL), lambda i: (0, 0))],
        out_specs=pl.BlockSpec((OUT_TM, D_MODEL), lambda i: (i, 0)),
        out_shape=jax.ShapeDtypeStruct((m, D_MODEL), BF16),
        compiler_params=pltpu.CompilerParams(
            dimension_semantics=("parallel",), vmem_limit_bytes=VMEM_LIMIT),
        name="out_ln1",
    )(y_conv, y_gla, w_out_b, w_out_b, x2, ln_g, ln_b)


def _ffn_ln2_kernel(x_ref, wu_ref, wd_ref, g_ref, b_ref, o_ref):
    f = pl.program_id(1)
    h = jnp.dot(x_ref[...], wu_ref[...], preferred_element_type=F32)
    h = jnp.maximum(h, 0.0)
    h = (h * h).astype(BF16)

    @pl.when(f == 0)
    def _():
        o_ref[...] = jnp.dot(h, wd_ref[...], preferred_element_type=F32)

    @pl.when(f > 0)
    def _():
        o_ref[...] += jnp.dot(h, wd_ref[...], preferred_element_type=F32)

    @pl.when(f == pl.num_programs(1) - 1)
    def _():
        o_ref[...] = _layer_norm(DN_ALPHA * x_ref[...].astype(F32) + o_ref[...],
                                 g_ref[...], b_ref[...])


def _ffn_ln2(x1, w_up_b, w_down_b, ln_g, ln_b):
    m = x1.shape[0]
    return pl.pallas_call(
        _ffn_ln2_kernel,
        grid=(m // FFN_TM, D_FF // FFN_TF),
        in_specs=[pl.BlockSpec((FFN_TM, D_MODEL), lambda i, f: (i, 0)),
                  pl.BlockSpec((D_MODEL, FFN_TF), lambda i, f: (0, f)),
                  pl.BlockSpec((FFN_TF, D_MODEL), lambda i, f: (f, 0)),
                  pl.BlockSpec((1, D_MODEL), lambda i, f: (0, 0)),
                  pl.BlockSpec((1, D_MODEL), lambda i, f: (0, 0))],
        out_specs=pl.BlockSpec((FFN_TM, D_MODEL), lambda i, f: (i, 0)),
        out_shape=jax.ShapeDtypeStruct((m, D_MODEL), F32),
        compiler_params=pltpu.CompilerParams(
            dimension_semantics=("parallel", "arbitrary"), vmem_limit_bytes=VMEM_LIMIT),
        name="ffn_ln2",
    )(x1, w_up_b, w_down_b, ln_g, ln_b)


def kernel(x, w_in, conv_w, conv_norm_g, w_gate_up, gate_bias, gla_norm_g, w_out,
           ln1_g, ln1_b, w_ff_up, w_ff_down, ln2_g, ln2_b):
    bsz, seq, _ = x.shape
    assert seq % GLA_T == 0 and seq % CONV_TS == 0 and GLA_T % CHUNK == 0
    x2 = x.reshape(bsz * seq, D_MODEL)
    w_in_t = jnp.swapaxes(w_in, 1, 2)
    for l in range(DEPTH):
        wg_b = w_gate_up[l].astype(BF16).reshape(GATE_RANK, GLA_HEADS, HEAD_K).transpose(1, 0, 2)
        gb = gate_bias[l].reshape(GLA_HEADS, 1, HEAD_K)
        conv_w8 = jnp.pad(conv_w[l], ((0, 8 - conv_w.shape[1]), (0, 0)))

        proj, z_low = _in_proj(x2, w_in_t, l)
        y_conv = _conv_branch(proj, conv_w8, conv_norm_g[l].reshape(1, D_CONV), bsz, seq)
        y_gla, w_out_b, w_up_b, w_down_b = _gla_branch(
            proj, z_low, wg_b, gb, gla_norm_g[l].reshape(1, D_GLA_V),
            w_out, w_ff_up, w_ff_down, l, bsz, seq)
        x1 = _out_ln1(y_conv, y_gla, w_out_b, x2,
                      ln1_g[l].reshape(1, D_MODEL), ln1_b[l].reshape(1, D_MODEL))
        x2 = _ffn_ln2(x1, w_up_b, w_down_b,
                      ln2_g[l].reshape(1, D_MODEL), ln2_b[l].reshape(1, D_MODEL))
    return x2.reshape(bsz, seq, D_MODEL)
```

```python
import functools

import jax
import jax.numpy as jnp
from jax import lax
from jax.experimental import pallas as pl
from jax.experimental.pallas import tpu as pltpu

F32 = jnp.float32
BF16 = jnp.bfloat16

D_MODEL = 2048
D_CONV = 1024
CONV_GROUPS = 8
CONV_GROUP_WIDTH = D_CONV // CONV_GROUPS
GLA_HEADS = 4
HEAD_K = 128
HEAD_V = 256
D_GLA_K = GLA_HEADS * HEAD_K
D_GLA_V = GLA_HEADS * HEAD_V
GATE_RANK = 16
GATE_TAU = 16.0
CHUNK = 64
D_FF = 4 * D_MODEL
LN_EPS = 1e-5
RMS_EPS = 1e-6
DEPTH = 1
DN_ALPHA = (2.0 * DEPTH) ** 0.25

LANE = 128
D_PROJ_MAIN = 3 * D_CONV + 2 * D_GLA_K + 2 * D_GLA_V

Q_BLK0 = (3 * D_CONV) // HEAD_K
K_BLK0 = (3 * D_CONV + D_GLA_K) // HEAD_K
V_BLK0 = (3 * D_CONV + 2 * D_GLA_K) // HEAD_V
R_BLK0 = (3 * D_CONV + 2 * D_GLA_K + D_GLA_V) // HEAD_V

VMEM_LIMIT = 56 * 1024 * 1024

PROJ_TM = 1024
PROJ_TN = 768
CONV_TS = 512
GLA_T = 512
OUT_TM = 1024
ROW_SUB = 256
FFN_TM = 1024
FFN_TF = 1024


def _layer_norm(y, g, b):
    mu = jnp.mean(y, axis=-1, keepdims=True)
    yc = y - mu
    var = jnp.mean(yc * yc, axis=-1, keepdims=True)
    return yc * lax.rsqrt(var + LN_EPS) * g + b


_NT_DIMS = (((1,), (1,)), ((), ()))


def _in_proj_kernel(x_ref, w_ref, wz_ref, o_ref, z_ref, xb_ref):
    @pl.when(pl.program_id(1) == 0)
    def _():
        xb_ref[...] = x_ref[...].astype(BF16)
        z_ref[...] = lax.dot_general(xb_ref[...], wz_ref[...].astype(BF16), _NT_DIMS,
                                     preferred_element_type=F32).astype(z_ref.dtype)

    o_ref[...] = lax.dot_general(xb_ref[...], w_ref[...].astype(BF16), _NT_DIMS,
                                 preferred_element_type=F32).astype(o_ref.dtype)


def _in_proj(x2, w_in_t, layer):
    m = x2.shape[0]
    return pl.pallas_call(
        _in_proj_kernel,
        grid=(m // PROJ_TM, D_PROJ_MAIN // PROJ_TN),
        in_specs=[pl.BlockSpec((PROJ_TM, D_MODEL), lambda i, j: (i, 0)),
                  pl.BlockSpec((None, PROJ_TN, D_MODEL), lambda i, j: (layer, j, 0)),
                  pl.BlockSpec((None, GATE_RANK, D_MODEL),
                               lambda i, j: (layer, D_PROJ_MAIN // GATE_RANK, 0))],
        out_specs=[pl.BlockSpec((PROJ_TM, PROJ_TN), lambda i, j: (i, j)),
                   pl.BlockSpec((PROJ_TM, GATE_RANK), lambda i, j: (i, 0))],
        out_shape=[jax.ShapeDtypeStruct((m, D_PROJ_MAIN), BF16),
                   jax.ShapeDtypeStruct((m, GATE_RANK), BF16)],
        scratch_shapes=[pltpu.VMEM((PROJ_TM, D_MODEL), BF16)],
        compiler_params=pltpu.CompilerParams(
            dimension_semantics=("parallel", "arbitrary"), vmem_limit_bytes=VMEM_LIMIT),
        name="in_proj",
    )(x2, w_in_t, w_in_t)


def _conv_kernel(b_ref, c_ref, u_ref, w_ref, g_ref, o_ref, carry_ref):
    @pl.when(pl.program_id(1) == 0)
    def _():
        carry_ref[...] = jnp.zeros_like(carry_ref)

    ts = b_ref.shape[0]
    h = c_ref[...].astype(F32) * u_ref[...].astype(F32)
    prev = carry_ref[...]
    c1 = prev[7:8, :]
    c2 = prev[6:7, :]
    row = lax.broadcasted_iota(jnp.int32, h.shape, 0)
    h1 = jnp.where(row == 0, c1, pltpu.roll(h, 1, axis=0))
    h2 = jnp.where(row == 0, c2, jnp.where(row == 1, c1, pltpu.roll(h, 2, axis=0)))
    carry_ref[...] = h[ts - 8:, :]
    w = w_ref[...]
    y = b_ref[...].astype(F32) * (w[2:3, :] * h + w[1:2, :] * h1 + w[0:1, :] * h2)
    g = g_ref[...]
    for grp in range(CONV_GROUPS):
        sl = slice(grp * CONV_GROUP_WIDTH, (grp + 1) * CONV_GROUP_WIDTH)
        yg = y[:, sl]
        ms = jnp.mean(yg * yg, axis=-1, keepdims=True)
        o_ref[:, sl] = (yg * lax.rsqrt(ms + RMS_EPS) * g[:, sl]).astype(o_ref.dtype)


def _conv_branch(proj, conv_w8, conv_g, bsz, seq):
    nblk = seq // CONV_TS
    row_blk = lambda b, t: b * nblk + t
    return pl.pallas_call(
        _conv_kernel,
        grid=(bsz, nblk),
        in_specs=[pl.BlockSpec((CONV_TS, D_CONV), lambda b, t: (row_blk(b, t), 0)),
                  pl.BlockSpec((CONV_TS, D_CONV), lambda b, t: (row_blk(b, t), 1)),
                  pl.BlockSpec((CONV_TS, D_CONV), lambda b, t: (row_blk(b, t), 2)),
                  pl.BlockSpec((8, D_CONV), lambda b, t: (0, 0)),
                  pl.BlockSpec((1, D_CONV), lambda b, t: (0, 0))],
        out_specs=pl.BlockSpec((CONV_TS, D_CONV), lambda b, t: (row_blk(b, t), 0)),
        out_shape=jax.ShapeDtypeStruct((bsz * seq, D_CONV), BF16),
        scratch_shapes=[pltpu.VMEM((8, D_CONV), F32)],
        compiler_params=pltpu.CompilerParams(
            dimension_semantics=("parallel", "arbitrary"), vmem_limit_bytes=VMEM_LIMIT),
        name="conv_branch",
    )(proj, proj, proj, conv_w8, conv_g)


def _gla_kernel(q_ref, k_ref, v_ref, r_ref, zl_ref, wg_ref, gb_ref, g_ref,
                wo_ref, wu_ref, wd_ref, o_ref, wo_b_ref, wu_b_ref, wd_b_ref, state_ref):
    @pl.when(pl.program_id(2) == 0)
    def _():
        state_ref[...] = jnp.zeros_like(state_ref)

    wo_b_ref[...] = wo_ref[...].astype(BF16)
    wu_b_ref[...] = wu_ref[...].astype(BF16)
    wd_b_ref[...] = wd_ref[...].astype(BF16)

    t_blk = q_ref.shape[0]
    n_chunks = t_blk // CHUNK

    z = jnp.dot(zl_ref[...], wg_ref[0], preferred_element_type=F32) + gb_ref[0]
    log_a = jax.nn.log_sigmoid(z) * (1.0 / GATE_TAU)

    pos = lax.broadcasted_iota(jnp.int32, log_a.shape, 0) & (CHUNK - 1)
    bcum = log_a
    shift = 1
    while shift < CHUNK:
        bcum = bcum + jnp.where(pos >= shift, pltpu.roll(bcum, shift, axis=0), 0.0)
        shift *= 2

    q_dec = ((q_ref[...].astype(F32) * (HEAD_K ** -0.5)) * jnp.exp(bcum)).astype(BF16)
    kf = k_ref[...].astype(F32)
    k_inv = (kf * jnp.exp(-bcum)).astype(BF16)

    b_last_rows = [bcum[c * CHUNK + CHUNK - 1:c * CHUNK + CHUNK, :] for c in range(n_chunks)]
    pad = jnp.zeros((HEAD_K - n_chunks, HEAD_K), F32)
    decay_t = jnp.transpose(jnp.concatenate([jnp.exp(b) for b in b_last_rows] + [pad], axis=0))

    causal = (lax.broadcasted_iota(jnp.int32, (CHUNK, CHUNK), 0)
              >= lax.broadcasted_iota(jnp.int32, (CHUNK, CHUNK), 1))
    gain = g_ref[...]
    state = state_ref[...]
    for c in range(n_chunks):
        rows = slice(c * CHUNK, (c + 1) * CHUNK)
        qd = q_dec[rows]
        v_c = v_ref[rows, :]
        k_end = (kf[rows] * jnp.exp(b_last_rows[c] - bcum[rows])).astype(BF16)
        scores = lax.dot_general(qd, k_inv[rows], (((1,), (1,)), ((), ())),
                                 preferred_element_type=F32)
        scores = jnp.where(causal, scores, 0.0).astype(BF16)
        o = (jnp.dot(scores, v_c, preferred_element_type=F32)
             + jnp.dot(qd, state.astype(BF16), preferred_element_type=F32))
        delta = lax.dot_general(k_end, v_c, (((0,), (0,)), ((), ())),
                                preferred_element_type=F32)
        state = decay_t[:, c:c + 1] * state + delta
        ms = jnp.mean(o * o, axis=-1, keepdims=True)
        o_n = o * lax.rsqrt(ms + RMS_EPS) * gain
        r_c = r_ref[rows, :].astype(F32)
        o_ref[rows, :] = (o_n * (r_c * jax.nn.sigmoid(r_c))).astype(o_ref.dtype)
    state_ref[...] = state


def _gla_branch(proj, z_low, wg_b, gate_bias, gla_g, w_out, w_ff_up, w_ff_down, layer, bsz, seq):
    nblk = seq // GLA_T
    n_steps = bsz * GLA_HEADS * nblk
    row_blk = lambda b, h, t: b * nblk + t
    slab = lambda b, h, t: (layer, (b * GLA_HEADS + h) * nblk + t, 0)
    out_rows, up_rows, down_rows = D_MODEL // n_steps, D_MODEL // n_steps, D_FF // n_steps
    return pl.pallas_call(
        _gla_kernel,
        grid=(bsz, GLA_HEADS, nblk),
        in_specs=[pl.BlockSpec((GLA_T, HEAD_K), lambda b, h, t: (row_blk(b, h, t), Q_BLK0 + h)),
                  pl.BlockSpec((GLA_T, HEAD_K), lambda b, h, t: (row_blk(b, h, t), K_BLK0 + h)),
                  pl.BlockSpec((GLA_T, HEAD_V), lambda b, h, t: (row_blk(b, h, t), V_BLK0 + h)),
                  pl.BlockSpec((GLA_T, HEAD_V), lambda b, h, t: (row_blk(b, h, t), R_BLK0 + h)),
                  pl.BlockSpec((GLA_T, GATE_RANK), lambda b, h, t: (row_blk(b, h, t), 0)),
                  pl.BlockSpec((1, GATE_RANK, HEAD_K), lambda b, h, t: (h, 0, 0)),
                  pl.BlockSpec((1, 1, HEAD_K), lambda b, h, t: (h, 0, 0)),
                  pl.BlockSpec((1, HEAD_V), lambda b, h, t: (0, h)),
                  pl.BlockSpec((None, out_rows, D_MODEL), slab),
                  pl.BlockSpec((None, up_rows, D_FF), slab),
                  pl.BlockSpec((None, down_rows, D_MODEL), slab)],
        out_specs=[pl.BlockSpec((GLA_T, HEAD_V), lambda b, h, t: (row_blk(b, h, t), h)),
                   pl.BlockSpec((out_rows, D_MODEL), lambda b, h, t: slab(b, h, t)[1:]),
                   pl.BlockSpec((up_rows, D_FF), lambda b, h, t: slab(b, h, t)[1:]),
                   pl.BlockSpec((down_rows, D_MODEL), lambda b, h, t: slab(b, h, t)[1:])],
        out_shape=[jax.ShapeDtypeStruct((bsz * seq, D_GLA_V), BF16),
                   jax.ShapeDtypeStruct((D_MODEL, D_MODEL), BF16),
                   jax.ShapeDtypeStruct((D_MODEL, D_FF), BF16),
                   jax.ShapeDtypeStruct((D_FF, D_MODEL), BF16)],
        scratch_shapes=[pltpu.VMEM((HEAD_K, HEAD_V), F32)],
        compiler_params=pltpu.CompilerParams(
            dimension_semantics=("parallel", "parallel", "arbitrary"), vmem_limit_bytes=VMEM_LIMIT),
        name="gla_branch",
    )(proj, proj, proj, proj, z_low, wg_b, gate_bias, gla_g, w_out, w_ff_up, w_ff_down)


def _out_ln1_kernel(yc_ref, yg_ref, w_ref, x_ref, g_ref, b_ref, o_ref):
    for s in range(o_ref.shape[0] // ROW_SUB):
        rows = slice(s * ROW_SUB, (s + 1) * ROW_SUB)
        y = jnp.concatenate([yc_ref[rows, :], yg_ref[rows, :]], axis=1)
        mix = jnp.dot(y, w_ref[...], preferred_element_type=F32)
        o_ref[rows, :] = _layer_norm(DN_ALPHA * x_ref[rows, :] + mix,
                                     g_ref[...], b_ref[...]).astype(o_ref.dtype)


def _out_ln1(y_conv, y_gla, w_out_b, x2, ln_g, ln_b):
    m = x2.shape[0]
    return pl.pallas_call(
        _out_ln1_kernel,
        grid=(m // OUT_TM,),
        in_specs=[pl.BlockSpec((OUT_TM, D_CONV), lambda i: (i, 0)),
                  pl.BlockSpec((OUT_TM, D_GLA_V), lambda i: (i, 0)),
                  pl.BlockSpec((D_CONV + D_GLA_V, D_MODEL), lambda i: (0, 0)),
                  pl.BlockSpec((OUT_TM, D_MODEL), lambda i: (i, 0)),
                  pl.BlockSpec((1, D_MODEL), lambda i: (0, 0)),
                  pl.BlockSpec((1, D_MODEL), lambda i: (0, 0))],
        out_specs=pl.BlockSpec((OUT_TM, D_MODEL), lambda i: (i, 0)),
        out_shape=jax.ShapeDtypeStruct((m, D_MODEL), BF16),
        compiler_params=pltpu.CompilerParams(
            dimension_semantics=("parallel",), vmem_limit_bytes=VMEM_LIMIT),
        name="out_ln1",
    )(y_conv, y_gla, w_out_b, x2, ln_g, ln_b)


def _ffn_ln2_kernel(x_ref, wu_ref, wd_ref, g_ref, b_ref, o_ref):
    f = pl.program_id(1)
    h = jnp.dot(x_ref[...], wu_ref[...], preferred_element_type=F32)
    h = jnp.maximum(h, 0.0)
    h = (h * h).astype(BF16)

    @pl.when(f == 0)
    def _():
        o_ref[...] = jnp.dot(h, wd_ref[...], preferred_element_type=F32)

    last = pl.num_programs(1) - 1

    @pl.when(jnp.logical_and(f > 0, f < last))
    def _():
        o_ref[...] += jnp.dot(h, wd_ref[...], preferred_element_type=F32)

    @pl.when(f == last)
    def _():
        for s in range(o_ref.shape[0] // ROW_SUB):
            rows = slice(s * ROW_SUB, (s + 1) * ROW_SUB)
            ff = o_ref[rows, :] + jnp.dot(h[rows, :], wd_ref[...], preferred_element_type=F32)
            o_ref[rows, :] = _layer_norm(DN_ALPHA * x_ref[rows, :].astype(F32) + ff,
                                         g_ref[...], b_ref[...])


def _ffn_ln2(x1, w_up_b, w_down_b, ln_g, ln_b):
    m = x1.shape[0]
    return pl.pallas_call(
        _ffn_ln2_kernel,
        grid=(m // FFN_TM, D_FF // FFN_TF),
        in_specs=[pl.BlockSpec((FFN_TM, D_MODEL), lambda i, f: (i, 0)),
                  pl.BlockSpec((D_MODEL, FFN_TF), lambda i, f: (0, f)),
                  pl.BlockSpec((FFN_TF, D_MODEL), lambda i, f: (f, 0)),
                  pl.BlockSpec((1, D_MODEL), lambda i, f: (0, 0)),
                  pl.BlockSpec((1, D_MODEL), lambda i, f: (0, 0))],
        out_specs=pl.BlockSpec((FFN_TM, D_MODEL), lambda i, f: (i, 0)),
        out_shape=jax.ShapeDtypeStruct((m, D_MODEL), F32),
        compiler_params=pltpu.CompilerParams(
            dimension_semantics=("parallel", "arbitrary"), vmem_limit_bytes=VMEM_LIMIT),
        name="ffn_ln2",
    )(x1, w_up_b, w_down_b, ln_g, ln_b)


def kernel(x, w_in, conv_w, conv_norm_g, w_gate_up, gate_bias, gla_norm_g, w_out,
           ln1_g, ln1_b, w_ff_up, w_ff_down, ln2_g, ln2_b):
    bsz, seq, _ = x.shape
    assert seq % GLA_T == 0 and seq % CONV_TS == 0 and GLA_T % CHUNK == 0
    x2 = x.reshape(bsz * seq, D_MODEL)
    w_in_t = jnp.swapaxes(w_in, 1, 2)
    for l in range(DEPTH):
        wg_b = w_gate_up[l].astype(BF16).reshape(GATE_RANK, GLA_HEADS, HEAD_K).transpose(1, 0, 2)
        gb = gate_bias[l].reshape(GLA_HEADS, 1, HEAD_K)
        conv_w8 = jnp.pad(conv_w[l], ((0, 8 - conv_w.shape[1]), (0, 0)))

        proj, z_low = _in_proj(x2, w_in_t, l)
        y_conv = _conv_branch(proj, conv_w8, conv_norm_g[l].reshape(1, D_CONV), bsz, seq)
        y_gla, w_out_b, w_up_b, w_down_b = _gla_branch(
            proj, z_low, wg_b, gb, gla_norm_g[l].reshape(1, D_GLA_V),
            w_out, w_ff_up, w_ff_down, l, bsz, seq)
        x1 = _out_ln1(y_conv, y_gla, w_out_b, x2,
                      ln1_g[l].reshape(1, D_MODEL), ln1_b[l].reshape(1, D_MODEL))
        x2 = _ffn_ln2(x1, w_up_b, w_down_b,
                      ln2_g[l].reshape(1, D_MODEL), ln2_b[l].reshape(1, D_MODEL))
    return x2.reshape(bsz, seq, D_MODEL)
```

```python
import functools

import jax
import jax.numpy as jnp
from jax import lax
from jax.experimental import pallas as pl
from jax.experimental.pallas import tpu as pltpu

F32 = jnp.float32
BF16 = jnp.bfloat16

D_MODEL = 2048
D_CONV = 1024
CONV_GROUPS = 8
CONV_GROUP_WIDTH = D_CONV // CONV_GROUPS
GLA_HEADS = 4
HEAD_K = 128
HEAD_V = 256
D_GLA_K = GLA_HEADS * HEAD_K
D_GLA_V = GLA_HEADS * HEAD_V
GATE_RANK = 16
GATE_TAU = 16.0
CHUNK = 64
D_FF = 4 * D_MODEL
LN_EPS = 1e-5
RMS_EPS = 1e-6
DEPTH = 1
DN_ALPHA = (2.0 * DEPTH) ** 0.25

D_PROJ_MAIN = 3 * D_CONV + 2 * D_GLA_K + 2 * D_GLA_V

COL_B, COL_C, COL_U = 0, D_CONV, 2 * D_CONV
COL_Q = 3 * D_CONV
COL_K = COL_Q + D_GLA_K
COL_V = COL_K + D_GLA_K
COL_R = COL_V + D_GLA_V

VMEM_LIMIT = 56 * 1024 * 1024
SUBLANES = 8

PROJ_TM = 1024
PROJ_TN = 768
MIX_T = 512
ROW_SUB = 256
OUT_TN = 256
MIX_PER_FINISH = 2
FFN_TM = 1024
FFN_TF = 512

_NT_DIMS = (((1,), (1,)), ((), ()))
_TN_DIMS = (((0,), (0,)), ((), ()))


def _layer_norm(y, g, b):
    mu = jnp.mean(y, axis=-1, keepdims=True)
    yc = y - mu
    var = jnp.mean(yc * yc, axis=-1, keepdims=True)
    return yc * lax.rsqrt(var + LN_EPS) * g + b


def _in_proj_kernel(x_ref, w_ref, wz_ref, wo_ref, o_ref, z_ref, wo_b_ref, xb_ref):
    @pl.when(pl.program_id(1) == 0)
    def _():
        xb_ref[...] = x_ref[...].astype(BF16)
        z_ref[...] = lax.dot_general(xb_ref[...], wz_ref[...].astype(BF16), _NT_DIMS,
                                     preferred_element_type=F32).astype(z_ref.dtype)

    wo_b_ref[...] = wo_ref[...].astype(BF16)
    o_ref[...] = lax.dot_general(xb_ref[...], w_ref[...].astype(BF16), _NT_DIMS,
                                 preferred_element_type=F32).astype(o_ref.dtype)


def _in_proj(x2, w_in_t, w_out, layer):
    m = x2.shape[0]
    n_i, n_j = m // PROJ_TM, D_PROJ_MAIN // PROJ_TN
    slab_rows = (D_CONV + D_GLA_V) // (n_i * n_j)
    return pl.pallas_call(
        _in_proj_kernel,
        grid=(n_i, n_j),
        in_specs=[pl.BlockSpec((PROJ_TM, D_MODEL), lambda i, j: (i, 0)),
                  pl.BlockSpec((None, PROJ_TN, D_MODEL), lambda i, j: (layer, j, 0)),
                  pl.BlockSpec((None, GATE_RANK, D_MODEL),
                               lambda i, j: (layer, D_PROJ_MAIN // GATE_RANK, 0)),
                  pl.BlockSpec((None, slab_rows, D_MODEL), lambda i, j: (layer, i * n_j + j, 0))],
        out_specs=[pl.BlockSpec((PROJ_TM, PROJ_TN), lambda i, j: (i, j)),
                   pl.BlockSpec((PROJ_TM, GATE_RANK), lambda i, j: (i, 0)),
                   pl.BlockSpec((slab_rows, D_MODEL), lambda i, j: (i * n_j + j, 0))],
        out_shape=[jax.ShapeDtypeStruct((m, D_PROJ_MAIN), BF16),
                   jax.ShapeDtypeStruct((m, GATE_RANK), BF16),
                   jax.ShapeDtypeStruct((D_CONV + D_GLA_V, D_MODEL), BF16)],
        scratch_shapes=[pltpu.VMEM((PROJ_TM, D_MODEL), BF16)],
        compiler_params=pltpu.CompilerParams(
            dimension_semantics=("parallel", "arbitrary"), vmem_limit_bytes=VMEM_LIMIT),
        name="in_proj",
    )(x2, w_in_t, w_in_t, w_out)


def _shift_rows(h3, carry3, shift):
    pos = lax.broadcasted_iota(jnp.int32, h3.shape, 1)
    rolled = pltpu.roll(h3, shift, axis=1)
    rolled_prev = jnp.concatenate([pltpu.roll(carry3, shift, axis=1), rolled[:-1]], axis=0)
    return jnp.where(pos < shift, rolled_prev, rolled)


def _conv_piece(grp, proj_ref, w_ref, g_ref, carry_ref, y_ref):
    ts = proj_ref.shape[0]
    lo = grp * CONV_GROUP_WIDTH
    sl = slice(lo, lo + CONV_GROUP_WIDTH)
    h = (proj_ref[:, COL_C + lo:COL_C + lo + CONV_GROUP_WIDTH].astype(F32)
         * proj_ref[:, COL_U + lo:COL_U + lo + CONV_GROUP_WIDTH].astype(F32))
    h3 = h.reshape(ts // SUBLANES, SUBLANES, CONV_GROUP_WIDTH)
    carry3 = carry_ref[:, sl].reshape(1, SUBLANES, CONV_GROUP_WIDTH)
    h1 = _shift_rows(h3, carry3, 1).reshape(h.shape)
    h2 = _shift_rows(h3, carry3, 2).reshape(h.shape)
    carry_ref[:, sl] = h[ts - SUBLANES:, :]
    w = w_ref[:, sl]
    y = (proj_ref[:, COL_B + lo:COL_B + lo + CONV_GROUP_WIDTH].astype(F32)
         * (w[2:3, :] * h + w[1:2, :] * h1 + w[0:1, :] * h2))
    ms = jnp.mean(y * y, axis=-1, keepdims=True)
    y_ref[:, sl] = (y * lax.rsqrt(ms + RMS_EPS) * g_ref[:, sl]).astype(y_ref.dtype)


def _chunk_cumsum(x):
    t_blk, width = x.shape
    x3 = x.reshape(t_blk // SUBLANES, SUBLANES, width)
    pos = lax.broadcasted_iota(jnp.int32, x3.shape, 1)
    shift = 1
    while shift < SUBLANES:
        x3 = x3 + jnp.where(pos >= shift, pltpu.roll(x3, shift, axis=1), 0.0)
        shift *= 2
    vregs_per_chunk = CHUNK // SUBLANES
    x4 = x3.reshape(t_blk // CHUNK, vregs_per_chunk, SUBLANES, width)
    outs = [x4[:, 0]]
    for j in range(1, vregs_per_chunk):
        outs.append(x4[:, j] + outs[-1][:, SUBLANES - 1:SUBLANES, :])
    return jnp.stack(outs, axis=1).reshape(t_blk, width)


def _gla_head_pieces(head, proj_ref, zl_ref, wg_ref, gb_ref, g_ref, state_ref, y_ref):
    t_blk = proj_ref.shape[0]
    n_chunks = t_blk // CHUNK
    col_q, col_k = COL_Q + head * HEAD_K, COL_K + head * HEAD_K
    col_v, col_r = COL_V + head * HEAD_V, COL_R + head * HEAD_V
    col_y = D_CONV + head * HEAD_V
    pre = {}

    def prepare():
        z = jnp.dot(zl_ref[...], wg_ref[head], preferred_element_type=F32) + gb_ref[head]
        bcum = _chunk_cumsum(jax.nn.log_sigmoid(z) * (1.0 / GATE_TAU))
        pre["q_dec"] = ((proj_ref[:, col_q:col_q + HEAD_K].astype(F32) * (HEAD_K ** -0.5))
                        * jnp.exp(bcum)).astype(BF16)
        k_inv_f = proj_ref[:, col_k:col_k + HEAD_K].astype(F32) * jnp.exp(-bcum)
        pre["k_inv"] = k_inv_f.astype(BF16)
        decay = jnp.exp(bcum.reshape(n_chunks, CHUNK, HEAD_K)[:, CHUNK - 1:CHUNK, :])
        pre["k_end"] = (k_inv_f.reshape(n_chunks, CHUNK, HEAD_K) * decay).astype(BF16).reshape(
            t_blk, HEAD_K)
        pad = jnp.zeros((HEAD_K - n_chunks, HEAD_K), F32)
        pre["decay_t"] = jnp.transpose(jnp.concatenate([decay.reshape(n_chunks, HEAD_K), pad], axis=0))

    def chunk(c):
        rows = slice(c * CHUNK, (c + 1) * CHUNK)
        causal = (lax.broadcasted_iota(jnp.int32, (CHUNK, CHUNK), 0)
                  >= lax.broadcasted_iota(jnp.int32, (CHUNK, CHUNK), 1))
        qd = pre["q_dec"][rows]
        v_c = proj_ref[rows, col_v:col_v + HEAD_V]
        state = state_ref[head]
        scores = lax.dot_general(qd, pre["k_inv"][rows], _NT_DIMS, preferred_element_type=F32)
        scores = jnp.where(causal, scores, 0.0).astype(BF16)
        o = (jnp.dot(scores, v_c, preferred_element_type=F32)
             + jnp.dot(qd, state.astype(BF16), preferred_element_type=F32))
        delta = lax.dot_general(pre["k_end"][rows], v_c, _TN_DIMS, preferred_element_type=F32)
        state_ref[head] = pre["decay_t"][:, c:c + 1] * state + delta
        ms = jnp.mean(o * o, axis=-1, keepdims=True)
        o_n = o * lax.rsqrt(ms + RMS_EPS) * g_ref[:, head * HEAD_V:(head + 1) * HEAD_V]
        r_c = proj_ref[rows, col_r:col_r + HEAD_V].astype(F32)
        y_ref[rows, col_y:col_y + HEAD_V] = (o_n * (r_c * jax.nn.sigmoid(r_c))).astype(y_ref.dtype)

    return [prepare] + [functools.partial(chunk, c) for c in range(n_chunks)]


def _mixer_kernel(proj_ref, zl_ref, x_ref, wo_ref, cw_ref, cg_ref, wg_ref, gb_ref, gg_ref,
                  ln_g_ref, ln_b_ref, o_ref, y_ref, yp_ref, mix_ref, state_ref, carry_ref,
                  *, blocks_per_seq):
    s = pl.program_id(0)

    @pl.when(s == 0)
    def _():
        y_ref[...] = jnp.zeros_like(y_ref)

    @pl.when(s % blocks_per_seq == 0)
    def _():
        state_ref[...] = jnp.zeros_like(state_ref)
        carry_ref[...] = jnp.zeros_like(carry_ref)

    yp_ref[...] = y_ref[...]

    mix_pieces = [functools.partial(_conv_piece, grp, proj_ref, cw_ref, cg_ref, carry_ref, y_ref)
                  for grp in range(CONV_GROUPS)]
    for head in range(GLA_HEADS):
        mix_pieces += _gla_head_pieces(head, proj_ref, zl_ref, wg_ref, gb_ref, gg_ref, state_ref, y_ref)

    def out_proj_tile(sub, n):
        rows = slice(sub * ROW_SUB, (sub + 1) * ROW_SUB)
        cols = slice(n * OUT_TN, (n + 1) * OUT_TN)
        mix_ref[rows, cols] = jnp.dot(yp_ref[rows, :], wo_ref[:, cols], preferred_element_type=F32)

    def layer_norm_rows(sub):
        rows = slice(sub * ROW_SUB, (sub + 1) * ROW_SUB)
        o_ref[rows, :] = _layer_norm(DN_ALPHA * x_ref[rows, :] + mix_ref[rows, :],
                                     ln_g_ref[...], ln_b_ref[...]).astype(o_ref.dtype)

    finish_pieces = []
    for sub in range(o_ref.shape[0] // ROW_SUB):
        finish_pieces += [functools.partial(out_proj_tile, sub, n) for n in range(D_MODEL // OUT_TN)]
        finish_pieces.append(functools.partial(layer_norm_rows, sub))

    while mix_pieces or finish_pieces:
        for piece in mix_pieces[:MIX_PER_FINISH]:
            piece()
        del mix_pieces[:MIX_PER_FINISH]
        if finish_pieces:
            finish_pieces.pop(0)()


def _mixer(proj, z_low, x2, w_out_b, conv_w8, conv_g, wg_b, gate_bias, gla_g, ln_g, ln_b, bsz, seq):
    m = x2.shape[0]
    n_blk = m // MIX_T
    cur_blk = lambda s: (jnp.minimum(s, n_blk - 1), 0)
    prev_blk = lambda s: (jnp.maximum(s - 1, 0), 0)
    const2 = lambda s: (0, 0)
    const3 = lambda s: (0, 0, 0)
    return pl.pallas_call(
        functools.partial(_mixer_kernel, blocks_per_seq=seq // MIX_T),
        grid=(n_blk + 1,),
        in_specs=[pl.BlockSpec((MIX_T, D_PROJ_MAIN), cur_blk),
                  pl.BlockSpec((MIX_T, GATE_RANK), cur_blk),
                  pl.BlockSpec((MIX_T, D_MODEL), prev_blk),
                  pl.BlockSpec((D_CONV + D_GLA_V, D_MODEL), const2),
                  pl.BlockSpec((8, D_CONV), const2),
                  pl.BlockSpec((1, D_CONV), const2),
                  pl.BlockSpec((GLA_HEADS, GATE_RANK, HEAD_K), const3),
                  pl.BlockSpec((GLA_HEADS, 1, HEAD_K), const3),
                  pl.BlockSpec((1, D_GLA_V), const2),
                  pl.BlockSpec((1, D_MODEL), const2),
                  pl.BlockSpec((1, D_MODEL), const2)],
        out_specs=pl.BlockSpec((MIX_T, D_MODEL), prev_blk),
        out_shape=jax.ShapeDtypeStruct((m, D_MODEL), BF16),
        scratch_shapes=[pltpu.VMEM((MIX_T, D_CONV + D_GLA_V), BF16),
                        pltpu.VMEM((MIX_T, D_CONV + D_GLA_V), BF16),
                        pltpu.VMEM((MIX_T, D_MODEL), F32),
                        pltpu.VMEM((GLA_HEADS, HEAD_K, HEAD_V), F32),
                        pltpu.VMEM((8, D_CONV), F32)],
        compiler_params=pltpu.CompilerParams(
            dimension_semantics=("arbitrary",), vmem_limit_bytes=VMEM_LIMIT),
        name="mixer",
    )(proj, z_low, x2, w_out_b, conv_w8, conv_g, wg_b, gate_bias, gla_g, ln_g, ln_b)


def _ffn_ln2_kernel(x_ref, wu_ref, wd_ref, g_ref, b_ref, o_ref):
    f = pl.program_id(1)
    h = jnp.dot(x_ref[...], wu_ref[...].astype(BF16), preferred_element_type=F32)
    h = jnp.maximum(h, 0.0)
    h = (h * h).astype(BF16)

    @pl.when(f == 0)
    def _():
        o_ref[...] = jnp.dot(h, wd_ref[...].astype(BF16), preferred_element_type=F32)

    last = pl.num_programs(1) - 1

    @pl.when(jnp.logical_and(f > 0, f < last))
    def _():
        o_ref[...] += jnp.dot(h, wd_ref[...].astype(BF16), preferred_element_type=F32)

    @pl.when(f == last)
    def _():
        wd = wd_ref[...].astype(BF16)
        for s in range(o_ref.shape[0] // ROW_SUB):
            rows = slice(s * ROW_SUB, (s + 1) * ROW_SUB)
            ff = o_ref[rows, :] + jnp.dot(h[rows, :], wd, preferred_element_type=F32)
            o_ref[rows, :] = _layer_norm(DN_ALPHA * x_ref[rows, :].astype(F32) + ff,
                                         g_ref[...], b_ref[...])


def _ffn_ln2(x1, w_ff_up, w_ff_down, ln_g, ln_b, layer):
    m = x1.shape[0]
    return pl.pallas_call(
        _ffn_ln2_kernel,
        grid=(m // FFN_TM, D_FF // FFN_TF),
        in_specs=[pl.BlockSpec((FFN_TM, D_MODEL), lambda i, f: (i, 0)),
                  pl.BlockSpec((None, D_MODEL, FFN_TF), lambda i, f: (layer, 0, f)),
                  pl.BlockSpec((None, FFN_TF, D_MODEL), lambda i, f: (layer, f, 0)),
                  pl.BlockSpec((1, D_MODEL), lambda i, f: (0, 0)),
                  pl.BlockSpec((1, D_MODEL), lambda i, f: (0, 0))],
        out_specs=pl.BlockSpec((FFN_TM, D_MODEL), lambda i, f: (i, 0)),
        out_shape=jax.ShapeDtypeStruct((m, D_MODEL), F32),
        compiler_params=pltpu.CompilerParams(
            dimension_semantics=("parallel", "arbitrary"), vmem_limit_bytes=VMEM_LIMIT),
        name="ffn_ln2",
    )(x1, w_ff_up, w_ff_down, ln_g, ln_b)


def kernel(x, w_in, conv_w, conv_norm_g, w_gate_up, gate_bias, gla_norm_g, w_out,
           ln1_g, ln1_b, w_ff_up, w_ff_down, ln2_g, ln2_b):
    bsz, seq, _ = x.shape
    assert seq % MIX_T == 0 and MIX_T % CHUNK == 0 and MIX_T % ROW_SUB == 0
    x2 = x.reshape(bsz * seq, D_MODEL)
    w_in_t = jnp.swapaxes(w_in, 1, 2)
    for l in range(DEPTH):
        wg_b = w_gate_up[l].astype(BF16).reshape(GATE_RANK, GLA_HEADS, HEAD_K).transpose(1, 0, 2)
        gb = gate_bias[l].reshape(GLA_HEADS, 1, HEAD_K)
        conv_w8 = jnp.pad(conv_w[l], ((0, 8 - conv_w.shape[1]), (0, 0)))

        proj, z_low, w_out_b = _in_proj(x2, w_in_t, w_out, l)
        x1 = _mixer(proj, z_low, x2, w_out_b, conv_w8, conv_norm_g[l].reshape(1, D_CONV),
                    wg_b, gb, gla_norm_g[l].reshape(1, D_GLA_V),
                    ln1_g[l].reshape(1, D_MODEL), ln1_b[l].reshape(1, D_MODEL), bsz, seq)
        x2 = _ffn_ln2(x1, w_ff_up, w_ff_down,
                      ln2_g[l].reshape(1, D_MODEL), ln2_b[l].reshape(1, D_MODEL), l)
    return x2.reshape(bsz, seq, D_MODEL)
```

```python
import functools

import jax
import jax.numpy as jnp
from jax import lax
from jax.experimental import pallas as pl
from jax.experimental.pallas import tpu as pltpu

F32 = jnp.float32
BF16 = jnp.bfloat16

D_MODEL = 2048
D_CONV = 1024
CONV_GROUPS = 8
CONV_GROUP_WIDTH = D_CONV // CONV_GROUPS
GLA_HEADS = 4
HEAD_K = 128
HEAD_V = 256
D_GLA_K = GLA_HEADS * HEAD_K
D_GLA_V = GLA_HEADS * HEAD_V
GATE_RANK = 16
GATE_TAU = 16.0
CHUNK = 64
D_FF = 4 * D_MODEL
LN_EPS = 1e-5
RMS_EPS = 1e-6
DEPTH = 1
DN_ALPHA = (2.0 * DEPTH) ** 0.25

D_PROJ_MAIN = 3 * D_CONV + 2 * D_GLA_K + 2 * D_GLA_V

COL_B, COL_C, COL_U = 0, D_CONV, 2 * D_CONV
COL_Q = 3 * D_CONV
COL_K = COL_Q + D_GLA_K
COL_V = COL_K + D_GLA_K
COL_R = COL_V + D_GLA_V

VMEM_LIMIT = 56 * 1024 * 1024
SUBLANES = 8

PROJ_TM = 1024
PROJ_TN = 768
MIX_T = 512
ROW_SUB = 256
OUT_TN = 256
MIX_PER_FINISH = 2
FFN_TM = 1024
FFN_TF = 1024

_NT_DIMS = (((1,), (1,)), ((), ()))
_TN_DIMS = (((0,), (0,)), ((), ()))


def _layer_norm(y, g, b):
    mu = jnp.mean(y, axis=-1, keepdims=True)
    yc = y - mu
    var = jnp.mean(yc * yc, axis=-1, keepdims=True)
    return yc * lax.rsqrt(var + LN_EPS) * g + b


def _in_proj_kernel(x_ref, w_ref, wz_ref, wo_ref, wu_ref, o_ref, z_ref, wo_b_ref, wu_b_ref, xb_ref):
    @pl.when(pl.program_id(1) == 0)
    def _():
        xb_ref[...] = x_ref[...].astype(BF16)
        z_ref[...] = lax.dot_general(xb_ref[...], wz_ref[...].astype(BF16), _NT_DIMS,
                                     preferred_element_type=F32).astype(z_ref.dtype)

    wo_b_ref[...] = wo_ref[...].astype(BF16)
    wu_b_ref[...] = wu_ref[...].astype(BF16)
    o_ref[...] = lax.dot_general(xb_ref[...], w_ref[...].astype(BF16), _NT_DIMS,
                                 preferred_element_type=F32).astype(o_ref.dtype)


def _in_proj(x2, w_in_t, w_out, w_ff_up, layer):
    m = x2.shape[0]
    n_i, n_j = m // PROJ_TM, D_PROJ_MAIN // PROJ_TN
    slab_rows = D_MODEL // (n_i * n_j)
    slab_in = lambda i, j: (layer, i * n_j + j, 0)
    slab_out = lambda i, j: (i * n_j + j, 0)
    return pl.pallas_call(
        _in_proj_kernel,
        grid=(n_i, n_j),
        in_specs=[pl.BlockSpec((PROJ_TM, D_MODEL), lambda i, j: (i, 0)),
                  pl.BlockSpec((None, PROJ_TN, D_MODEL), lambda i, j: (layer, j, 0)),
                  pl.BlockSpec((None, GATE_RANK, D_MODEL),
                               lambda i, j: (layer, D_PROJ_MAIN // GATE_RANK, 0)),
                  pl.BlockSpec((None, slab_rows, D_MODEL), slab_in),
                  pl.BlockSpec((None, slab_rows, D_FF), slab_in)],
        out_specs=[pl.BlockSpec((PROJ_TM, PROJ_TN), lambda i, j: (i, j)),
                   pl.BlockSpec((PROJ_TM, GATE_RANK), lambda i, j: (i, 0)),
                   pl.BlockSpec((slab_rows, D_MODEL), slab_out),
                   pl.BlockSpec((slab_rows, D_FF), slab_out)],
        out_shape=[jax.ShapeDtypeStruct((m, D_PROJ_MAIN), BF16),
                   jax.ShapeDtypeStruct((m, GATE_RANK), BF16),
                   jax.ShapeDtypeStruct((D_MODEL, D_MODEL), BF16),
                   jax.ShapeDtypeStruct((D_MODEL, D_FF), BF16)],
        scratch_shapes=[pltpu.VMEM((PROJ_TM, D_MODEL), BF16)],
        compiler_params=pltpu.CompilerParams(
            dimension_semantics=("parallel", "arbitrary"), vmem_limit_bytes=VMEM_LIMIT),
        name="in_proj",
    )(x2, w_in_t, w_in_t, w_out, w_ff_up)


def _shift_rows(h3, carry3, shift):
    pos = lax.broadcasted_iota(jnp.int32, h3.shape, 1)
    rolled = pltpu.roll(h3, shift, axis=1)
    rolled_prev = jnp.concatenate([pltpu.roll(carry3, shift, axis=1), rolled[:-1]], axis=0)
    return jnp.where(pos < shift, rolled_prev, rolled)


def _conv_piece(grp, proj_ref, w_ref, g_ref, carry_ref, y_ref):
    ts = proj_ref.shape[0]
    lo = grp * CONV_GROUP_WIDTH
    sl = slice(lo, lo + CONV_GROUP_WIDTH)
    h = (proj_ref[:, COL_C + lo:COL_C + lo + CONV_GROUP_WIDTH].astype(F32)
         * proj_ref[:, COL_U + lo:COL_U + lo + CONV_GROUP_WIDTH].astype(F32))
    h3 = h.reshape(ts // SUBLANES, SUBLANES, CONV_GROUP_WIDTH)
    carry3 = carry_ref[:, sl].reshape(1, SUBLANES, CONV_GROUP_WIDTH)
    h1 = _shift_rows(h3, carry3, 1).reshape(h.shape)
    h2 = _shift_rows(h3, carry3, 2).reshape(h.shape)
    carry_ref[:, sl] = h[ts - SUBLANES:, :]
    w = w_ref[:, sl]
    y = (proj_ref[:, COL_B + lo:COL_B + lo + CONV_GROUP_WIDTH].astype(F32)
         * (w[2:3, :] * h + w[1:2, :] * h1 + w[0:1, :] * h2))
    ms = jnp.mean(y * y, axis=-1, keepdims=True)
    y_ref[:, sl] = (y * lax.rsqrt(ms + RMS_EPS) * g_ref[:, sl]).astype(y_ref.dtype)


def _chunk_cumsum(x):
    t_blk, width = x.shape
    x3 = x.reshape(t_blk // SUBLANES, SUBLANES, width)
    pos = lax.broadcasted_iota(jnp.int32, x3.shape, 1)
    shift = 1
    while shift < SUBLANES:
        x3 = x3 + jnp.where(pos >= shift, pltpu.roll(x3, shift, axis=1), 0.0)
        shift *= 2
    vregs_per_chunk = CHUNK // SUBLANES
    x4 = x3.reshape(t_blk // CHUNK, vregs_per_chunk, SUBLANES, width)
    outs = [x4[:, 0]]
    for j in range(1, vregs_per_chunk):
        outs.append(x4[:, j] + outs[-1][:, SUBLANES - 1:SUBLANES, :])
    return jnp.stack(outs, axis=1).reshape(t_blk, width)


def _gla_head_pieces(head, proj_ref, zl_ref, wg_ref, gb_ref, g_ref, state_ref, y_ref):
    t_blk = proj_ref.shape[0]
    n_chunks = t_blk // CHUNK
    col_q, col_k = COL_Q + head * HEAD_K, COL_K + head * HEAD_K
    col_v, col_r = COL_V + head * HEAD_V, COL_R + head * HEAD_V
    col_y = D_CONV + head * HEAD_V
    pre = {}

    def prepare():
        z = jnp.dot(zl_ref[...], wg_ref[head], preferred_element_type=F32) + gb_ref[head]
        bcum = _chunk_cumsum(jax.nn.log_sigmoid(z) * (1.0 / GATE_TAU))
        pre["q_dec"] = ((proj_ref[:, col_q:col_q + HEAD_K].astype(F32) * (HEAD_K ** -0.5))
                        * jnp.exp(bcum)).astype(BF16)
        k_inv_f = proj_ref[:, col_k:col_k + HEAD_K].astype(F32) * jnp.exp(-bcum)
        pre["k_inv"] = k_inv_f.astype(BF16)
        decay = jnp.exp(bcum.reshape(n_chunks, CHUNK, HEAD_K)[:, CHUNK - 1:CHUNK, :])
        pre["k_end"] = (k_inv_f.reshape(n_chunks, CHUNK, HEAD_K) * decay).astype(BF16).reshape(
            t_blk, HEAD_K)
        pad = jnp.zeros((HEAD_K - n_chunks, HEAD_K), F32)
        pre["decay_t"] = jnp.transpose(jnp.concatenate([decay.reshape(n_chunks, HEAD_K), pad], axis=0))

    def chunk(c):
        rows = slice(c * CHUNK, (c + 1) * CHUNK)
        causal = (lax.broadcasted_iota(jnp.int32, (CHUNK, CHUNK), 0)
                  >= lax.broadcasted_iota(jnp.int32, (CHUNK, CHUNK), 1))
        qd = pre["q_dec"][rows]
        v_c = proj_ref[rows, col_v:col_v + HEAD_V]
        state = state_ref[head]
        scores = lax.dot_general(qd, pre["k_inv"][rows], _NT_DIMS, preferred_element_type=F32)
        scores = jnp.where(causal, scores, 0.0).astype(BF16)
        o = (jnp.dot(scores, v_c, preferred_element_type=F32)
             + jnp.dot(qd, state.astype(BF16), preferred_element_type=F32))
        delta = lax.dot_general(pre["k_end"][rows], v_c, _TN_DIMS, preferred_element_type=F32)
        state_ref[head] = pre["decay_t"][:, c:c + 1] * state + delta
        ms = jnp.mean(o * o, axis=-1, keepdims=True)
        o_n = o * lax.rsqrt(ms + RMS_EPS) * g_ref[:, head * HEAD_V:(head + 1) * HEAD_V]
        r_c = proj_ref[rows, col_r:col_r + HEAD_V].astype(F32)
        y_ref[rows, col_y:col_y + HEAD_V] = (o_n * (r_c * jax.nn.sigmoid(r_c))).astype(y_ref.dtype)

    return [prepare] + [functools.partial(chunk, c) for c in range(n_chunks)]


def _mixer_kernel(proj_ref, zl_ref, x_ref, wo_ref, cw_ref, cg_ref, wg_ref, gb_ref, gg_ref,
                  ln_g_ref, ln_b_ref, wd_ref, o_ref, wd_b_ref, y_ref, yp_ref, mix_ref, state_ref,
                  carry_ref, *, blocks_per_seq):
    s = pl.program_id(0)

    @pl.when(s == 0)
    def _():
        y_ref[...] = jnp.zeros_like(y_ref)

    @pl.when(s % blocks_per_seq == 0)
    def _():
        state_ref[...] = jnp.zeros_like(state_ref)
        carry_ref[...] = jnp.zeros_like(carry_ref)

    yp_ref[...] = y_ref[...]
    wd_b_ref[...] = wd_ref[...].astype(BF16)

    mix_pieces = [functools.partial(_conv_piece, grp, proj_ref, cw_ref, cg_ref, carry_ref, y_ref)
                  for grp in range(CONV_GROUPS)]
    for head in range(GLA_HEADS):
        mix_pieces += _gla_head_pieces(head, proj_ref, zl_ref, wg_ref, gb_ref, gg_ref, state_ref, y_ref)

    def out_proj_tile(sub, n):
        rows = slice(sub * ROW_SUB, (sub + 1) * ROW_SUB)
        cols = slice(n * OUT_TN, (n + 1) * OUT_TN)
        mix_ref[rows, cols] = jnp.dot(yp_ref[rows, :], wo_ref[:, cols], preferred_element_type=F32)

    def layer_norm_rows(sub):
        rows = slice(sub * ROW_SUB, (sub + 1) * ROW_SUB)
        o_ref[rows, :] = _layer_norm(DN_ALPHA * x_ref[rows, :] + mix_ref[rows, :],
                                     ln_g_ref[...], ln_b_ref[...]).astype(o_ref.dtype)

    finish_pieces = []
    for sub in range(o_ref.shape[0] // ROW_SUB):
        finish_pieces += [functools.partial(out_proj_tile, sub, n) for n in range(D_MODEL // OUT_TN)]
        finish_pieces.append(functools.partial(layer_norm_rows, sub))

    while mix_pieces or finish_pieces:
        for piece in mix_pieces[:MIX_PER_FINISH]:
            piece()
        del mix_pieces[:MIX_PER_FINISH]
        if finish_pieces:
            finish_pieces.pop(0)()


def _mixer(proj, z_low, x2, w_out_b, conv_w8, conv_g, wg_b, gate_bias, gla_g, ln_g, ln_b,
           w_ff_down, layer, bsz, seq):
    m = x2.shape[0]
    n_blk = m // MIX_T
    cur_blk = lambda s: (jnp.minimum(s, n_blk - 1), 0)
    prev_blk = lambda s: (jnp.maximum(s - 1, 0), 0)
    const2 = lambda s: (0, 0)
    const3 = lambda s: (0, 0, 0)
    slab_rows = D_FF // n_blk
    return pl.pallas_call(
        functools.partial(_mixer_kernel, blocks_per_seq=seq // MIX_T),
        grid=(n_blk + 1,),
        in_specs=[pl.BlockSpec((MIX_T, D_PROJ_MAIN), cur_blk),
                  pl.BlockSpec((MIX_T, GATE_RANK), cur_blk),
                  pl.BlockSpec((MIX_T, D_MODEL), prev_blk),
                  pl.BlockSpec((D_CONV + D_GLA_V, D_MODEL), const2, pipeline_mode=pl.Buffered(1)),
                  pl.BlockSpec((8, D_CONV), const2),
                  pl.BlockSpec((1, D_CONV), const2),
                  pl.BlockSpec((GLA_HEADS, GATE_RANK, HEAD_K), const3),
                  pl.BlockSpec((GLA_HEADS, 1, HEAD_K), const3),
                  pl.BlockSpec((1, D_GLA_V), const2),
                  pl.BlockSpec((1, D_MODEL), const2),
                  pl.BlockSpec((1, D_MODEL), const2),
                  pl.BlockSpec((None, slab_rows, D_MODEL),
                               lambda s: (layer, jnp.minimum(s, n_blk - 1), 0))],
        out_specs=[pl.BlockSpec((MIX_T, D_MODEL), prev_blk),
                   pl.BlockSpec((slab_rows, D_MODEL), cur_blk)],
        out_shape=[jax.ShapeDtypeStruct((m, D_MODEL), BF16),
                   jax.ShapeDtypeStruct((D_FF, D_MODEL), BF16)],
        scratch_shapes=[pltpu.VMEM((MIX_T, D_CONV + D_GLA_V), BF16),
                        pltpu.VMEM((MIX_T, D_CONV + D_GLA_V), BF16),
                        pltpu.VMEM((MIX_T, D_MODEL), F32),
                        pltpu.VMEM((GLA_HEADS, HEAD_K, HEAD_V), F32),
                        pltpu.VMEM((8, D_CONV), F32)],
        compiler_params=pltpu.CompilerParams(
            dimension_semantics=("arbitrary",), vmem_limit_bytes=VMEM_LIMIT),
        name="mixer",
    )(proj, z_low, x2, w_out_b, conv_w8, conv_g, wg_b, gate_bias, gla_g, ln_g, ln_b, w_ff_down)


def _ffn_ln2_kernel(x_ref, wu_ref, wd_ref, g_ref, b_ref, o_ref):
    f = pl.program_id(1)
    h = jnp.dot(x_ref[...], wu_ref[...], preferred_element_type=F32)
    h = jnp.maximum(h, 0.0)
    h = (h * h).astype(BF16)

    @pl.when(f == 0)
    def _():
        o_ref[...] = jnp.dot(h, wd_ref[...], preferred_element_type=F32)

    last = pl.num_programs(1) - 1

    @pl.when(jnp.logical_and(f > 0, f < last))
    def _():
        o_ref[...] += jnp.dot(h, wd_ref[...], preferred_element_type=F32)

    @pl.when(f == last)
    def _():
        for s in range(o_ref.shape[0] // ROW_SUB):
            rows = slice(s * ROW_SUB, (s + 1) * ROW_SUB)
            ff = o_ref[rows, :] + jnp.dot(h[rows, :], wd_ref[...], preferred_element_type=F32)
            o_ref[rows, :] = _layer_norm(DN_ALPHA * x_ref[rows, :].astype(F32) + ff,
                                         g_ref[...], b_ref[...])


def _ffn_ln2(x1, w_up_b, w_down_b, ln_g, ln_b):
    m = x1.shape[0]
    return pl.pallas_call(
        _ffn_ln2_kernel,
        grid=(m // FFN_TM, D_FF // FFN_TF),
        in_specs=[pl.BlockSpec((FFN_TM, D_MODEL), lambda i, f: (i, 0)),
                  pl.BlockSpec((D_MODEL, FFN_TF), lambda i, f: (0, f)),
                  pl.BlockSpec((FFN_TF, D_MODEL), lambda i, f: (f, 0)),
                  pl.BlockSpec((1, D_MODEL), lambda i, f: (0, 0)),
                  pl.BlockSpec((1, D_MODEL), lambda i, f: (0, 0))],
        out_specs=pl.BlockSpec((FFN_TM, D_MODEL), lambda i, f: (i, 0)),
        out_shape=jax.ShapeDtypeStruct((m, D_MODEL), F32),
        compiler_params=pltpu.CompilerParams(
            dimension_semantics=("parallel", "arbitrary"), vmem_limit_bytes=VMEM_LIMIT),
        name="ffn_ln2",
    )(x1, w_up_b, w_down_b, ln_g, ln_b)


def kernel(x, w_in, conv_w, conv_norm_g, w_gate_up, gate_bias, gla_norm_g, w_out,
           ln1_g, ln1_b, w_ff_up, w_ff_down, ln2_g, ln2_b):
    bsz, seq, _ = x.shape
    assert seq % MIX_T == 0 and MIX_T % CHUNK == 0 and MIX_T % ROW_SUB == 0
    x2 = x.reshape(bsz * seq, D_MODEL)
    w_in_t = jnp.swapaxes(w_in, 1, 2)
    for l in range(DEPTH):
        wg_b = w_gate_up[l].astype(BF16).reshape(GATE_RANK, GLA_HEADS, HEAD_K).transpose(1, 0, 2)
        gb = gate_bias[l].reshape(GLA_HEADS, 1, HEAD_K)
        conv_w8 = jnp.pad(conv_w[l], ((0, 8 - conv_w.shape[1]), (0, 0)))

        proj, z_low, w_out_b, w_up_b = _in_proj(x2, w_in_t, w_out, w_ff_up, l)
        x1, w_down_b = _mixer(proj, z_low, x2, w_out_b, conv_w8, conv_norm_g[l].reshape(1, D_CONV),
                              wg_b, gb, gla_norm_g[l].reshape(1, D_GLA_V),
                              ln1_g[l].reshape(1, D_MODEL), ln1_b[l].reshape(1, D_MODEL),
                              w_ff_down, l, bsz, seq)
        x2 = _ffn_ln2(x1, w_up_b, w_down_b,
                      ln2_g[l].reshape(1, D_MODEL), ln2_b[l].reshape(1, D_MODEL))
    return x2.reshape(bsz, seq, D_MODEL)
```

```python
import functools

import jax
import jax.numpy as jnp
from jax import lax
from jax.experimental import pallas as pl
from jax.experimental.pallas import tpu as pltpu

F32 = jnp.float32
BF16 = jnp.bfloat16

D_MODEL = 2048
D_CONV = 1024
CONV_GROUPS = 8
CONV_GROUP_WIDTH = D_CONV // CONV_GROUPS
GLA_HEADS = 4
HEAD_K = 128
HEAD_V = 256
D_GLA_K = GLA_HEADS * HEAD_K
D_GLA_V = GLA_HEADS * HEAD_V
GATE_RANK = 16
GATE_TAU = 16.0
CHUNK = 64
D_FF = 4 * D_MODEL
LN_EPS = 1e-5
RMS_EPS = 1e-6
DEPTH = 1
DN_ALPHA = (2.0 * DEPTH) ** 0.25

D_PROJ_MAIN = 3 * D_CONV + 2 * D_GLA_K + 2 * D_GLA_V

COL_B, COL_C, COL_U = 0, D_CONV, 2 * D_CONV
COL_Q = 3 * D_CONV
COL_K = COL_Q + D_GLA_K
COL_V = COL_K + D_GLA_K
COL_R = COL_V + D_GLA_V

VMEM_LIMIT = 60 * 1024 * 1024
SUBLANES = 8

PROJ_TM = 1024
PROJ_TN = 768
MIX_T = 512
ROW_SUB = 256
OUT_TN = 512
FFN_TM = 1024
FFN_TF = 1024

_NT_DIMS = (((1,), (1,)), ((), ()))
_TN_DIMS = (((0,), (0,)), ((), ()))


def _layer_norm(y, g, b):
    mu = jnp.mean(y, axis=-1, keepdims=True)
    yc = y - mu
    var = jnp.mean(yc * yc, axis=-1, keepdims=True)
    return yc * lax.rsqrt(var + LN_EPS) * g + b


def _in_proj_kernel(x_ref, w_ref, wz_ref, wo_ref, wu_ref, o_ref, z_ref, wo_b_ref, wu_b_ref, xb_ref):
    @pl.when(pl.program_id(1) == 0)
    def _():
        xb_ref[...] = x_ref[...].astype(BF16)
        z_ref[...] = lax.dot_general(xb_ref[...], wz_ref[...].astype(BF16), _NT_DIMS,
                                     preferred_element_type=F32).astype(z_ref.dtype)

    wo_b_ref[...] = wo_ref[...].astype(BF16)
    wu_b_ref[...] = wu_ref[...].astype(BF16)
    o_ref[...] = lax.dot_general(xb_ref[...], w_ref[...].astype(BF16), _NT_DIMS,
                                 preferred_element_type=F32).astype(o_ref.dtype)


def _in_proj(x2, w_in_t, w_out, w_ff_up, layer):
    m = x2.shape[0]
    n_i, n_j = m // PROJ_TM, D_PROJ_MAIN // PROJ_TN
    slab_rows = D_MODEL // (n_i * n_j)
    slab_in = lambda i, j: (layer, i * n_j + j, 0)
    slab_out = lambda i, j: (i * n_j + j, 0)
    return pl.pallas_call(
        _in_proj_kernel,
        grid=(n_i, n_j),
        in_specs=[pl.BlockSpec((PROJ_TM, D_MODEL), lambda i, j: (i, 0)),
                  pl.BlockSpec((None, PROJ_TN, D_MODEL), lambda i, j: (layer, j, 0)),
                  pl.BlockSpec((None, GATE_RANK, D_MODEL),
                               lambda i, j: (layer, D_PROJ_MAIN // GATE_RANK, 0)),
                  pl.BlockSpec((None, slab_rows, D_MODEL), slab_in),
                  pl.BlockSpec((None, slab_rows, D_FF), slab_in)],
        out_specs=[pl.BlockSpec((PROJ_TM, PROJ_TN), lambda i, j: (i, j)),
                   pl.BlockSpec((PROJ_TM, GATE_RANK), lambda i, j: (i, 0)),
                   pl.BlockSpec((slab_rows, D_MODEL), slab_out),
                   pl.BlockSpec((slab_rows, D_FF), slab_out)],
        out_shape=[jax.ShapeDtypeStruct((m, D_PROJ_MAIN), BF16),
                   jax.ShapeDtypeStruct((m, GATE_RANK), BF16),
                   jax.ShapeDtypeStruct((D_MODEL, D_MODEL), BF16),
                   jax.ShapeDtypeStruct((D_MODEL, D_FF), BF16)],
        scratch_shapes=[pltpu.VMEM((PROJ_TM, D_MODEL), BF16)],
        compiler_params=pltpu.CompilerParams(
            dimension_semantics=("parallel", "arbitrary"), vmem_limit_bytes=VMEM_LIMIT),
        name="in_proj",
    )(x2, w_in_t, w_in_t, w_out, w_ff_up)


def _shift_rows(h3, carry3, shift):
    pos = lax.broadcasted_iota(jnp.int32, h3.shape, 1)
    rolled = pltpu.roll(h3, shift, axis=1)
    rolled_prev = jnp.concatenate([pltpu.roll(carry3, shift, axis=1), rolled[:-1]], axis=0)
    return jnp.where(pos < shift, rolled_prev, rolled)


def _conv_piece(grp, proj_ref, w_ref, g_ref, carry_ref, y_ref):
    ts = proj_ref.shape[0]
    lo = grp * CONV_GROUP_WIDTH
    sl = slice(lo, lo + CONV_GROUP_WIDTH)
    h = (proj_ref[:, COL_C + lo:COL_C + lo + CONV_GROUP_WIDTH].astype(F32)
         * proj_ref[:, COL_U + lo:COL_U + lo + CONV_GROUP_WIDTH].astype(F32))
    h3 = h.reshape(ts // SUBLANES, SUBLANES, CONV_GROUP_WIDTH)
    carry3 = carry_ref[:, sl].reshape(1, SUBLANES, CONV_GROUP_WIDTH)
    h1 = _shift_rows(h3, carry3, 1).reshape(h.shape)
    h2 = _shift_rows(h3, carry3, 2).reshape(h.shape)
    carry_ref[:, sl] = h[ts - SUBLANES:, :]
    w = w_ref[:, sl]
    y = (proj_ref[:, COL_B + lo:COL_B + lo + CONV_GROUP_WIDTH].astype(F32)
         * (w[2:3, :] * h + w[1:2, :] * h1 + w[0:1, :] * h2))
    ms = jnp.mean(y * y, axis=-1, keepdims=True)
    y_ref[:, sl] = (y * lax.rsqrt(ms + RMS_EPS) * g_ref[:, sl]).astype(y_ref.dtype)


def _chunk_cumsum(x):
    t_blk, width = x.shape
    x3 = x.reshape(t_blk // SUBLANES, SUBLANES, width)
    pos = lax.broadcasted_iota(jnp.int32, x3.shape, 1)
    shift = 1
    while shift < SUBLANES:
        x3 = x3 + jnp.where(pos >= shift, pltpu.roll(x3, shift, axis=1), 0.0)
        shift *= 2
    vregs_per_chunk = CHUNK // SUBLANES
    x4 = x3.reshape(t_blk // CHUNK, vregs_per_chunk, SUBLANES, width)
    outs = [x4[:, 0]]
    for j in range(1, vregs_per_chunk):
        outs.append(x4[:, j] + outs[-1][:, SUBLANES - 1:SUBLANES, :])
    return jnp.stack(outs, axis=1).reshape(t_blk, width)


def _gla_prepare(head, proj_ref, z_ref, gb_ref, qd_ref, ki_ref, ke_ref, dct_ref):
    t_blk = proj_ref.shape[0]
    n_chunks = t_blk // CHUNK
    col_q, col_k = COL_Q + head * HEAD_K, COL_K + head * HEAD_K
    z = z_ref[:, head * HEAD_K:(head + 1) * HEAD_K] + gb_ref[:, head * HEAD_K:(head + 1) * HEAD_K]
    bcum = _chunk_cumsum(jax.nn.log_sigmoid(z) * (1.0 / GATE_TAU))
    qd_ref[head] = ((proj_ref[:, col_q:col_q + HEAD_K].astype(F32) * (HEAD_K ** -0.5))
                    * jnp.exp(bcum)).astype(BF16)
    k_inv_f = proj_ref[:, col_k:col_k + HEAD_K].astype(F32) * jnp.exp(-bcum)
    ki_ref[head] = k_inv_f.astype(BF16)
    decay = jnp.exp(bcum.reshape(n_chunks, CHUNK, HEAD_K)[:, CHUNK - 1:CHUNK, :])
    ke_ref[head] = (k_inv_f.reshape(n_chunks, CHUNK, HEAD_K) * decay).astype(BF16).reshape(
        t_blk, HEAD_K)
    pad = jnp.zeros((HEAD_K - n_chunks, HEAD_K), F32)
    dct_ref[head] = jnp.transpose(jnp.concatenate([decay.reshape(n_chunks, HEAD_K), pad], axis=0))


def _gla_scores(head, c, qd_ref, ki_ref, sc_ref):
    rows = slice(c * CHUNK, (c + 1) * CHUNK)
    causal = (lax.broadcasted_iota(jnp.int32, (CHUNK, CHUNK), 0)
              >= lax.broadcasted_iota(jnp.int32, (CHUNK, CHUNK), 1))
    scores = lax.dot_general(qd_ref[head, rows, :], ki_ref[head, rows, :], _NT_DIMS,
                             preferred_element_type=F32)
    sc_ref[head, rows, :] = jnp.where(causal, scores, 0.0).astype(sc_ref.dtype)


def _gla_state_step(head, c, proj_ref, ke_ref, dct_ref, state_ref, st_ref):
    rows = slice(c * CHUNK, (c + 1) * CHUNK)
    col_v = COL_V + head * HEAD_V
    state = state_ref[head]
    st_ref[head, c] = state.astype(st_ref.dtype)
    delta = lax.dot_general(ke_ref[head, rows, :], proj_ref[rows, col_v:col_v + HEAD_V], _TN_DIMS,
                            preferred_element_type=F32)
    state_ref[head] = dct_ref[head, :, c:c + 1] * state + delta


def _gla_output(head, c, proj_ref, g_ref, qd_ref, sc_ref, st_ref, y_ref):
    rows = slice(c * CHUNK, (c + 1) * CHUNK)
    col_v, col_r = COL_V + head * HEAD_V, COL_R + head * HEAD_V
    col_y = D_CONV + head * HEAD_V
    o = (jnp.dot(sc_ref[head, rows, :], proj_ref[rows, col_v:col_v + HEAD_V],
                 preferred_element_type=F32)
         + jnp.dot(qd_ref[head, rows, :], st_ref[head, c], preferred_element_type=F32))
    ms = jnp.mean(o * o, axis=-1, keepdims=True)
    o_n = o * lax.rsqrt(ms + RMS_EPS) * g_ref[:, head * HEAD_V:(head + 1) * HEAD_V]
    r_c = proj_ref[rows, col_r:col_r + HEAD_V].astype(F32)
    y_ref[rows, col_y:col_y + HEAD_V] = (o_n * (r_c * jax.nn.sigmoid(r_c))).astype(y_ref.dtype)


def _mixer_kernel(proj_ref, zl_ref, x_ref, wo_ref, cw_ref, cg_ref, wg_ref, gb_ref, gg_ref,
                  ln_g_ref, ln_b_ref, wd_ref, o_ref, wd_b_ref, y_ref, yp_ref, mix_ref, state_ref,
                  carry_ref, qd_ref, ki_ref, ke_ref, dct_ref, z_ref, sc_ref, st_ref, *, blocks_per_seq):
    s = pl.program_id(0)

    @pl.when(s == 0)
    def _():
        y_ref[...] = jnp.zeros_like(y_ref)

    @pl.when(s % blocks_per_seq == 0)
    def _():
        state_ref[...] = jnp.zeros_like(state_ref)
        carry_ref[...] = jnp.zeros_like(carry_ref)

    yp_ref[...] = y_ref[...]
    wd_b_ref[...] = wd_ref[...].astype(BF16)

    gla_refs = (qd_ref, ki_ref, ke_ref, dct_ref)
    n_chunks = proj_ref.shape[0] // CHUNK
    vector_pieces = [functools.partial(_conv_piece, grp, proj_ref, cw_ref, cg_ref, carry_ref, y_ref)
                     for grp in range(CONV_GROUPS)]
    vector_pieces += [functools.partial(_gla_prepare, head, proj_ref, z_ref, gb_ref, *gla_refs)
                      for head in range(GLA_HEADS)]
    head_chunks = [(head, c) for c in range(n_chunks) for head in range(GLA_HEADS)]
    chunk_pieces = (
        [functools.partial(_gla_scores, h, c, qd_ref, ki_ref, sc_ref) for h, c in head_chunks]
        + [functools.partial(_gla_state_step, h, c, proj_ref, ke_ref, dct_ref, state_ref, st_ref)
           for h, c in head_chunks]
        + [functools.partial(_gla_output, h, c, proj_ref, gg_ref, qd_ref, sc_ref, st_ref, y_ref)
           for h, c in head_chunks])

    def gate_preactivation():
        z_ref[...] = jnp.dot(zl_ref[...], wg_ref[...], preferred_element_type=F32)

    def out_proj_tile(sub, n):
        rows = slice(sub * ROW_SUB, (sub + 1) * ROW_SUB)
        cols = slice(n * OUT_TN, (n + 1) * OUT_TN)
        mix_ref[rows, cols] = jnp.dot(yp_ref[rows, :], wo_ref[:, cols], preferred_element_type=F32)

    def layer_norm_rows(row0, n_rows):
        rows = slice(row0, row0 + n_rows)
        o_ref[rows, :] = _layer_norm(DN_ALPHA * x_ref[rows, :] + mix_ref[rows, :],
                                     ln_g_ref[...], ln_b_ref[...]).astype(o_ref.dtype)

    tiles_per_sub = D_MODEL // OUT_TN
    n_tiles = (o_ref.shape[0] // ROW_SUB) * tiles_per_sub
    ln_rows = ROW_SUB // 2
    ln_queue = []
    pieces = [gate_preactivation]
    for t in range(n_tiles):
        sub, n = divmod(t, tiles_per_sub)
        lo = len(vector_pieces) * t // n_tiles
        hi = len(vector_pieces) * (t + 1) // n_tiles
        pieces += vector_pieces[lo:hi] + ln_queue[:1] + [functools.partial(out_proj_tile, sub, n)]
        del ln_queue[:1]
        if n == tiles_per_sub - 1:
            ln_queue += [functools.partial(layer_norm_rows, sub * ROW_SUB + r, ln_rows)
                         for r in range(0, ROW_SUB, ln_rows)]
    for piece in pieces + chunk_pieces + ln_queue:
        piece()


def _mixer(proj, z_low, x2, w_out_b, conv_w8, conv_g, wg_b, gate_bias, gla_g, ln_g, ln_b,
           w_ff_down, layer, bsz, seq):
    m = x2.shape[0]
    n_blk = m // MIX_T
    cur_blk = lambda s: (jnp.minimum(s, n_blk - 1), 0)
    prev_blk = lambda s: (jnp.maximum(s - 1, 0), 0)
    const2 = lambda s: (0, 0)
    slab_rows = D_FF // n_blk
    return pl.pallas_call(
        functools.partial(_mixer_kernel, blocks_per_seq=seq // MIX_T),
        grid=(n_blk + 1,),
        in_specs=[pl.BlockSpec((MIX_T, D_PROJ_MAIN), cur_blk),
                  pl.BlockSpec((MIX_T, GATE_RANK), cur_blk),
                  pl.BlockSpec((MIX_T, D_MODEL), prev_blk),
                  pl.BlockSpec((D_CONV + D_GLA_V, D_MODEL), const2, pipeline_mode=pl.Buffered(1)),
                  pl.BlockSpec((8, D_CONV), const2),
                  pl.BlockSpec((1, D_CONV), const2),
                  pl.BlockSpec((GATE_RANK, D_GLA_K), const2),
                  pl.BlockSpec((1, D_GLA_K), const2),
                  pl.BlockSpec((1, D_GLA_V), const2),
                  pl.BlockSpec((1, D_MODEL), const2),
                  pl.BlockSpec((1, D_MODEL), const2),
                  pl.BlockSpec((None, slab_rows, D_MODEL),
                               lambda s: (layer, jnp.minimum(s, n_blk - 1), 0))],
        out_specs=[pl.BlockSpec((MIX_T, D_MODEL), prev_blk),
                   pl.BlockSpec((slab_rows, D_MODEL), cur_blk)],
        out_shape=[jax.ShapeDtypeStruct((m, D_MODEL), BF16),
                   jax.ShapeDtypeStruct((D_FF, D_MODEL), BF16)],
        scratch_shapes=[pltpu.VMEM((MIX_T, D_CONV + D_GLA_V), BF16),
                        pltpu.VMEM((MIX_T, D_CONV + D_GLA_V), BF16),
                        pltpu.VMEM((MIX_T, D_MODEL), F32),
                        pltpu.VMEM((GLA_HEADS, HEAD_K, HEAD_V), F32),
                        pltpu.VMEM((8, D_CONV), F32),
                        pltpu.VMEM((GLA_HEADS, MIX_T, HEAD_K), BF16),
                        pltpu.VMEM((GLA_HEADS, MIX_T, HEAD_K), BF16),
                        pltpu.VMEM((GLA_HEADS, MIX_T, HEAD_K), BF16),
                        pltpu.VMEM((GLA_HEADS, HEAD_K, HEAD_K), F32),
                        pltpu.VMEM((MIX_T, D_GLA_K), F32),
                        pltpu.VMEM((GLA_HEADS, MIX_T, CHUNK), BF16),
                        pltpu.VMEM((GLA_HEADS, MIX_T // CHUNK, HEAD_K, HEAD_V), BF16)],
        compiler_params=pltpu.CompilerParams(
            dimension_semantics=("arbitrary",), vmem_limit_bytes=VMEM_LIMIT),
        name="mixer",
    )(proj, z_low, x2, w_out_b, conv_w8, conv_g, wg_b, gate_bias, gla_g, ln_g, ln_b, w_ff_down)


def _ffn_ln2_kernel(x_ref, wu_ref, wd_ref, g_ref, b_ref, o_ref):
    f = pl.program_id(1)
    h = jnp.dot(x_ref[...], wu_ref[...], preferred_element_type=F32)
    h = jnp.maximum(h, 0.0)
    h = (h * h).astype(BF16)

    @pl.when(f == 0)
    def _():
        o_ref[...] = jnp.dot(h, wd_ref[...], preferred_element_type=F32)

    last = pl.num_programs(1) - 1

    @pl.when(jnp.logical_and(f > 0, f < last))
    def _():
        o_ref[...] += jnp.dot(h, wd_ref[...], preferred_element_type=F32)

    @pl.when(f == last)
    def _():
        for s in range(o_ref.shape[0] // ROW_SUB):
            rows = slice(s * ROW_SUB, (s + 1) * ROW_SUB)
            ff = o_ref[rows, :] + jnp.dot(h[rows, :], wd_ref[...], preferred_element_type=F32)
            o_ref[rows, :] = _layer_norm(DN_ALPHA * x_ref[rows, :].astype(F32) + ff,
                                         g_ref[...], b_ref[...])


def _ffn_ln2(x1, w_up_b, w_down_b, ln_g, ln_b):
    m = x1.shape[0]
    return pl.pallas_call(
        _ffn_ln2_kernel,
        grid=(m // FFN_TM, D_FF // FFN_TF),
        in_specs=[pl.BlockSpec((FFN_TM, D_MODEL), lambda i, f: (i, 0)),
                  pl.BlockSpec((D_MODEL, FFN_TF), lambda i, f: (0, f)),
                  pl.BlockSpec((FFN_TF, D_MODEL), lambda i, f: (f, 0)),
                  pl.BlockSpec((1, D_MODEL), lambda i, f: (0, 0)),
                  pl.BlockSpec((1, D_MODEL), lambda i, f: (0, 0))],
        out_specs=pl.BlockSpec((FFN_TM, D_MODEL), lambda i, f: (i, 0)),
        out_shape=jax.ShapeDtypeStruct((m, D_MODEL), F32),
        compiler_params=pltpu.CompilerParams(
            dimension_semantics=("parallel", "arbitrary"), vmem_limit_bytes=VMEM_LIMIT),
        name="ffn_ln2",
    )(x1, w_up_b, w_down_b, ln_g, ln_b)


def kernel(x, w_in, conv_w, conv_norm_g, w_gate_up, gate_bias, gla_norm_g, w_out,
           ln1_g, ln1_b, w_ff_up, w_ff_down, ln2_g, ln2_b):
    bsz, seq, _ = x.shape
    assert seq % MIX_T == 0 and MIX_T % CHUNK == 0 and MIX_T % ROW_SUB == 0
    x2 = x.reshape(bsz * seq, D_MODEL)
    w_in_t = jnp.swapaxes(w_in, 1, 2)
    for l in range(DEPTH):
        wg_b = w_gate_up[l].astype(BF16)
        gb = gate_bias[l].reshape(1, D_GLA_K)
        conv_w8 = jnp.pad(conv_w[l], ((0, 8 - conv_w.shape[1]), (0, 0)))

        proj, z_low, w_out_b, w_up_b = _in_proj(x2, w_in_t, w_out, w_ff_up, l)
        x1, w_down_b = _mixer(proj, z_low, x2, w_out_b, conv_w8, conv_norm_g[l].reshape(1, D_CONV),
                              wg_b, gb, gla_norm_g[l].reshape(1, D_GLA_V),
                              ln1_g[l].reshape(1, D_MODEL), ln1_b[l].reshape(1, D_MODEL),
                              w_ff_down, l, bsz, seq)
        x2 = _ffn_ln2(x1, w_up_b, w_down_b,
                      ln2_g[l].reshape(1, D_MODEL), ln2_b[l].reshape(1, D_MODEL))
    return x2.reshape(bsz, seq, D_MODEL)
```

```python
import functools

import jax
import jax.numpy as jnp
from jax import lax
from jax.experimental import pallas as pl
from jax.experimental.pallas import tpu as pltpu

F32 = jnp.float32
BF16 = jnp.bfloat16

D_MODEL = 2048
D_CONV = 1024
CONV_GROUPS = 8
CONV_GROUP_WIDTH = D_CONV // CONV_GROUPS
GLA_HEADS = 4
HEAD_K = 128
HEAD_V = 256
D_GLA_K = GLA_HEADS * HEAD_K
D_GLA_V = GLA_HEADS * HEAD_V
GATE_RANK = 16
GATE_TAU = 16.0
CHUNK = 64
D_FF = 4 * D_MODEL
LN_EPS = 1e-5
RMS_EPS = 1e-6
DEPTH = 1
DN_ALPHA = (2.0 * DEPTH) ** 0.25

D_PROJ_MAIN = 3 * D_CONV + 2 * D_GLA_K + 2 * D_GLA_V

COL_B, COL_C, COL_U = 0, D_CONV, 2 * D_CONV
COL_Q = 3 * D_CONV
COL_K = COL_Q + D_GLA_K
COL_V = COL_K + D_GLA_K
COL_R = COL_V + D_GLA_V

VMEM_LIMIT = 60 * 1024 * 1024
SUBLANES = 8

PROJ_TM = 1024
PROJ_HEAD_TM = 2048
PROJ_HEAD_TN = 512
PROJ_TN = 1024
CAST_SLABS = 32
MIX_T = 512
ROW_SUB = 256
OUT_TN = 512
FFN_TM = 1024
FFN_TF = 1024

_NT_DIMS = (((1,), (1,)), ((), ()))
_TN_DIMS = (((0,), (0,)), ((), ()))


def _layer_norm(y, g, b):
    mu = jnp.mean(y, axis=-1, keepdims=True)
    yc = y - mu
    var = jnp.mean(yc * yc, axis=-1, keepdims=True)
    return yc * lax.rsqrt(var + LN_EPS) * g + b


def _in_proj_head_kernel(x_ref, w_ref, wz_ref, o_ref, z_ref, wb_ref, wzb_ref, xb_ref):
    @pl.when(pl.program_id(0) == 0)
    def _():
        xb_ref[...] = x_ref[...].astype(BF16)
        wzb_ref[...] = wz_ref[...].astype(BF16)
        z_ref[...] = lax.dot_general(xb_ref[...], wzb_ref[...], _NT_DIMS,
                                     preferred_element_type=F32).astype(z_ref.dtype)

    wb_ref[...] = w_ref[...].astype(BF16)
    o_ref[...] = lax.dot_general(xb_ref[...], wb_ref[...], _NT_DIMS,
                                 preferred_element_type=F32).astype(o_ref.dtype)


def _in_proj_rest_kernel(x_ref, wb_ref, wzb_ref, proj_hbm_ref, z_hbm_ref, wo_ref, wu_ref,
                         o_ref, z_ref, wo_b_ref, wu_b_ref, xb_ref):
    del proj_hbm_ref, z_hbm_ref

    @pl.when(pl.program_id(1) == 0)
    def _():
        xb_ref[...] = x_ref[...].astype(BF16)
        z_ref[...] = lax.dot_general(xb_ref[...], wzb_ref[...], _NT_DIMS,
                                     preferred_element_type=F32).astype(z_ref.dtype)

    wo_b_ref[...] = wo_ref[...].astype(BF16)
    wu_b_ref[...] = wu_ref[...].astype(BF16)
    o_ref[...] = lax.dot_general(xb_ref[...], wb_ref[...], _NT_DIMS,
                                 preferred_element_type=F32).astype(o_ref.dtype)


def _in_proj(x2, w_in_t, w_out, w_ff_up, layer):
    m = x2.shape[0]
    i0 = PROJ_HEAD_TM // PROJ_TM
    n_i = m // PROJ_TM - i0
    proj_shape = jax.ShapeDtypeStruct((m, D_PROJ_MAIN), BF16)
    z_shape = jax.ShapeDtypeStruct((m, GATE_RANK), BF16)

    n_jh = D_PROJ_MAIN // PROJ_HEAD_TN
    proj, z_low, w_in_b, wz_b = pl.pallas_call(
        _in_proj_head_kernel,
        grid=(n_jh,),
        in_specs=[pl.BlockSpec((PROJ_HEAD_TM, D_MODEL), lambda j: (0, 0), pipeline_mode=pl.Buffered(1)),
                  pl.BlockSpec((None, PROJ_HEAD_TN, D_MODEL), lambda j: (layer, j, 0)),
                  pl.BlockSpec((None, GATE_RANK, D_MODEL),
                               lambda j: (layer, D_PROJ_MAIN // GATE_RANK, 0))],
        out_specs=[pl.BlockSpec((PROJ_HEAD_TM, PROJ_HEAD_TN), lambda j: (0, j)),
                   pl.BlockSpec((PROJ_HEAD_TM, GATE_RANK), lambda j: (0, 0)),
                   pl.BlockSpec((PROJ_HEAD_TN, D_MODEL), lambda j: (j, 0)),
                   pl.BlockSpec((GATE_RANK, D_MODEL), lambda j: (0, 0))],
        out_shape=[proj_shape, z_shape,
                   jax.ShapeDtypeStruct((D_PROJ_MAIN, D_MODEL), BF16),
                   jax.ShapeDtypeStruct((GATE_RANK, D_MODEL), BF16)],
        scratch_shapes=[pltpu.VMEM((PROJ_HEAD_TM, D_MODEL), BF16)],
        compiler_params=pltpu.CompilerParams(
            dimension_semantics=("arbitrary",), vmem_limit_bytes=VMEM_LIMIT),
        name="in_proj_head",
    )(x2, w_in_t, w_in_t)

    n_j = D_PROJ_MAIN // PROJ_TN
    slab_rows = D_MODEL // CAST_SLABS
    slab = lambda i, j: jnp.minimum(i * n_j + j, CAST_SLABS - 1)
    return pl.pallas_call(
        _in_proj_rest_kernel,
        grid=(n_i, n_j),
        in_specs=[pl.BlockSpec((PROJ_TM, D_MODEL), lambda i, j: (i + i0, 0)),
                  pl.BlockSpec((PROJ_TN, D_MODEL), lambda i, j: (j, 0)),
                  pl.BlockSpec((GATE_RANK, D_MODEL), lambda i, j: (0, 0)),
                  pl.BlockSpec(memory_space=pl.ANY),
                  pl.BlockSpec(memory_space=pl.ANY),
                  pl.BlockSpec((None, slab_rows, D_MODEL), lambda i, j: (layer, slab(i, j), 0)),
                  pl.BlockSpec((None, slab_rows, D_FF), lambda i, j: (layer, slab(i, j), 0))],
        out_specs=[pl.BlockSpec((PROJ_TM, PROJ_TN), lambda i, j: (i + i0, j)),
                   pl.BlockSpec((PROJ_TM, GATE_RANK), lambda i, j: (i + i0, 0)),
                   pl.BlockSpec((slab_rows, D_MODEL), lambda i, j: (slab(i, j), 0)),
                   pl.BlockSpec((slab_rows, D_FF), lambda i, j: (slab(i, j), 0))],
        out_shape=[proj_shape, z_shape,
                   jax.ShapeDtypeStruct((D_MODEL, D_MODEL), BF16),
                   jax.ShapeDtypeStruct((D_MODEL, D_FF), BF16)],
        input_output_aliases={3: 0, 4: 1},
        scratch_shapes=[pltpu.VMEM((PROJ_TM, D_MODEL), BF16)],
        compiler_params=pltpu.CompilerParams(
            dimension_semantics=("arbitrary", "arbitrary"), vmem_limit_bytes=VMEM_LIMIT),
        name="in_proj_rest",
    )(x2, w_in_b, wz_b, proj, z_low, w_out, w_ff_up)


def _shift_rows(h3, carry3, shift):
    pos = lax.broadcasted_iota(jnp.int32, h3.shape, 1)
    rolled = pltpu.roll(h3, shift, axis=1)
    rolled_prev = jnp.concatenate([pltpu.roll(carry3, shift, axis=1), rolled[:-1]], axis=0)
    return jnp.where(pos < shift, rolled_prev, rolled)


def _conv_piece(grp, proj_ref, w_ref, g_ref, carry_ref, y_ref):
    ts = proj_ref.shape[0]
    lo = grp * CONV_GROUP_WIDTH
    sl = slice(lo, lo + CONV_GROUP_WIDTH)
    h = (proj_ref[:, COL_C + lo:COL_C + lo + CONV_GROUP_WIDTH].astype(F32)
         * proj_ref[:, COL_U + lo:COL_U + lo + CONV_GROUP_WIDTH].astype(F32))
    h3 = h.reshape(ts // SUBLANES, SUBLANES, CONV_GROUP_WIDTH)
    carry3 = carry_ref[:, sl].reshape(1, SUBLANES, CONV_GROUP_WIDTH)
    h1 = _shift_rows(h3, carry3, 1).reshape(h.shape)
    h2 = _shift_rows(h3, carry3, 2).reshape(h.shape)
    carry_ref[:, sl] = h[ts - SUBLANES:, :]
    w = w_ref[:, sl]
    y = (proj_ref[:, COL_B + lo:COL_B + lo + CONV_GROUP_WIDTH].astype(F32)
         * (w[2:3, :] * h + w[1:2, :] * h1 + w[0:1, :] * h2))
    ms = jnp.mean(y * y, axis=-1, keepdims=True)
    y_ref[:, sl] = (y * lax.rsqrt(ms + RMS_EPS) * g_ref[:, sl]).astype(y_ref.dtype)


def _chunk_cumsum(x):
    t_blk, width = x.shape
    x3 = x.reshape(t_blk // SUBLANES, SUBLANES, width)
    pos = lax.broadcasted_iota(jnp.int32, x3.shape, 1)
    shift = 1
    while shift < SUBLANES:
        x3 = x3 + jnp.where(pos >= shift, pltpu.roll(x3, shift, axis=1), 0.0)
        shift *= 2
    vregs_per_chunk = CHUNK // SUBLANES
    x4 = x3.reshape(t_blk // CHUNK, vregs_per_chunk, SUBLANES, width)
    outs = [x4[:, 0]]
    for j in range(1, vregs_per_chunk):
        outs.append(x4[:, j] + outs[-1][:, SUBLANES - 1:SUBLANES, :])
    return jnp.stack(outs, axis=1).reshape(t_blk, width)


def _gla_prepare(head, proj_ref, z_ref, gb_ref, qd_ref, ki_ref, ke_ref, dct_ref):
    t_blk = proj_ref.shape[0]
    n_chunks = t_blk // CHUNK
    col_q, col_k = COL_Q + head * HEAD_K, COL_K + head * HEAD_K
    z = z_ref[:, head * HEAD_K:(head + 1) * HEAD_K] + gb_ref[:, head * HEAD_K:(head + 1) * HEAD_K]
    log_sig = jnp.minimum(z, 0.0) - jnp.log(1.0 + jnp.exp(-jnp.abs(z)))
    bcum = _chunk_cumsum(log_sig * (1.0 / GATE_TAU))
    qd_ref[head] = ((proj_ref[:, col_q:col_q + HEAD_K].astype(F32) * (HEAD_K ** -0.5))
                    * jnp.exp(bcum)).astype(BF16)
    k_inv_f = proj_ref[:, col_k:col_k + HEAD_K].astype(F32) * jnp.exp(-bcum)
    ki_ref[head] = k_inv_f.astype(BF16)
    decay = jnp.exp(bcum.reshape(n_chunks, CHUNK, HEAD_K)[:, CHUNK - 1:CHUNK, :])
    ke_ref[head] = (k_inv_f.reshape(n_chunks, CHUNK, HEAD_K) * decay).astype(BF16).reshape(
        t_blk, HEAD_K)
    pad = jnp.zeros((HEAD_K - n_chunks, HEAD_K), F32)
    dct_ref[head] = jnp.transpose(jnp.concatenate([decay.reshape(n_chunks, HEAD_K), pad], axis=0))


def _gla_scores(head, c, qd_ref, ki_ref, sc_ref):
    rows = slice(c * CHUNK, (c + 1) * CHUNK)
    causal = (lax.broadcasted_iota(jnp.int32, (CHUNK, CHUNK), 0)
              >= lax.broadcasted_iota(jnp.int32, (CHUNK, CHUNK), 1))
    scores = lax.dot_general(qd_ref[head, rows, :], ki_ref[head, rows, :], _NT_DIMS,
                             preferred_element_type=F32)
    sc_ref[head, rows, :] = jnp.where(causal, scores, 0.0).astype(sc_ref.dtype)


def _gla_state_step(head, c, proj_ref, ke_ref, dct_ref, state_ref, st_ref):
    rows = slice(c * CHUNK, (c + 1) * CHUNK)
    col_v = COL_V + head * HEAD_V
    state = state_ref[head]
    st_ref[head, c] = state.astype(st_ref.dtype)
    delta = lax.dot_general(ke_ref[head, rows, :], proj_ref[rows, col_v:col_v + HEAD_V], _TN_DIMS,
                            preferred_element_type=F32)
    state_ref[head] = dct_ref[head, :, c:c + 1] * state + delta


def _gla_output(head, c, proj_ref, g_ref, qd_ref, sc_ref, st_ref, y_ref):
    rows = slice(c * CHUNK, (c + 1) * CHUNK)
    col_v, col_r = COL_V + head * HEAD_V, COL_R + head * HEAD_V
    col_y = D_CONV + head * HEAD_V
    o = (jnp.dot(sc_ref[head, rows, :], proj_ref[rows, col_v:col_v + HEAD_V],
                 preferred_element_type=F32)
         + jnp.dot(qd_ref[head, rows, :], st_ref[head, c], preferred_element_type=F32))
    ms = jnp.mean(o * o, axis=-1, keepdims=True)
    o_n = o * lax.rsqrt(ms + RMS_EPS) * g_ref[:, head * HEAD_V:(head + 1) * HEAD_V]
    r_c = proj_ref[rows, col_r:col_r + HEAD_V].astype(F32)
    y_ref[rows, col_y:col_y + HEAD_V] = (o_n * (r_c * jax.nn.sigmoid(r_c))).astype(y_ref.dtype)


def _mixer_kernel(proj_ref, zl_ref, x_ref, wo_ref, cw_ref, cg_ref, wg_ref, gb_ref, gg_ref,
                  ln_g_ref, ln_b_ref, wd_ref, o_ref, wd_b_ref, y_ref, yp_ref, mix_ref, state_ref,
                  carry_ref, qd_ref, ki_ref, ke_ref, dct_ref, z_ref, sc_ref, st_ref, *, blocks_per_seq):
    s = pl.program_id(0)

    @pl.when(s == 0)
    def _():
        y_ref[...] = jnp.zeros_like(y_ref)

    @pl.when(s % blocks_per_seq == 0)
    def _():
        state_ref[...] = jnp.zeros_like(state_ref)
        carry_ref[...] = jnp.zeros_like(carry_ref)

    yp_ref[...] = y_ref[...]
    wd_b_ref[...] = wd_ref[...].astype(BF16)

    gla_refs = (qd_ref, ki_ref, ke_ref, dct_ref)
    n_chunks = proj_ref.shape[0] // CHUNK
    vector_pieces = [functools.partial(_conv_piece, grp, proj_ref, cw_ref, cg_ref, carry_ref, y_ref)
                     for grp in range(CONV_GROUPS)]
    vector_pieces += [functools.partial(_gla_prepare, head, proj_ref, z_ref, gb_ref, *gla_refs)
                      for head in range(GLA_HEADS)]
    head_chunks = [(head, c) for c in range(n_chunks) for head in range(GLA_HEADS)]
    chunk_pieces = (
        [functools.partial(_gla_scores, h, c, qd_ref, ki_ref, sc_ref) for h, c in head_chunks]
        + [functools.partial(_gla_state_step, h, c, proj_ref, ke_ref, dct_ref, state_ref, st_ref)
           for h, c in head_chunks]
        + [functools.partial(_gla_output, h, c, proj_ref, gg_ref, qd_ref, sc_ref, st_ref, y_ref)
           for h, c in head_chunks])

    def gate_preactivation():
        z_ref[...] = jnp.dot(zl_ref[...], wg_ref[...], preferred_element_type=F32)

    def out_proj_tile(sub, n):
        rows = slice(sub * ROW_SUB, (sub + 1) * ROW_SUB)
        cols = slice(n * OUT_TN, (n + 1) * OUT_TN)
        mix_ref[rows, cols] = jnp.dot(yp_ref[rows, :], wo_ref[:, cols], preferred_element_type=F32)

    def layer_norm_rows(row0, n_rows):
        rows = slice(row0, row0 + n_rows)
        o_ref[rows, :] = _layer_norm(DN_ALPHA * x_ref[rows, :] + mix_ref[rows, :],
                                     ln_g_ref[...], ln_b_ref[...]).astype(o_ref.dtype)

    tiles_per_sub = D_MODEL // OUT_TN
    n_tiles = (o_ref.shape[0] // ROW_SUB) * tiles_per_sub
    ln_rows = ROW_SUB // 2
    ln_queue = []
    pieces = [gate_preactivation]
    for t in range(n_tiles):
        sub, n = divmod(t, tiles_per_sub)
        lo = len(vector_pieces) * t // n_tiles
        hi = len(vector_pieces) * (t + 1) // n_tiles
        pieces += vector_pieces[lo:hi] + ln_queue[:1] + [functools.partial(out_proj_tile, sub, n)]
        del ln_queue[:1]
        if n == tiles_per_sub - 1:
            ln_queue += [functools.partial(layer_norm_rows, sub * ROW_SUB + r, ln_rows)
                         for r in range(0, ROW_SUB, ln_rows)]
    for piece in pieces + chunk_pieces + ln_queue:
        piece()


def _mixer(proj, z_low, x2, w_out_b, conv_w8, conv_g, wg_b, gate_bias, gla_g, ln_g, ln_b,
           w_ff_down, layer, bsz, seq):
    m = x2.shape[0]
    n_blk = m // MIX_T
    cur_blk = lambda s: (jnp.minimum(s, n_blk - 1), 0)
    prev_blk = lambda s: (jnp.maximum(s - 1, 0), 0)
    const2 = lambda s: (0, 0)
    slab_rows = D_FF // n_blk
    return pl.pallas_call(
        functools.partial(_mixer_kernel, blocks_per_seq=seq // MIX_T),
        grid=(n_blk + 1,),
        in_specs=[pl.BlockSpec((MIX_T, D_PROJ_MAIN), cur_blk),
                  pl.BlockSpec((MIX_T, GATE_RANK), cur_blk),
                  pl.BlockSpec((MIX_T, D_MODEL), prev_blk),
                  pl.BlockSpec((D_CONV + D_GLA_V, D_MODEL), const2, pipeline_mode=pl.Buffered(1)),
                  pl.BlockSpec((8, D_CONV), const2),
                  pl.BlockSpec((1, D_CONV), const2),
                  pl.BlockSpec((GATE_RANK, D_GLA_K), const2),
                  pl.BlockSpec((1, D_GLA_K), const2),
                  pl.BlockSpec((1, D_GLA_V), const2),
                  pl.BlockSpec((1, D_MODEL), const2),
                  pl.BlockSpec((1, D_MODEL), const2),
                  pl.BlockSpec((None, slab_rows, D_MODEL),
                               lambda s: (layer, jnp.minimum(s, n_blk - 1), 0))],
        out_specs=[pl.BlockSpec((MIX_T, D_MODEL), prev_blk),
                   pl.BlockSpec((slab_rows, D_MODEL), cur_blk)],
        out_shape=[jax.ShapeDtypeStruct((m, D_MODEL), BF16),
                   jax.ShapeDtypeStruct((D_FF, D_MODEL), BF16)],
        scratch_shapes=[pltpu.VMEM((MIX_T, D_CONV + D_GLA_V), BF16),
                        pltpu.VMEM((MIX_T, D_CONV + D_GLA_V), BF16),
                        pltpu.VMEM((MIX_T, D_MODEL), F32),
                        pltpu.VMEM((GLA_HEADS, HEAD_K, HEAD_V), F32),
                        pltpu.VMEM((8, D_CONV), F32),
                        pltpu.VMEM((GLA_HEADS, MIX_T, HEAD_K), BF16),
                        pltpu.VMEM((GLA_HEADS, MIX_T, HEAD_K), BF16),
                        pltpu.VMEM((GLA_HEADS, MIX_T, HEAD_K), BF16),
                        pltpu.VMEM((GLA_HEADS, HEAD_K, HEAD_K), F32),
                        pltpu.VMEM((MIX_T, D_GLA_K), F32),
                        pltpu.VMEM((GLA_HEADS, MIX_T, CHUNK), BF16),
                        pltpu.VMEM((GLA_HEADS, MIX_T // CHUNK, HEAD_K, HEAD_V), BF16)],
        compiler_params=pltpu.CompilerParams(
            dimension_semantics=("arbitrary",), vmem_limit_bytes=VMEM_LIMIT),
        name="mixer",
    )(proj, z_low, x2, w_out_b, conv_w8, conv_g, wg_b, gate_bias, gla_g, ln_g, ln_b, w_ff_down)


def _ffn_ln2_kernel(x_ref, wu_ref, wd_ref, g_ref, b_ref, o_ref):
    f = pl.program_id(1)
    h = jnp.dot(x_ref[...], wu_ref[...], preferred_element_type=F32)
    h = jnp.maximum(h, 0.0)
    h = (h * h).astype(BF16)

    @pl.when(f == 0)
    def _():
        o_ref[...] = jnp.dot(h, wd_ref[...], preferred_element_type=F32)

    last = pl.num_programs(1) - 1

    @pl.when(jnp.logical_and(f > 0, f < last))
    def _():
        o_ref[...] += jnp.dot(h, wd_ref[...], preferred_element_type=F32)

    @pl.when(f == last)
    def _():
        for s in range(o_ref.shape[0] // ROW_SUB):
            rows = slice(s * ROW_SUB, (s + 1) * ROW_SUB)
            ff = o_ref[rows, :] + jnp.dot(h[rows, :], wd_ref[...], preferred_element_type=F32)
            o_ref[rows, :] = _layer_norm(DN_ALPHA * x_ref[rows, :].astype(F32) + ff,
                                         g_ref[...], b_ref[...])


def _ffn_ln2(x1, w_up_b, w_down_b, ln_g, ln_b):
    m = x1.shape[0]
    return pl.pallas_call(
        _ffn_ln2_kernel,
        grid=(m // FFN_TM, D_FF // FFN_TF),
        in_specs=[pl.BlockSpec((FFN_TM, D_MODEL), lambda i, f: (i, 0)),
                  pl.BlockSpec((D_MODEL, FFN_TF), lambda i, f: (0, f)),
                  pl.BlockSpec((FFN_TF, D_MODEL), lambda i, f: (f, 0)),
                  pl.BlockSpec((1, D_MODEL), lambda i, f: (0, 0)),
                  pl.BlockSpec((1, D_MODEL), lambda i, f: (0, 0))],
        out_specs=pl.BlockSpec((FFN_TM, D_MODEL), lambda i, f: (i, 0)),
        out_shape=jax.ShapeDtypeStruct((m, D_MODEL), F32),
        compiler_params=pltpu.CompilerParams(
            dimension_semantics=("parallel", "arbitrary"), vmem_limit_bytes=VMEM_LIMIT),
        name="ffn_ln2",
    )(x1, w_up_b, w_down_b, ln_g, ln_b)


def kernel(x, w_in, conv_w, conv_norm_g, w_gate_up, gate_bias, gla_norm_g, w_out,
           ln1_g, ln1_b, w_ff_up, w_ff_down, ln2_g, ln2_b):
    bsz, seq, _ = x.shape
    assert seq % MIX_T == 0 and MIX_T % CHUNK == 0 and MIX_T % ROW_SUB == 0
    x2 = x.reshape(bsz * seq, D_MODEL)
    w_in_t = jnp.swapaxes(w_in, 1, 2)
    for l in range(DEPTH):
        wg_b = w_gate_up[l].astype(BF16)
        gb = gate_bias[l].reshape(1, D_GLA_K)
        conv_w8 = jnp.pad(conv_w[l], ((0, 8 - conv_w.shape[1]), (0, 0)))

        proj, z_low, w_out_b, w_up_b = _in_proj(x2, w_in_t, w_out, w_ff_up, l)
        x1, w_down_b = _mixer(proj, z_low, x2, w_out_b, conv_w8, conv_norm_g[l].reshape(1, D_CONV),
                              wg_b, gb, gla_norm_g[l].reshape(1, D_GLA_V),
                              ln1_g[l].reshape(1, D_MODEL), ln1_b[l].reshape(1, D_MODEL),
                              w_ff_down, l, bsz, seq)
        x2 = _ffn_ln2(x1, w_up_b, w_down_b,
                      ln2_g[l].reshape(1, D_MODEL), ln2_b[l].reshape(1, D_MODEL))
    return x2.reshape(bsz, seq, D_MODEL)
```

```python
import functools

import jax
import jax.numpy as jnp
from jax import lax
from jax.experimental import pallas as pl
from jax.experimental.pallas import tpu as pltpu

F32 = jnp.float32
BF16 = jnp.bfloat16

D_MODEL = 2048
D_CONV = 1024
CONV_GROUPS = 8
CONV_GROUP_WIDTH = D_CONV // CONV_GROUPS
GLA_HEADS = 4
HEAD_K = 128
HEAD_V = 256
D_GLA_K = GLA_HEADS * HEAD_K
D_GLA_V = GLA_HEADS * HEAD_V
GATE_RANK = 16
GATE_TAU = 16.0
CHUNK = 64
D_FF = 4 * D_MODEL
LN_EPS = 1e-5
RMS_EPS = 1e-6
DEPTH = 1
DN_ALPHA = (2.0 * DEPTH) ** 0.25

D_PROJ_MAIN = 3 * D_CONV + 2 * D_GLA_K + 2 * D_GLA_V

COL_B, COL_C, COL_U = 0, D_CONV, 2 * D_CONV
COL_Q = 3 * D_CONV
COL_K = COL_Q + D_GLA_K
COL_V = COL_K + D_GLA_K
COL_R = COL_V + D_GLA_V

VMEM_LIMIT = 60 * 1024 * 1024
SUBLANES = 8

PROJ_TM = 1024
PROJ_HEAD_TM = 2048
PROJ_HEAD_TN = 512
PROJ_TN = 1024
CAST_SLABS = 32
MIX_T = 512
ROW_SUB = 256
OUT_TN = 512
FFN_TM = 1024
FFN_TF = 1024

_NT_DIMS = (((1,), (1,)), ((), ()))
_TN_DIMS = (((0,), (0,)), ((), ()))


def _layer_norm(y, g, b):
    mu = jnp.mean(y, axis=-1, keepdims=True)
    yc = y - mu
    var = jnp.mean(yc * yc, axis=-1, keepdims=True)
    return yc * lax.rsqrt(var + LN_EPS) * g + b


def _in_proj_head_kernel(x_ref, w_ref, wz_ref, o_ref, z_ref, wb_ref, wzb_ref, xb_ref):
    @pl.when(pl.program_id(0) == 0)
    def _():
        xb_ref[...] = x_ref[...].astype(BF16)
        wzb_ref[...] = wz_ref[...].astype(BF16)
        z_ref[...] = lax.dot_general(xb_ref[...], wzb_ref[...], _NT_DIMS,
                                     preferred_element_type=F32).astype(z_ref.dtype)

    wb_ref[...] = w_ref[...].astype(BF16)
    o_ref[...] = lax.dot_general(xb_ref[...], wb_ref[...], _NT_DIMS,
                                 preferred_element_type=F32).astype(o_ref.dtype)


def _in_proj_rest_kernel(x_ref, wb_ref, wzb_ref, proj_hbm_ref, z_hbm_ref, wo_ref, wu_ref,
                         o_ref, z_ref, wo_b_ref, wu_b_ref, xb_ref):
    del proj_hbm_ref, z_hbm_ref

    @pl.when(pl.program_id(1) == 0)
    def _():
        xb_ref[...] = x_ref[...].astype(BF16)
        z_ref[...] = lax.dot_general(xb_ref[...], wzb_ref[...], _NT_DIMS,
                                     preferred_element_type=F32).astype(z_ref.dtype)

    wo_b_ref[...] = wo_ref[...].astype(BF16)
    wu_b_ref[...] = wu_ref[...].astype(BF16)
    o_ref[...] = lax.dot_general(xb_ref[...], wb_ref[...], _NT_DIMS,
                                 preferred_element_type=F32).astype(o_ref.dtype)


def _in_proj(x2, w_in_t, w_out, w_ff_up, layer):
    m = x2.shape[0]
    i0 = PROJ_HEAD_TM // PROJ_TM
    n_i = m // PROJ_TM - i0
    proj_shape = jax.ShapeDtypeStruct((m, D_PROJ_MAIN), BF16)
    z_shape = jax.ShapeDtypeStruct((m, GATE_RANK), BF16)

    n_jh = D_PROJ_MAIN // PROJ_HEAD_TN
    proj, z_low, w_in_b, wz_b = pl.pallas_call(
        _in_proj_head_kernel,
        grid=(n_jh,),
        in_specs=[pl.BlockSpec((PROJ_HEAD_TM, D_MODEL), lambda j: (0, 0), pipeline_mode=pl.Buffered(1)),
                  pl.BlockSpec((None, PROJ_HEAD_TN, D_MODEL), lambda j: (layer, j, 0)),
                  pl.BlockSpec((None, GATE_RANK, D_MODEL),
                               lambda j: (layer, D_PROJ_MAIN // GATE_RANK, 0))],
        out_specs=[pl.BlockSpec((PROJ_HEAD_TM, PROJ_HEAD_TN), lambda j: (0, j)),
                   pl.BlockSpec((PROJ_HEAD_TM, GATE_RANK), lambda j: (0, 0)),
                   pl.BlockSpec((PROJ_HEAD_TN, D_MODEL), lambda j: (j, 0)),
                   pl.BlockSpec((GATE_RANK, D_MODEL), lambda j: (0, 0))],
        out_shape=[proj_shape, z_shape,
                   jax.ShapeDtypeStruct((D_PROJ_MAIN, D_MODEL), BF16),
                   jax.ShapeDtypeStruct((GATE_RANK, D_MODEL), BF16)],
        scratch_shapes=[pltpu.VMEM((PROJ_HEAD_TM, D_MODEL), BF16)],
        compiler_params=pltpu.CompilerParams(
            dimension_semantics=("arbitrary",), vmem_limit_bytes=VMEM_LIMIT),
        name="in_proj_head",
    )(x2, w_in_t, w_in_t)

    n_j = D_PROJ_MAIN // PROJ_TN
    slab_rows = D_MODEL // CAST_SLABS
    slab = lambda i, j: jnp.minimum(i * n_j + j, CAST_SLABS - 1)
    return pl.pallas_call(
        _in_proj_rest_kernel,
        grid=(n_i, n_j),
        in_specs=[pl.BlockSpec((PROJ_TM, D_MODEL), lambda i, j: (i + i0, 0)),
                  pl.BlockSpec((PROJ_TN, D_MODEL), lambda i, j: (j, 0)),
                  pl.BlockSpec((GATE_RANK, D_MODEL), lambda i, j: (0, 0)),
                  pl.BlockSpec(memory_space=pl.ANY),
                  pl.BlockSpec(memory_space=pl.ANY),
                  pl.BlockSpec((None, slab_rows, D_MODEL), lambda i, j: (layer, slab(i, j), 0)),
                  pl.BlockSpec((None, slab_rows, D_FF), lambda i, j: (layer, slab(i, j), 0))],
        out_specs=[pl.BlockSpec((PROJ_TM, PROJ_TN), lambda i, j: (i + i0, j)),
                   pl.BlockSpec((PROJ_TM, GATE_RANK), lambda i, j: (i + i0, 0)),
                   pl.BlockSpec((slab_rows, D_MODEL), lambda i, j: (slab(i, j), 0)),
                   pl.BlockSpec((slab_rows, D_FF), lambda i, j: (slab(i, j), 0))],
        out_shape=[proj_shape, z_shape,
                   jax.ShapeDtypeStruct((D_MODEL, D_MODEL), BF16),
                   jax.ShapeDtypeStruct((D_MODEL, D_FF), BF16)],
        input_output_aliases={3: 0, 4: 1},
        scratch_shapes=[pltpu.VMEM((PROJ_TM, D_MODEL), BF16)],
        compiler_params=pltpu.CompilerParams(
            dimension_semantics=("arbitrary", "arbitrary"), vmem_limit_bytes=VMEM_LIMIT),
        name="in_proj_rest",
    )(x2, w_in_b, wz_b, proj, z_low, w_out, w_ff_up)


def _shift_rows(h3, carry3, shift):
    pos = lax.broadcasted_iota(jnp.int32, h3.shape, 1)
    rolled = pltpu.roll(h3, shift, axis=1)
    rolled_prev = jnp.concatenate([pltpu.roll(carry3, shift, axis=1), rolled[:-1]], axis=0)
    return jnp.where(pos < shift, rolled_prev, rolled)


def _conv_piece(grp, proj_ref, w_ref, g_ref, carry_ref, y_ref):
    ts = proj_ref.shape[0]
    lo = grp * CONV_GROUP_WIDTH
    sl = slice(lo, lo + CONV_GROUP_WIDTH)
    h = (proj_ref[:, COL_C + lo:COL_C + lo + CONV_GROUP_WIDTH].astype(F32)
         * proj_ref[:, COL_U + lo:COL_U + lo + CONV_GROUP_WIDTH].astype(F32))
    h3 = h.reshape(ts // SUBLANES, SUBLANES, CONV_GROUP_WIDTH)
    carry3 = carry_ref[:, sl].reshape(1, SUBLANES, CONV_GROUP_WIDTH)
    h1 = _shift_rows(h3, carry3, 1).reshape(h.shape)
    h2 = _shift_rows(h3, carry3, 2).reshape(h.shape)
    carry_ref[:, sl] = h[ts - SUBLANES:, :]
    w = w_ref[:, sl]
    y = (proj_ref[:, COL_B + lo:COL_B + lo + CONV_GROUP_WIDTH].astype(F32)
         * (w[2:3, :] * h + w[1:2, :] * h1 + w[0:1, :] * h2))
    ms = jnp.mean(y * y, axis=-1, keepdims=True)
    y_ref[:, sl] = (y * lax.rsqrt(ms + RMS_EPS) * g_ref[:, sl]).astype(y_ref.dtype)


def _chunk_cumsum(x):
    t_blk, width = x.shape
    x3 = x.reshape(t_blk // SUBLANES, SUBLANES, width)
    pos = lax.broadcasted_iota(jnp.int32, x3.shape, 1)
    shift = 1
    while shift < SUBLANES:
        x3 = x3 + jnp.where(pos >= shift, pltpu.roll(x3, shift, axis=1), 0.0)
        shift *= 2
    vregs_per_chunk = CHUNK // SUBLANES
    x4 = x3.reshape(t_blk // CHUNK, vregs_per_chunk, SUBLANES, width)
    outs = [x4[:, 0]]
    for j in range(1, vregs_per_chunk):
        outs.append(x4[:, j] + outs[-1][:, SUBLANES - 1:SUBLANES, :])
    return jnp.stack(outs, axis=1).reshape(t_blk, width)


def _gla_prepare(head, proj_ref, z_ref, gb_ref, qd_ref, ki_ref, ke_ref, dct_ref):
    t_blk = proj_ref.shape[0]
    n_chunks = t_blk // CHUNK
    col_q, col_k = COL_Q + head * HEAD_K, COL_K + head * HEAD_K
    z = z_ref[:, head * HEAD_K:(head + 1) * HEAD_K] + gb_ref[:, head * HEAD_K:(head + 1) * HEAD_K]
    log_sig = jnp.minimum(z, 0.0) - jnp.log(1.0 + jnp.exp(-jnp.abs(z)))
    bcum = _chunk_cumsum(log_sig * (1.0 / GATE_TAU))
    qd_ref[head] = ((proj_ref[:, col_q:col_q + HEAD_K].astype(F32) * (HEAD_K ** -0.5))
                    * jnp.exp(bcum)).astype(BF16)
    k_inv_f = proj_ref[:, col_k:col_k + HEAD_K].astype(F32) * jnp.exp(-bcum)
    ki_ref[head] = k_inv_f.astype(BF16)
    decay = jnp.exp(bcum.reshape(n_chunks, CHUNK, HEAD_K)[:, CHUNK - 1:CHUNK, :])
    ke_ref[head] = (k_inv_f.reshape(n_chunks, CHUNK, HEAD_K) * decay).astype(BF16).reshape(
        t_blk, HEAD_K)
    pad = jnp.zeros((HEAD_K - n_chunks, HEAD_K), F32)
    dct_ref[head] = jnp.transpose(jnp.concatenate([decay.reshape(n_chunks, HEAD_K), pad], axis=0))


def _gla_scores(head, c, qd_ref, ki_ref, sc_ref):
    rows = slice(c * CHUNK, (c + 1) * CHUNK)
    causal = (lax.broadcasted_iota(jnp.int32, (CHUNK, CHUNK), 0)
              >= lax.broadcasted_iota(jnp.int32, (CHUNK, CHUNK), 1))
    scores = lax.dot_general(qd_ref[head, rows, :], ki_ref[head, rows, :], _NT_DIMS,
                             preferred_element_type=F32)
    sc_ref[head, rows, :] = jnp.where(causal, scores, 0.0).astype(sc_ref.dtype)


def _gla_state_step(head, c, proj_ref, ke_ref, dct_ref, state_ref, st_ref):
    rows = slice(c * CHUNK, (c + 1) * CHUNK)
    col_v = COL_V + head * HEAD_V
    state = state_ref[head]
    st_ref[head, c] = state.astype(st_ref.dtype)
    delta = lax.dot_general(ke_ref[head, rows, :], proj_ref[rows, col_v:col_v + HEAD_V], _TN_DIMS,
                            preferred_element_type=F32)
    state_ref[head] = dct_ref[head, :, c:c + 1] * state + delta


def _gla_output(head, c, proj_ref, g_ref, qd_ref, sc_ref, st_ref, y_ref):
    rows = slice(c * CHUNK, (c + 1) * CHUNK)
    col_v, col_r = COL_V + head * HEAD_V, COL_R + head * HEAD_V
    col_y = D_CONV + head * HEAD_V
    o = (jnp.dot(sc_ref[head, rows, :], proj_ref[rows, col_v:col_v + HEAD_V],
                 preferred_element_type=F32)
         + jnp.dot(qd_ref[head, rows, :], st_ref[head, c], preferred_element_type=F32))
    ms = jnp.mean(o * o, axis=-1, keepdims=True)
    o_n = o * lax.rsqrt(ms + RMS_EPS) * g_ref[:, head * HEAD_V:(head + 1) * HEAD_V]
    r_c = proj_ref[rows, col_r:col_r + HEAD_V].astype(F32)
    y_ref[rows, col_y:col_y + HEAD_V] = (o_n * (r_c * jax.nn.sigmoid(r_c))).astype(y_ref.dtype)


def _mixer_kernel(proj_ref, zl_ref, x_ref, wo_ref, cw_ref, cg_ref, wg_ref, gb_ref, gg_ref,
                  ln_g_ref, ln_b_ref, wd_ref, o_ref, wd_b_ref, y_ref, yp_ref, mix_ref, state_ref,
                  carry_ref, qd_ref, ki_ref, ke_ref, dct_ref, z_ref, sc_ref, st_ref, *, blocks_per_seq):
    s = pl.program_id(0)

    @pl.when(s == 0)
    def _():
        y_ref[...] = jnp.zeros_like(y_ref)

    @pl.when(s % blocks_per_seq == 0)
    def _():
        state_ref[...] = jnp.zeros_like(state_ref)
        carry_ref[...] = jnp.zeros_like(carry_ref)

    yp_ref[...] = y_ref[...]
    wd_b_ref[...] = wd_ref[...].astype(BF16)

    gla_refs = (qd_ref, ki_ref, ke_ref, dct_ref)
    n_chunks = proj_ref.shape[0] // CHUNK
    vector_pieces = [functools.partial(_conv_piece, grp, proj_ref, cw_ref, cg_ref, carry_ref, y_ref)
                     for grp in range(CONV_GROUPS)]
    vector_pieces += [functools.partial(_gla_prepare, head, proj_ref, z_ref, gb_ref, *gla_refs)
                      for head in range(GLA_HEADS)]
    head_chunks = [(head, c) for c in range(n_chunks) for head in range(GLA_HEADS)]
    chunk_pieces = (
        [functools.partial(_gla_scores, h, c, qd_ref, ki_ref, sc_ref) for h, c in head_chunks]
        + [functools.partial(_gla_state_step, h, c, proj_ref, ke_ref, dct_ref, state_ref, st_ref)
           for h, c in head_chunks]
        + [functools.partial(_gla_output, h, c, proj_ref, gg_ref, qd_ref, sc_ref, st_ref, y_ref)
           for h, c in head_chunks])

    def gate_preactivation():
        z_ref[...] = jnp.dot(zl_ref[...], wg_ref[...], preferred_element_type=F32)

    def out_proj_tile(sub, n):
        rows = slice(sub * ROW_SUB, (sub + 1) * ROW_SUB)
        cols = slice(n * OUT_TN, (n + 1) * OUT_TN)
        mix_ref[rows, cols] = jnp.dot(yp_ref[rows, :], wo_ref[:, cols], preferred_element_type=F32)

    def layer_norm_rows(row0, n_rows):
        rows = slice(row0, row0 + n_rows)
        o_ref[rows, :] = _layer_norm(DN_ALPHA * x_ref[rows, :] + mix_ref[rows, :],
                                     ln_g_ref[...], ln_b_ref[...]).astype(o_ref.dtype)

    tiles_per_sub = D_MODEL // OUT_TN
    n_tiles = (o_ref.shape[0] // ROW_SUB) * tiles_per_sub
    ln_rows = ROW_SUB // 2
    ln_queue = []
    pieces = [gate_preactivation]
    for t in range(n_tiles):
        sub, n = divmod(t, tiles_per_sub)
        lo = len(vector_pieces) * t // n_tiles
        hi = len(vector_pieces) * (t + 1) // n_tiles
        pieces += vector_pieces[lo:hi] + ln_queue[:1] + [functools.partial(out_proj_tile, sub, n)]
        del ln_queue[:1]
        if n == tiles_per_sub - 1:
            ln_queue += [functools.partial(layer_norm_rows, sub * ROW_SUB + r, ln_rows)
                         for r in range(0, ROW_SUB, ln_rows)]
    for piece in pieces + chunk_pieces + ln_queue:
        piece()


def _mixer(proj, z_low, x2, w_out_b, conv_w8, conv_g, wg_b, gate_bias, gla_g, ln_g, ln_b,
           w_ff_down, layer, bsz, seq):
    m = x2.shape[0]
    n_blk = m // MIX_T
    cur_blk = lambda s: (jnp.minimum(s, n_blk - 1), 0)
    prev_blk = lambda s: (jnp.maximum(s - 1, 0), 0)
    const2 = lambda s: (0, 0)
    slab_rows = D_FF // n_blk
    return pl.pallas_call(
        functools.partial(_mixer_kernel, blocks_per_seq=seq // MIX_T),
        grid=(n_blk + 1,),
        in_specs=[pl.BlockSpec((MIX_T, D_PROJ_MAIN), cur_blk),
                  pl.BlockSpec((MIX_T, GATE_RANK), cur_blk),
                  pl.BlockSpec((MIX_T, D_MODEL), prev_blk),
                  pl.BlockSpec((D_CONV + D_GLA_V, D_MODEL), const2, pipeline_mode=pl.Buffered(1)),
                  pl.BlockSpec((8, D_CONV), const2),
                  pl.BlockSpec((1, D_CONV), const2),
                  pl.BlockSpec((GATE_RANK, D_GLA_K), const2),
                  pl.BlockSpec((1, D_GLA_K), const2),
                  pl.BlockSpec((1, D_GLA_V), const2),
                  pl.BlockSpec((1, D_MODEL), const2),
                  pl.BlockSpec((1, D_MODEL), const2),
                  pl.BlockSpec((None, slab_rows, D_MODEL),
                               lambda s: (layer, jnp.minimum(s, n_blk - 1), 0))],
        out_specs=[pl.BlockSpec((MIX_T, D_MODEL), prev_blk),
                   pl.BlockSpec((slab_rows, D_MODEL), cur_blk)],
        out_shape=[jax.ShapeDtypeStruct((m, D_MODEL), BF16),
                   jax.ShapeDtypeStruct((D_FF, D_MODEL), BF16)],
        scratch_shapes=[pltpu.VMEM((MIX_T, D_CONV + D_GLA_V), BF16),
                        pltpu.VMEM((MIX_T, D_CONV + D_GLA_V), BF16),
                        pltpu.VMEM((MIX_T, D_MODEL), F32),
                        pltpu.VMEM((GLA_HEADS, HEAD_K, HEAD_V), F32),
                        pltpu.VMEM((8, D_CONV), F32),
                        pltpu.VMEM((GLA_HEADS, MIX_T, HEAD_K), BF16),
                        pltpu.VMEM((GLA_HEADS, MIX_T, HEAD_K), BF16),
                        pltpu.VMEM((GLA_HEADS, MIX_T, HEAD_K), BF16),
                        pltpu.VMEM((GLA_HEADS, HEAD_K, HEAD_K), F32),
                        pltpu.VMEM((MIX_T, D_GLA_K), F32),
                        pltpu.VMEM((GLA_HEADS, MIX_T, CHUNK), BF16),
                        pltpu.VMEM((GLA_HEADS, MIX_T // CHUNK, HEAD_K, HEAD_V), BF16)],
        compiler_params=pltpu.CompilerParams(
            dimension_semantics=("arbitrary",), vmem_limit_bytes=VMEM_LIMIT),
        name="mixer",
    )(proj, z_low, x2, w_out_b, conv_w8, conv_g, wg_b, gate_bias, gla_g, ln_g, ln_b, w_ff_down)


def _ffn_ln2_kernel(x_ref, wu_ref, wd_ref, g_ref, b_ref, o_ref):
    f = pl.program_id(1)
    last = pl.num_programs(1) - 1

    def hidden():
        h = jnp.dot(x_ref[...], wu_ref[...], preferred_element_type=F32)
        h = jnp.maximum(h, 0.0)
        return (h * h).astype(BF16)

    @pl.when(f == 0)
    def _():
        o_ref[...] = jnp.dot(hidden(), wd_ref[...], preferred_element_type=F32)

    @pl.when(jnp.logical_and(f > 0, f < last))
    def _():
        o_ref[...] += jnp.dot(hidden(), wd_ref[...], preferred_element_type=F32)

    @pl.when(f == last)
    def _():
        h = hidden()
        for s in range(o_ref.shape[0] // ROW_SUB):
            rows = slice(s * ROW_SUB, (s + 1) * ROW_SUB)
            ff = o_ref[rows, :] + jnp.dot(h[rows, :], wd_ref[...], preferred_element_type=F32)
            o_ref[rows, :] = _layer_norm(DN_ALPHA * x_ref[rows, :].astype(F32) + ff,
                                         g_ref[...], b_ref[...])


def _ffn_ln2(x1, w_up_b, w_down_b, ln_g, ln_b):
    m = x1.shape[0]
    return pl.pallas_call(
        _ffn_ln2_kernel,
        grid=(m // FFN_TM, D_FF // FFN_TF),
        in_specs=[pl.BlockSpec((FFN_TM, D_MODEL), lambda i, f: (i, 0)),
                  pl.BlockSpec((D_MODEL, FFN_TF), lambda i, f: (0, f)),
                  pl.BlockSpec((FFN_TF, D_MODEL), lambda i, f: (f, 0)),
                  pl.BlockSpec((1, D_MODEL), lambda i, f: (0, 0)),
                  pl.BlockSpec((1, D_MODEL), lambda i, f: (0, 0))],
        out_specs=pl.BlockSpec((FFN_TM, D_MODEL), lambda i, f: (i, 0)),
        out_shape=jax.ShapeDtypeStruct((m, D_MODEL), F32),
        compiler_params=pltpu.CompilerParams(
            dimension_semantics=("parallel", "arbitrary"), vmem_limit_bytes=VMEM_LIMIT),
        name="ffn_ln2",
    )(x1, w_up_b, w_down_b, ln_g, ln_b)


def kernel(x, w_in, conv_w, conv_norm_g, w_gate_up, gate_bias, gla_norm_g, w_out,
           ln1_g, ln1_b, w_ff_up, w_ff_down, ln2_g, ln2_b):
    bsz, seq, _ = x.shape
    assert seq % MIX_T == 0 and MIX_T % CHUNK == 0 and MIX_T % ROW_SUB == 0
    x2 = x.reshape(bsz * seq, D_MODEL)
    w_in_t = jnp.swapaxes(w_in, 1, 2)
    for l in range(DEPTH):
        wg_b = w_gate_up[l].astype(BF16)
        gb = gate_bias[l].reshape(1, D_GLA_K)
        conv_w8 = jnp.pad(conv_w[l], ((0, 8 - conv_w.shape[1]), (0, 0)))

        proj, z_low, w_out_b, w_up_b = _in_proj(x2, w_in_t, w_out, w_ff_up, l)
        x1, w_down_b = _mixer(proj, z_low, x2, w_out_b, conv_w8, conv_norm_g[l].reshape(1, D_CONV),
                              wg_b, gb, gla_norm_g[l].reshape(1, D_GLA_V),
                              ln1_g[l].reshape(1, D_MODEL), ln1_b[l].reshape(1, D_MODEL),
                              w_ff_down, l, bsz, seq)
        x2 = _ffn_ln2(x1, w_up_b, w_down_b,
                      ln2_g[l].reshape(1, D_MODEL), ln2_b[l].reshape(1, D_MODEL))
    return x2.reshape(bsz, seq, D_MODEL)
```

```python
import functools

import jax
import jax.numpy as jnp
from jax import lax
from jax.experimental import pallas as pl
from jax.experimental.pallas import tpu as pltpu

F32 = jnp.float32
BF16 = jnp.bfloat16

D_MODEL = 2048
D_CONV = 1024
CONV_GROUPS = 8
CONV_GROUP_WIDTH = D_CONV // CONV_GROUPS
GLA_HEADS = 4
HEAD_K = 128
HEAD_V = 256
D_GLA_K = GLA_HEADS * HEAD_K
D_GLA_V = GLA_HEADS * HEAD_V
GATE_RANK = 16
GATE_TAU = 16.0
CHUNK = 64
D_FF = 4 * D_MODEL
LN_EPS = 1e-5
RMS_EPS = 1e-6
DEPTH = 1
DN_ALPHA = (2.0 * DEPTH) ** 0.25

D_PROJ_MAIN = 3 * D_CONV + 2 * D_GLA_K + 2 * D_GLA_V

COL_B, COL_C, COL_U = 0, D_CONV, 2 * D_CONV
COL_Q = 3 * D_CONV
COL_K = COL_Q + D_GLA_K
COL_V = COL_K + D_GLA_K
COL_R = COL_V + D_GLA_V

VMEM_LIMIT = 60 * 1024 * 1024
SUBLANES = 8

PROJ_TM = 1024
PROJ_HEAD_TM = 2048
PROJ_HEAD_TN = 512
PROJ_TN = 1024
CAST_SLABS = 32
MIX_T = 512
ROW_SUB = 256
OUT_TN = 512
FFN_TM = 2048
FFN_TF = 512

_NT_DIMS = (((1,), (1,)), ((), ()))
_TN_DIMS = (((0,), (0,)), ((), ()))


def _layer_norm(y, g, b):
    mu = jnp.mean(y, axis=-1, keepdims=True)
    yc = y - mu
    var = jnp.mean(yc * yc, axis=-1, keepdims=True)
    return yc * lax.rsqrt(var + LN_EPS) * g + b


def _in_proj_head_kernel(x_ref, w_ref, wz_ref, o_ref, z_ref, wb_ref, wzb_ref, xb_ref):
    @pl.when(pl.program_id(0) == 0)
    def _():
        xb_ref[...] = x_ref[...].astype(BF16)
        wzb_ref[...] = wz_ref[...].astype(BF16)
        z_ref[...] = lax.dot_general(xb_ref[...], wzb_ref[...], _NT_DIMS,
                                     preferred_element_type=F32).astype(z_ref.dtype)

    wb_ref[...] = w_ref[...].astype(BF16)
    o_ref[...] = lax.dot_general(xb_ref[...], wb_ref[...], _NT_DIMS,
                                 preferred_element_type=F32).astype(o_ref.dtype)


def _in_proj_rest_kernel(x_ref, wb_ref, wzb_ref, proj_hbm_ref, z_hbm_ref, wo_ref, wu_ref,
                         o_ref, z_ref, wo_b_ref, wu_b_ref, xb_ref):
    del proj_hbm_ref, z_hbm_ref

    @pl.when(pl.program_id(1) == 0)
    def _():
        xb_ref[...] = x_ref[...].astype(BF16)
        z_ref[...] = lax.dot_general(xb_ref[...], wzb_ref[...], _NT_DIMS,
                                     preferred_element_type=F32).astype(z_ref.dtype)

    wo_b_ref[...] = wo_ref[...].astype(BF16)
    wu_b_ref[...] = wu_ref[...].astype(BF16)
    o_ref[...] = lax.dot_general(xb_ref[...], wb_ref[...], _NT_DIMS,
                                 preferred_element_type=F32).astype(o_ref.dtype)


def _in_proj(x2, w_in_t, w_out, w_ff_up, layer):
    m = x2.shape[0]
    i0 = PROJ_HEAD_TM // PROJ_TM
    n_i = m // PROJ_TM - i0
    proj_shape = jax.ShapeDtypeStruct((m, D_PROJ_MAIN), BF16)
    z_shape = jax.ShapeDtypeStruct((m, GATE_RANK), BF16)

    n_jh = D_PROJ_MAIN // PROJ_HEAD_TN
    proj, z_low, w_in_b, wz_b = pl.pallas_call(
        _in_proj_head_kernel,
        grid=(n_jh,),
        in_specs=[pl.BlockSpec((PROJ_HEAD_TM, D_MODEL), lambda j: (0, 0), pipeline_mode=pl.Buffered(1)),
                  pl.BlockSpec((None, PROJ_HEAD_TN, D_MODEL), lambda j: (layer, j, 0)),
                  pl.BlockSpec((None, GATE_RANK, D_MODEL),
                               lambda j: (layer, D_PROJ_MAIN // GATE_RANK, 0))],
        out_specs=[pl.BlockSpec((PROJ_HEAD_TM, PROJ_HEAD_TN), lambda j: (0, j)),
                   pl.BlockSpec((PROJ_HEAD_TM, GATE_RANK), lambda j: (0, 0)),
                   pl.BlockSpec((PROJ_HEAD_TN, D_MODEL), lambda j: (j, 0)),
                   pl.BlockSpec((GATE_RANK, D_MODEL), lambda j: (0, 0))],
        out_shape=[proj_shape, z_shape,
                   jax.ShapeDtypeStruct((D_PROJ_MAIN, D_MODEL), BF16),
                   jax.ShapeDtypeStruct((GATE_RANK, D_MODEL), BF16)],
        scratch_shapes=[pltpu.VMEM((PROJ_HEAD_TM, D_MODEL), BF16)],
        compiler_params=pltpu.CompilerParams(
            dimension_semantics=("arbitrary",), vmem_limit_bytes=VMEM_LIMIT),
        name="in_proj_head",
    )(x2, w_in_t, w_in_t)

    n_j = D_PROJ_MAIN // PROJ_TN
    slab_rows = D_MODEL // CAST_SLABS
    slab = lambda i, j: jnp.minimum(i * n_j + j, CAST_SLABS - 1)
    return pl.pallas_call(
        _in_proj_rest_kernel,
        grid=(n_i, n_j),
        in_specs=[pl.BlockSpec((PROJ_TM, D_MODEL), lambda i, j: (i + i0, 0)),
                  pl.BlockSpec((PROJ_TN, D_MODEL), lambda i, j: (j, 0)),
                  pl.BlockSpec((GATE_RANK, D_MODEL), lambda i, j: (0, 0)),
                  pl.BlockSpec(memory_space=pl.ANY),
                  pl.BlockSpec(memory_space=pl.ANY),
                  pl.BlockSpec((None, slab_rows, D_MODEL), lambda i, j: (layer, slab(i, j), 0)),
                  pl.BlockSpec((None, slab_rows, D_FF), lambda i, j: (layer, slab(i, j), 0))],
        out_specs=[pl.BlockSpec((PROJ_TM, PROJ_TN), lambda i, j: (i + i0, j)),
                   pl.BlockSpec((PROJ_TM, GATE_RANK), lambda i, j: (i + i0, 0)),
                   pl.BlockSpec((slab_rows, D_MODEL), lambda i, j: (slab(i, j), 0)),
                   pl.BlockSpec((slab_rows, D_FF), lambda i, j: (slab(i, j), 0))],
        out_shape=[proj_shape, z_shape,
                   jax.ShapeDtypeStruct((D_MODEL, D_MODEL), BF16),
                   jax.ShapeDtypeStruct((D_MODEL, D_FF), BF16)],
        input_output_aliases={3: 0, 4: 1},
        scratch_shapes=[pltpu.VMEM((PROJ_TM, D_MODEL), BF16)],
        compiler_params=pltpu.CompilerParams(
            dimension_semantics=("arbitrary", "arbitrary"), vmem_limit_bytes=VMEM_LIMIT),
        name="in_proj_rest",
    )(x2, w_in_b, wz_b, proj, z_low, w_out, w_ff_up)


def _shift_rows(h3, carry3, shift):
    pos = lax.broadcasted_iota(jnp.int32, h3.shape, 1)
    rolled = pltpu.roll(h3, shift, axis=1)
    rolled_prev = jnp.concatenate([pltpu.roll(carry3, shift, axis=1), rolled[:-1]], axis=0)
    return jnp.where(pos < shift, rolled_prev, rolled)


def _conv_piece(grp, proj_ref, w_ref, g_ref, carry_ref, y_ref):
    ts = proj_ref.shape[0]
    lo = grp * CONV_GROUP_WIDTH
    sl = slice(lo, lo + CONV_GROUP_WIDTH)
    h = (proj_ref[:, COL_C + lo:COL_C + lo + CONV_GROUP_WIDTH].astype(F32)
         * proj_ref[:, COL_U + lo:COL_U + lo + CONV_GROUP_WIDTH].astype(F32))
    h3 = h.reshape(ts // SUBLANES, SUBLANES, CONV_GROUP_WIDTH)
    carry3 = carry_ref[:, sl].reshape(1, SUBLANES, CONV_GROUP_WIDTH)
    h1 = _shift_rows(h3, carry3, 1).reshape(h.shape)
    h2 = _shift_rows(h3, carry3, 2).reshape(h.shape)
    carry_ref[:, sl] = h[ts - SUBLANES:, :]
    w = w_ref[:, sl]
    y = (proj_ref[:, COL_B + lo:COL_B + lo + CONV_GROUP_WIDTH].astype(F32)
         * (w[2:3, :] * h + w[1:2, :] * h1 + w[0:1, :] * h2))
    ms = jnp.mean(y * y, axis=-1, keepdims=True)
    y_ref[:, sl] = (y * lax.rsqrt(ms + RMS_EPS) * g_ref[:, sl]).astype(y_ref.dtype)


def _chunk_cumsum(x):
    t_blk, width = x.shape
    x3 = x.reshape(t_blk // SUBLANES, SUBLANES, width)
    pos = lax.broadcasted_iota(jnp.int32, x3.shape, 1)
    shift = 1
    while shift < SUBLANES:
        x3 = x3 + jnp.where(pos >= shift, pltpu.roll(x3, shift, axis=1), 0.0)
        shift *= 2
    vregs_per_chunk = CHUNK // SUBLANES
    x4 = x3.reshape(t_blk // CHUNK, vregs_per_chunk, SUBLANES, width)
    outs = [x4[:, 0]]
    for j in range(1, vregs_per_chunk):
        outs.append(x4[:, j] + outs[-1][:, SUBLANES - 1:SUBLANES, :])
    return jnp.stack(outs, axis=1).reshape(t_blk, width)


def _gla_prepare(head, proj_ref, z_ref, gb_ref, qd_ref, ki_ref, ke_ref, dct_ref):
    t_blk = proj_ref.shape[0]
    n_chunks = t_blk // CHUNK
    col_q, col_k = COL_Q + head * HEAD_K, COL_K + head * HEAD_K
    z = z_ref[:, head * HEAD_K:(head + 1) * HEAD_K] + gb_ref[:, head * HEAD_K:(head + 1) * HEAD_K]
    log_sig = jnp.minimum(z, 0.0) - jnp.log(1.0 + jnp.exp(-jnp.abs(z)))
    bcum = _chunk_cumsum(log_sig * (1.0 / GATE_TAU))
    qd_ref[head] = ((proj_ref[:, col_q:col_q + HEAD_K].astype(F32) * (HEAD_K ** -0.5))
                    * jnp.exp(bcum)).astype(BF16)
    k_inv_f = proj_ref[:, col_k:col_k + HEAD_K].astype(F32) * jnp.exp(-bcum)
    ki_ref[head] = k_inv_f.astype(BF16)
    decay = jnp.exp(bcum.reshape(n_chunks, CHUNK, HEAD_K)[:, CHUNK - 1:CHUNK, :])
    ke_ref[head] = (k_inv_f.reshape(n_chunks, CHUNK, HEAD_K) * decay).astype(BF16).reshape(
        t_blk, HEAD_K)
    pad = jnp.zeros((HEAD_K - n_chunks, HEAD_K), F32)
    dct_ref[head] = jnp.transpose(jnp.concatenate([decay.reshape(n_chunks, HEAD_K), pad], axis=0))


def _gla_scores(head, c, qd_ref, ki_ref, sc_ref):
    rows = slice(c * CHUNK, (c + 1) * CHUNK)
    causal = (lax.broadcasted_iota(jnp.int32, (CHUNK, CHUNK), 0)
              >= lax.broadcasted_iota(jnp.int32, (CHUNK, CHUNK), 1))
    scores = lax.dot_general(qd_ref[head, rows, :], ki_ref[head, rows, :], _NT_DIMS,
                             preferred_element_type=F32)
    sc_ref[head, rows, :] = jnp.where(causal, scores, 0.0).astype(sc_ref.dtype)


def _gla_state_step(head, c, proj_ref, ke_ref, dct_ref, state_ref, st_ref):
    rows = slice(c * CHUNK, (c + 1) * CHUNK)
    col_v = COL_V + head * HEAD_V
    state = state_ref[head]
    st_ref[head, c] = state.astype(st_ref.dtype)
    delta = lax.dot_general(ke_ref[head, rows, :], proj_ref[rows, col_v:col_v + HEAD_V], _TN_DIMS,
                            preferred_element_type=F32)
    state_ref[head] = dct_ref[head, :, c:c + 1] * state + delta


def _gla_output(head, c, proj_ref, g_ref, qd_ref, sc_ref, st_ref, y_ref):
    rows = slice(c * CHUNK, (c + 1) * CHUNK)
    col_v, col_r = COL_V + head * HEAD_V, COL_R + head * HEAD_V
    col_y = D_CONV + head * HEAD_V
    o = (jnp.dot(sc_ref[head, rows, :], proj_ref[rows, col_v:col_v + HEAD_V],
                 preferred_element_type=F32)
         + jnp.dot(qd_ref[head, rows, :], st_ref[head, c], preferred_element_type=F32))
    ms = jnp.mean(o * o, axis=-1, keepdims=True)
    o_n = o * lax.rsqrt(ms + RMS_EPS) * g_ref[:, head * HEAD_V:(head + 1) * HEAD_V]
    r_c = proj_ref[rows, col_r:col_r + HEAD_V].astype(F32)
    y_ref[rows, col_y:col_y + HEAD_V] = (o_n * (r_c * jax.nn.sigmoid(r_c))).astype(y_ref.dtype)


def _mixer_kernel(proj_ref, zl_ref, x_ref, wo_ref, cw_ref, cg_ref, wg_ref, gb_ref, gg_ref,
                  ln_g_ref, ln_b_ref, wd_ref, o_ref, wd_b_ref, y_ref, yp_ref, mix_ref, state_ref,
                  carry_ref, qd_ref, ki_ref, ke_ref, dct_ref, z_ref, sc_ref, st_ref, *, blocks_per_seq):
    s = pl.program_id(0)

    @pl.when(s == 0)
    def _():
        y_ref[...] = jnp.zeros_like(y_ref)

    @pl.when(s % blocks_per_seq == 0)
    def _():
        state_ref[...] = jnp.zeros_like(state_ref)
        carry_ref[...] = jnp.zeros_like(carry_ref)

    yp_ref[...] = y_ref[...]
    wd_b_ref[...] = wd_ref[...].astype(BF16)

    gla_refs = (qd_ref, ki_ref, ke_ref, dct_ref)
    n_chunks = proj_ref.shape[0] // CHUNK
    vector_pieces = [functools.partial(_conv_piece, grp, proj_ref, cw_ref, cg_ref, carry_ref, y_ref)
                     for grp in range(CONV_GROUPS)]
    vector_pieces += [functools.partial(_gla_prepare, head, proj_ref, z_ref, gb_ref, *gla_refs)
                      for head in range(GLA_HEADS)]
    head_chunks = [(head, c) for c in range(n_chunks) for head in range(GLA_HEADS)]
    chunk_pieces = (
        [functools.partial(_gla_scores, h, c, qd_ref, ki_ref, sc_ref) for h, c in head_chunks]
        + [functools.partial(_gla_state_step, h, c, proj_ref, ke_ref, dct_ref, state_ref, st_ref)
           for h, c in head_chunks]
        + [functools.partial(_gla_output, h, c, proj_ref, gg_ref, qd_ref, sc_ref, st_ref, y_ref)
           for h, c in head_chunks])

    def gate_preactivation():
        z_ref[...] = jnp.dot(zl_ref[...], wg_ref[...], preferred_element_type=F32)

    def out_proj_tile(sub, n):
        rows = slice(sub * ROW_SUB, (sub + 1) * ROW_SUB)
        cols = slice(n * OUT_TN, (n + 1) * OUT_TN)
        mix_ref[rows, cols] = jnp.dot(yp_ref[rows, :], wo_ref[:, cols], preferred_element_type=F32)

    def layer_norm_rows(row0, n_rows):
        rows = slice(row0, row0 + n_rows)
        o_ref[rows, :] = _layer_norm(DN_ALPHA * x_ref[rows, :] + mix_ref[rows, :],
                                     ln_g_ref[...], ln_b_ref[...]).astype(o_ref.dtype)

    tiles_per_sub = D_MODEL // OUT_TN
    n_tiles = (o_ref.shape[0] // ROW_SUB) * tiles_per_sub
    ln_rows = ROW_SUB // 2
    ln_queue = []
    pieces = [gate_preactivation]
    for t in range(n_tiles):
        sub, n = divmod(t, tiles_per_sub)
        lo = len(vector_pieces) * t // n_tiles
        hi = len(vector_pieces) * (t + 1) // n_tiles
        pieces += vector_pieces[lo:hi] + ln_queue[:1] + [functools.partial(out_proj_tile, sub, n)]
        del ln_queue[:1]
        if n == tiles_per_sub - 1:
            ln_queue += [functools.partial(layer_norm_rows, sub * ROW_SUB + r, ln_rows)
                         for r in range(0, ROW_SUB, ln_rows)]
    for piece in pieces + chunk_pieces + ln_queue:
        piece()


def _mixer(proj, z_low, x2, w_out_b, conv_w8, conv_g, wg_b, gate_bias, gla_g, ln_g, ln_b,
           w_ff_down, layer, bsz, seq):
    m = x2.shape[0]
    n_blk = m // MIX_T
    cur_blk = lambda s: (jnp.minimum(s, n_blk - 1), 0)
    prev_blk = lambda s: (jnp.maximum(s - 1, 0), 0)
    const2 = lambda s: (0, 0)
    slab_rows = D_FF // n_blk
    return pl.pallas_call(
        functools.partial(_mixer_kernel, blocks_per_seq=seq // MIX_T),
        grid=(n_blk + 1,),
        in_specs=[pl.BlockSpec((MIX_T, D_PROJ_MAIN), cur_blk),
                  pl.BlockSpec((MIX_T, GATE_RANK), cur_blk),
                  pl.BlockSpec((MIX_T, D_MODEL), prev_blk),
                  pl.BlockSpec((D_CONV + D_GLA_V, D_MODEL), const2, pipeline_mode=pl.Buffered(1)),
                  pl.BlockSpec((8, D_CONV), const2),
                  pl.BlockSpec((1, D_CONV), const2),
                  pl.BlockSpec((GATE_RANK, D_GLA_K), const2),
                  pl.BlockSpec((1, D_GLA_K), const2),
                  pl.BlockSpec((1, D_GLA_V), const2),
                  pl.BlockSpec((1, D_MODEL), const2),
                  pl.BlockSpec((1, D_MODEL), const2),
                  pl.BlockSpec((None, slab_rows, D_MODEL),
                               lambda s: (layer, jnp.minimum(s, n_blk - 1), 0))],
        out_specs=[pl.BlockSpec((MIX_T, D_MODEL), prev_blk),
                   pl.BlockSpec((slab_rows, D_MODEL), cur_blk)],
        out_shape=[jax.ShapeDtypeStruct((m, D_MODEL), BF16),
                   jax.ShapeDtypeStruct((D_FF, D_MODEL), BF16)],
        scratch_shapes=[pltpu.VMEM((MIX_T, D_CONV + D_GLA_V), BF16),
                        pltpu.VMEM((MIX_T, D_CONV + D_GLA_V), BF16),
                        pltpu.VMEM((MIX_T, D_MODEL), F32),
                        pltpu.VMEM((GLA_HEADS, HEAD_K, HEAD_V), F32),
                        pltpu.VMEM((8, D_CONV), F32),
                        pltpu.VMEM((GLA_HEADS, MIX_T, HEAD_K), BF16),
                        pltpu.VMEM((GLA_HEADS, MIX_T, HEAD_K), BF16),
                        pltpu.VMEM((GLA_HEADS, MIX_T, HEAD_K), BF16),
                        pltpu.VMEM((GLA_HEADS, HEAD_K, HEAD_K), F32),
                        pltpu.VMEM((MIX_T, D_GLA_K), F32),
                        pltpu.VMEM((GLA_HEADS, MIX_T, CHUNK), BF16),
                        pltpu.VMEM((GLA_HEADS, MIX_T // CHUNK, HEAD_K, HEAD_V), BF16)],
        compiler_params=pltpu.CompilerParams(
            dimension_semantics=("arbitrary",), vmem_limit_bytes=VMEM_LIMIT),
        name="mixer",
    )(proj, z_low, x2, w_out_b, conv_w8, conv_g, wg_b, gate_bias, gla_g, ln_g, ln_b, w_ff_down)


def _ffn_ln2_kernel(x_ref, wu_ref, wd_ref, g_ref, b_ref, o_ref):
    f = pl.program_id(1)
    last = pl.num_programs(1) - 1

    def hidden():
        h = jnp.dot(x_ref[...], wu_ref[...], preferred_element_type=F32)
        h = jnp.maximum(h, 0.0)
        return (h * h).astype(BF16)

    @pl.when(f == 0)
    def _():
        o_ref[...] = jnp.dot(hidden(), wd_ref[...], preferred_element_type=F32)

    @pl.when(jnp.logical_and(f > 0, f < last))
    def _():
        o_ref[...] += jnp.dot(hidden(), wd_ref[...], preferred_element_type=F32)

    @pl.when(f == last)
    def _():
        h = hidden()
        for s in range(o_ref.shape[0] // ROW_SUB):
            rows = slice(s * ROW_SUB, (s + 1) * ROW_SUB)
            ff = o_ref[rows, :] + jnp.dot(h[rows, :], wd_ref[...], preferred_element_type=F32)
            o_ref[rows, :] = _layer_norm(DN_ALPHA * x_ref[rows, :].astype(F32) + ff,
                                         g_ref[...], b_ref[...])


def _ffn_ln2(x1, w_up_b, w_down_b, ln_g, ln_b):
    m = x1.shape[0]
    return pl.pallas_call(
        _ffn_ln2_kernel,
        grid=(m // FFN_TM, D_FF // FFN_TF),
        in_specs=[pl.BlockSpec((FFN_TM, D_MODEL), lambda i, f: (i, 0), pipeline_mode=pl.Buffered(1)),
                  pl.BlockSpec((D_MODEL, FFN_TF), lambda i, f: (0, f)),
                  pl.BlockSpec((FFN_TF, D_MODEL), lambda i, f: (f, 0)),
                  pl.BlockSpec((1, D_MODEL), lambda i, f: (0, 0)),
                  pl.BlockSpec((1, D_MODEL), lambda i, f: (0, 0))],
        out_specs=pl.BlockSpec((FFN_TM, D_MODEL), lambda i, f: (i, 0)),
        out_shape=jax.ShapeDtypeStruct((m, D_MODEL), F32),
        compiler_params=pltpu.CompilerParams(
            dimension_semantics=("parallel", "arbitrary"), vmem_limit_bytes=VMEM_LIMIT),
        name="ffn_ln2",
    )(x1, w_up_b, w_down_b, ln_g, ln_b)


def kernel(x, w_in, conv_w, conv_norm_g, w_gate_up, gate_bias, gla_norm_g, w_out,
           ln1_g, ln1_b, w_ff_up, w_ff_down, ln2_g, ln2_b):
    bsz, seq, _ = x.shape
    assert seq % MIX_T == 0 and MIX_T % CHUNK == 0 and MIX_T % ROW_SUB == 0
    x2 = x.reshape(bsz * seq, D_MODEL)
    w_in_t = jnp.swapaxes(w_in, 1, 2)
    for l in range(DEPTH):
        wg_b = w_gate_up[l].astype(BF16)
        gb = gate_bias[l].reshape(1, D_GLA_K)
        conv_w8 = jnp.pad(conv_w[l], ((0, 8 - conv_w.shape[1]), (0, 0)))

        proj, z_low, w_out_b, w_up_b = _in_proj(x2, w_in_t, w_out, w_ff_up, l)
        x1, w_down_b = _mixer(proj, z_low, x2, w_out_b, conv_w8, conv_norm_g[l].reshape(1, D_CONV),
                              wg_b, gb, gla_norm_g[l].reshape(1, D_GLA_V),
                              ln1_g[l].reshape(1, D_MODEL), ln1_b[l].reshape(1, D_MODEL),
                              w_ff_down, l, bsz, seq)
        x2 = _ffn_ln2(x1, w_up_b, w_down_b,
                      ln2_g[l].reshape(1, D_MODEL), ln2_b[l].reshape(1, D_MODEL))
    return x2.reshape(bsz, seq, D_MODEL)
```

```python
import functools

import jax
import jax.numpy as jnp
from jax import lax
from jax.experimental import pallas as pl
from jax.experimental.pallas import tpu as pltpu

F32 = jnp.float32
BF16 = jnp.bfloat16

D_MODEL = 2048
D_CONV = 1024
CONV_GROUPS = 8
CONV_GROUP_WIDTH = D_CONV // CONV_GROUPS
GLA_HEADS = 4
HEAD_K = 128
HEAD_V = 256
D_GLA_K = GLA_HEADS * HEAD_K
D_GLA_V = GLA_HEADS * HEAD_V
GATE_RANK = 16
GATE_TAU = 16.0
CHUNK = 64
D_FF = 4 * D_MODEL
LN_EPS = 1e-5
RMS_EPS = 1e-6
DEPTH = 1
DN_ALPHA = (2.0 * DEPTH) ** 0.25

D_PROJ_MAIN = 3 * D_CONV + 2 * D_GLA_K + 2 * D_GLA_V

COL_B, COL_C, COL_U = 0, D_CONV, 2 * D_CONV
COL_Q = 3 * D_CONV
COL_K = COL_Q + D_GLA_K
COL_V = COL_K + D_GLA_K
COL_R = COL_V + D_GLA_V

VMEM_LIMIT = 60 * 1024 * 1024
SUBLANES = 8

PROJ_TM = 1024
PROJ_HEAD_TM = 2048
PROJ_HEAD_TN = 512
PROJ_TN = 1024
CAST_SLABS = 32
MIX_T = 512
ROW_SUB = 256
OUT_TN = 512
FFN_TM = 512
FFN_TF = 2048

_NT_DIMS = (((1,), (1,)), ((), ()))
_TN_DIMS = (((0,), (0,)), ((), ()))


def _layer_norm(y, g, b):
    mu = jnp.mean(y, axis=-1, keepdims=True)
    yc = y - mu
    var = jnp.mean(yc * yc, axis=-1, keepdims=True)
    return yc * lax.rsqrt(var + LN_EPS) * g + b


def _in_proj_head_kernel(x_ref, w_ref, wz_ref, o_ref, z_ref, wb_ref, wzb_ref, xb_ref):
    @pl.when(pl.program_id(0) == 0)
    def _():
        xb_ref[...] = x_ref[...].astype(BF16)
        wzb_ref[...] = wz_ref[...].astype(BF16)
        z_ref[...] = lax.dot_general(xb_ref[...], wzb_ref[...], _NT_DIMS,
                                     preferred_element_type=F32).astype(z_ref.dtype)

    wb_ref[...] = w_ref[...].astype(BF16)
    o_ref[...] = lax.dot_general(xb_ref[...], wb_ref[...], _NT_DIMS,
                                 preferred_element_type=F32).astype(o_ref.dtype)


def _in_proj_rest_kernel(x_ref, wb_ref, wzb_ref, proj_hbm_ref, z_hbm_ref, wo_ref, wu_ref,
                         o_ref, z_ref, wo_b_ref, wu_b_ref, xb_ref):
    del proj_hbm_ref, z_hbm_ref

    @pl.when(pl.program_id(1) == 0)
    def _():
        xb_ref[...] = x_ref[...].astype(BF16)
        z_ref[...] = lax.dot_general(xb_ref[...], wzb_ref[...], _NT_DIMS,
                                     preferred_element_type=F32).astype(z_ref.dtype)

    wo_b_ref[...] = wo_ref[...].astype(BF16)
    wu_b_ref[...] = wu_ref[...].astype(BF16)
    o_ref[...] = lax.dot_general(xb_ref[...], wb_ref[...], _NT_DIMS,
                                 preferred_element_type=F32).astype(o_ref.dtype)


def _in_proj(x2, w_in_t, w_out, w_ff_up, layer):
    m = x2.shape[0]
    i0 = PROJ_HEAD_TM // PROJ_TM
    n_i = m // PROJ_TM - i0
    proj_shape = jax.ShapeDtypeStruct((m, D_PROJ_MAIN), BF16)
    z_shape = jax.ShapeDtypeStruct((m, GATE_RANK), BF16)

    n_jh = D_PROJ_MAIN // PROJ_HEAD_TN
    proj, z_low, w_in_b, wz_b = pl.pallas_call(
        _in_proj_head_kernel,
        grid=(n_jh,),
        in_specs=[pl.BlockSpec((PROJ_HEAD_TM, D_MODEL), lambda j: (0, 0), pipeline_mode=pl.Buffered(1)),
                  pl.BlockSpec((None, PROJ_HEAD_TN, D_MODEL), lambda j: (layer, j, 0)),
                  pl.BlockSpec((None, GATE_RANK, D_MODEL),
                               lambda j: (layer, D_PROJ_MAIN // GATE_RANK, 0))],
        out_specs=[pl.BlockSpec((PROJ_HEAD_TM, PROJ_HEAD_TN), lambda j: (0, j)),
                   pl.BlockSpec((PROJ_HEAD_TM, GATE_RANK), lambda j: (0, 0)),
                   pl.BlockSpec((PROJ_HEAD_TN, D_MODEL), lambda j: (j, 0)),
                   pl.BlockSpec((GATE_RANK, D_MODEL), lambda j: (0, 0))],
        out_shape=[proj_shape, z_shape,
                   jax.ShapeDtypeStruct((D_PROJ_MAIN, D_MODEL), BF16),
                   jax.ShapeDtypeStruct((GATE_RANK, D_MODEL), BF16)],
        scratch_shapes=[pltpu.VMEM((PROJ_HEAD_TM, D_MODEL), BF16)],
        compiler_params=pltpu.CompilerParams(
            dimension_semantics=("arbitrary",), vmem_limit_bytes=VMEM_LIMIT),
        name="in_proj_head",
    )(x2, w_in_t, w_in_t)

    n_j = D_PROJ_MAIN // PROJ_TN
    slab_rows = D_MODEL // CAST_SLABS
    slab = lambda i, j: jnp.minimum(i * n_j + j, CAST_SLABS - 1)
    return pl.pallas_call(
        _in_proj_rest_kernel,
        grid=(n_i, n_j),
        in_specs=[pl.BlockSpec((PROJ_TM, D_MODEL), lambda i, j: (i + i0, 0)),
                  pl.BlockSpec((PROJ_TN, D_MODEL), lambda i, j: (j, 0)),
                  pl.BlockSpec((GATE_RANK, D_MODEL), lambda i, j: (0, 0)),
                  pl.BlockSpec(memory_space=pl.ANY),
                  pl.BlockSpec(memory_space=pl.ANY),
                  pl.BlockSpec((None, slab_rows, D_MODEL), lambda i, j: (layer, slab(i, j), 0)),
                  pl.BlockSpec((None, slab_rows, D_FF), lambda i, j: (layer, slab(i, j), 0))],
        out_specs=[pl.BlockSpec((PROJ_TM, PROJ_TN), lambda i, j: (i + i0, j)),
                   pl.BlockSpec((PROJ_TM, GATE_RANK), lambda i, j: (i + i0, 0)),
                   pl.BlockSpec((slab_rows, D_MODEL), lambda i, j: (slab(i, j), 0)),
                   pl.BlockSpec((slab_rows, D_FF), lambda i, j: (slab(i, j), 0))],
        out_shape=[proj_shape, z_shape,
                   jax.ShapeDtypeStruct((D_MODEL, D_MODEL), BF16),
                   jax.ShapeDtypeStruct((D_MODEL, D_FF), BF16)],
        input_output_aliases={3: 0, 4: 1},
        scratch_shapes=[pltpu.VMEM((PROJ_TM, D_MODEL), BF16)],
        compiler_params=pltpu.CompilerParams(
            dimension_semantics=("arbitrary", "arbitrary"), vmem_limit_bytes=VMEM_LIMIT),
        name="in_proj_rest",
    )(x2, w_in_b, wz_b, proj, z_low, w_out, w_ff_up)


def _shift_rows(h3, carry3, shift):
    pos = lax.broadcasted_iota(jnp.int32, h3.shape, 1)
    rolled = pltpu.roll(h3, shift, axis=1)
    rolled_prev = jnp.concatenate([pltpu.roll(carry3, shift, axis=1), rolled[:-1]], axis=0)
    return jnp.where(pos < shift, rolled_prev, rolled)


def _conv_piece(grp, proj_ref, w_ref, g_ref, carry_ref, y_ref):
    ts = proj_ref.shape[0]
    lo = grp * CONV_GROUP_WIDTH
    sl = slice(lo, lo + CONV_GROUP_WIDTH)
    h = (proj_ref[:, COL_C + lo:COL_C + lo + CONV_GROUP_WIDTH].astype(F32)
         * proj_ref[:, COL_U + lo:COL_U + lo + CONV_GROUP_WIDTH].astype(F32))
    h3 = h.reshape(ts // SUBLANES, SUBLANES, CONV_GROUP_WIDTH)
    carry3 = carry_ref[:, sl].reshape(1, SUBLANES, CONV_GROUP_WIDTH)
    h1 = _shift_rows(h3, carry3, 1).reshape(h.shape)
    h2 = _shift_rows(h3, carry3, 2).reshape(h.shape)
    carry_ref[:, sl] = h[ts - SUBLANES:, :]
    w = w_ref[:, sl]
    y = (proj_ref[:, COL_B + lo:COL_B + lo + CONV_GROUP_WIDTH].astype(F32)
         * (w[2:3, :] * h + w[1:2, :] * h1 + w[0:1, :] * h2))
    ms = jnp.mean(y * y, axis=-1, keepdims=True)
    y_ref[:, sl] = (y * lax.rsqrt(ms + RMS_EPS) * g_ref[:, sl]).astype(y_ref.dtype)


def _chunk_cumsum(x):
    t_blk, width = x.shape
    x3 = x.reshape(t_blk // SUBLANES, SUBLANES, width)
    pos = lax.broadcasted_iota(jnp.int32, x3.shape, 1)
    shift = 1
    while shift < SUBLANES:
        x3 = x3 + jnp.where(pos >= shift, pltpu.roll(x3, shift, axis=1), 0.0)
        shift *= 2
    vregs_per_chunk = CHUNK // SUBLANES
    x4 = x3.reshape(t_blk // CHUNK, vregs_per_chunk, SUBLANES, width)
    outs = [x4[:, 0]]
    for j in range(1, vregs_per_chunk):
        outs.append(x4[:, j] + outs[-1][:, SUBLANES - 1:SUBLANES, :])
    return jnp.stack(outs, axis=1).reshape(t_blk, width)


def _gla_prepare(head, proj_ref, z_ref, gb_ref, qd_ref, ki_ref, ke_ref, dct_ref):
    t_blk = proj_ref.shape[0]
    n_chunks = t_blk // CHUNK
    col_q, col_k = COL_Q + head * HEAD_K, COL_K + head * HEAD_K
    z = z_ref[:, head * HEAD_K:(head + 1) * HEAD_K] + gb_ref[:, head * HEAD_K:(head + 1) * HEAD_K]
    log_sig = jnp.minimum(z, 0.0) - jnp.log(1.0 + jnp.exp(-jnp.abs(z)))
    bcum = _chunk_cumsum(log_sig * (1.0 / GATE_TAU))
    qd_ref[head] = ((proj_ref[:, col_q:col_q + HEAD_K].astype(F32) * (HEAD_K ** -0.5))
                    * jnp.exp(bcum)).astype(BF16)
    k_inv_f = proj_ref[:, col_k:col_k + HEAD_K].astype(F32) * jnp.exp(-bcum)
    ki_ref[head] = k_inv_f.astype(BF16)
    decay = jnp.exp(bcum.reshape(n_chunks, CHUNK, HEAD_K)[:, CHUNK - 1:CHUNK, :])
    ke_ref[head] = (k_inv_f.reshape(n_chunks, CHUNK, HEAD_K) * decay).astype(BF16).reshape(
        t_blk, HEAD_K)
    pad = jnp.zeros((HEAD_K - n_chunks, HEAD_K), F32)
    dct_ref[head] = jnp.transpose(jnp.concatenate([decay.reshape(n_chunks, HEAD_K), pad], axis=0))


def _gla_scores(head, c, qd_ref, ki_ref, sc_ref):
    rows = slice(c * CHUNK, (c + 1) * CHUNK)
    causal = (lax.broadcasted_iota(jnp.int32, (CHUNK, CHUNK), 0)
              >= lax.broadcasted_iota(jnp.int32, (CHUNK, CHUNK), 1))
    scores = lax.dot_general(qd_ref[head, rows, :], ki_ref[head, rows, :], _NT_DIMS,
                             preferred_element_type=F32)
    sc_ref[head, rows, :] = jnp.where(causal, scores, 0.0).astype(sc_ref.dtype)


def _gla_state_step(head, c, proj_ref, ke_ref, dct_ref, state_ref, st_ref):
    rows = slice(c * CHUNK, (c + 1) * CHUNK)
    col_v = COL_V + head * HEAD_V
    state = state_ref[head]
    st_ref[head, c] = state.astype(st_ref.dtype)
    delta = lax.dot_general(ke_ref[head, rows, :], proj_ref[rows, col_v:col_v + HEAD_V], _TN_DIMS,
                            preferred_element_type=F32)
    state_ref[head] = dct_ref[head, :, c:c + 1] * state + delta


def _gla_output(head, c, proj_ref, g_ref, qd_ref, sc_ref, st_ref, y_ref):
    rows = slice(c * CHUNK, (c + 1) * CHUNK)
    col_v, col_r = COL_V + head * HEAD_V, COL_R + head * HEAD_V
    col_y = D_CONV + head * HEAD_V
    o = (jnp.dot(sc_ref[head, rows, :], proj_ref[rows, col_v:col_v + HEAD_V],
                 preferred_element_type=F32)
         + jnp.dot(qd_ref[head, rows, :], st_ref[head, c], preferred_element_type=F32))
    ms = jnp.mean(o * o, axis=-1, keepdims=True)
    o_n = o * lax.rsqrt(ms + RMS_EPS) * g_ref[:, head * HEAD_V:(head + 1) * HEAD_V]
    r_c = proj_ref[rows, col_r:col_r + HEAD_V].astype(F32)
    y_ref[rows, col_y:col_y + HEAD_V] = (o_n * (r_c * jax.nn.sigmoid(r_c))).astype(y_ref.dtype)


def _mixer_kernel(proj_ref, zl_ref, x_ref, wo_ref, cw_ref, cg_ref, wg_ref, gb_ref, gg_ref,
                  ln_g_ref, ln_b_ref, wd_ref, o_ref, wd_b_ref, y_ref, yp_ref, mix_ref, state_ref,
                  carry_ref, qd_ref, ki_ref, ke_ref, dct_ref, z_ref, sc_ref, st_ref, *, blocks_per_seq):
    s = pl.program_id(0)

    @pl.when(s == 0)
    def _():
        y_ref[...] = jnp.zeros_like(y_ref)

    @pl.when(s % blocks_per_seq == 0)
    def _():
        state_ref[...] = jnp.zeros_like(state_ref)
        carry_ref[...] = jnp.zeros_like(carry_ref)

    yp_ref[...] = y_ref[...]
    wd_b_ref[...] = wd_ref[...].astype(BF16)

    gla_refs = (qd_ref, ki_ref, ke_ref, dct_ref)
    n_chunks = proj_ref.shape[0] // CHUNK
    vector_pieces = [functools.partial(_conv_piece, grp, proj_ref, cw_ref, cg_ref, carry_ref, y_ref)
                     for grp in range(CONV_GROUPS)]
    vector_pieces += [functools.partial(_gla_prepare, head, proj_ref, z_ref, gb_ref, *gla_refs)
                      for head in range(GLA_HEADS)]
    head_chunks = [(head, c) for c in range(n_chunks) for head in range(GLA_HEADS)]
    chunk_pieces = (
        [functools.partial(_gla_scores, h, c, qd_ref, ki_ref, sc_ref) for h, c in head_chunks]
        + [functools.partial(_gla_state_step, h, c, proj_ref, ke_ref, dct_ref, state_ref, st_ref)
           for h, c in head_chunks]
        + [functools.partial(_gla_output, h, c, proj_ref, gg_ref, qd_ref, sc_ref, st_ref, y_ref)
           for h, c in head_chunks])

    def gate_preactivation():
        z_ref[...] = jnp.dot(zl_ref[...], wg_ref[...], preferred_element_type=F32)

    def out_proj_tile(sub, n):
        rows = slice(sub * ROW_SUB, (sub + 1) * ROW_SUB)
        cols = slice(n * OUT_TN, (n + 1) * OUT_TN)
        mix_ref[rows, cols] = jnp.dot(yp_ref[rows, :], wo_ref[:, cols], preferred_element_type=F32)

    def layer_norm_rows(row0, n_rows):
        rows = slice(row0, row0 + n_rows)
        o_ref[rows, :] = _layer_norm(DN_ALPHA * x_ref[rows, :] + mix_ref[rows, :],
                                     ln_g_ref[...], ln_b_ref[...]).astype(o_ref.dtype)

    tiles_per_sub = D_MODEL // OUT_TN
    n_tiles = (o_ref.shape[0] // ROW_SUB) * tiles_per_sub
    ln_rows = ROW_SUB // 2
    ln_queue = []
    pieces = [gate_preactivation]
    for t in range(n_tiles):
        sub, n = divmod(t, tiles_per_sub)
        lo = len(vector_pieces) * t // n_tiles
        hi = len(vector_pieces) * (t + 1) // n_tiles
        pieces += vector_pieces[lo:hi] + ln_queue[:1] + [functools.partial(out_proj_tile, sub, n)]
        del ln_queue[:1]
        if n == tiles_per_sub - 1:
            ln_queue += [functools.partial(layer_norm_rows, sub * ROW_SUB + r, ln_rows)
                         for r in range(0, ROW_SUB, ln_rows)]
    for piece in pieces + chunk_pieces + ln_queue:
        piece()


def _mixer(proj, z_low, x2, w_out_b, conv_w8, conv_g, wg_b, gate_bias, gla_g, ln_g, ln_b,
           w_ff_down, layer, bsz, seq):
    m = x2.shape[0]
    n_blk = m // MIX_T
    cur_blk = lambda s: (jnp.minimum(s, n_blk - 1), 0)
    prev_blk = lambda s: (jnp.maximum(s - 1, 0), 0)
    const2 = lambda s: (0, 0)
    slab_rows = D_FF // n_blk
    return pl.pallas_call(
        functools.partial(_mixer_kernel, blocks_per_seq=seq // MIX_T),
        grid=(n_blk + 1,),
        in_specs=[pl.BlockSpec((MIX_T, D_PROJ_MAIN), cur_blk),
                  pl.BlockSpec((MIX_T, GATE_RANK), cur_blk),
                  pl.BlockSpec((MIX_T, D_MODEL), prev_blk),
                  pl.BlockSpec((D_CONV + D_GLA_V, D_MODEL), const2, pipeline_mode=pl.Buffered(1)),
                  pl.BlockSpec((8, D_CONV), const2),
                  pl.BlockSpec((1, D_CONV), const2),
                  pl.BlockSpec((GATE_RANK, D_GLA_K), const2),
                  pl.BlockSpec((1, D_GLA_K), const2),
                  pl.BlockSpec((1, D_GLA_V), const2),
                  pl.BlockSpec((1, D_MODEL), const2),
                  pl.BlockSpec((1, D_MODEL), const2),
                  pl.BlockSpec((None, slab_rows, D_MODEL),
                               lambda s: (layer, jnp.minimum(s, n_blk - 1), 0))],
        out_specs=[pl.BlockSpec((MIX_T, D_MODEL), prev_blk),
                   pl.BlockSpec((slab_rows, D_MODEL), cur_blk)],
        out_shape=[jax.ShapeDtypeStruct((m, D_MODEL), BF16),
                   jax.ShapeDtypeStruct((D_FF, D_MODEL), BF16)],
        scratch_shapes=[pltpu.VMEM((MIX_T, D_CONV + D_GLA_V), BF16),
                        pltpu.VMEM((MIX_T, D_CONV + D_GLA_V), BF16),
                        pltpu.VMEM((MIX_T, D_MODEL), F32),
                        pltpu.VMEM((GLA_HEADS, HEAD_K, HEAD_V), F32),
                        pltpu.VMEM((8, D_CONV), F32),
                        pltpu.VMEM((GLA_HEADS, MIX_T, HEAD_K), BF16),
                        pltpu.VMEM((GLA_HEADS, MIX_T, HEAD_K), BF16),
                        pltpu.VMEM((GLA_HEADS, MIX_T, HEAD_K), BF16),
                        pltpu.VMEM((GLA_HEADS, HEAD_K, HEAD_K), F32),
                        pltpu.VMEM((MIX_T, D_GLA_K), F32),
                        pltpu.VMEM((GLA_HEADS, MIX_T, CHUNK), BF16),
                        pltpu.VMEM((GLA_HEADS, MIX_T // CHUNK, HEAD_K, HEAD_V), BF16)],
        compiler_params=pltpu.CompilerParams(
            dimension_semantics=("arbitrary",), vmem_limit_bytes=VMEM_LIMIT),
        name="mixer",
    )(proj, z_low, x2, w_out_b, conv_w8, conv_g, wg_b, gate_bias, gla_g, ln_g, ln_b, w_ff_down)


def _ffn_ln2_kernel(x_ref, wu_ref, wd_ref, g_ref, b_ref, o_ref):
    f = pl.program_id(1)
    last = pl.num_programs(1) - 1

    def hidden():
        h = jnp.dot(x_ref[...], wu_ref[...], preferred_element_type=F32)
        h = jnp.maximum(h, 0.0)
        return (h * h).astype(BF16)

    @pl.when(f == 0)
    def _():
        o_ref[...] = jnp.dot(hidden(), wd_ref[...], preferred_element_type=F32)

    @pl.when(jnp.logical_and(f > 0, f < last))
    def _():
        o_ref[...] += jnp.dot(hidden(), wd_ref[...], preferred_element_type=F32)

    @pl.when(f == last)
    def _():
        h = hidden()
        for s in range(o_ref.shape[0] // ROW_SUB):
            rows = slice(s * ROW_SUB, (s + 1) * ROW_SUB)
            ff = o_ref[rows, :] + jnp.dot(h[rows, :], wd_ref[...], preferred_element_type=F32)
            o_ref[rows, :] = _layer_norm(DN_ALPHA * x_ref[rows, :].astype(F32) + ff,
                                         g_ref[...], b_ref[...])


def _ffn_ln2(x1, w_up_b, w_down_b, ln_g, ln_b):
    m = x1.shape[0]
    return pl.pallas_call(
        _ffn_ln2_kernel,
        grid=(m // FFN_TM, D_FF // FFN_TF),
        in_specs=[pl.BlockSpec((FFN_TM, D_MODEL), lambda i, f: (i, 0)),
                  pl.BlockSpec((D_MODEL, FFN_TF), lambda i, f: (0, f)),
                  pl.BlockSpec((FFN_TF, D_MODEL), lambda i, f: (f, 0)),
                  pl.BlockSpec((1, D_MODEL), lambda i, f: (0, 0)),
                  pl.BlockSpec((1, D_MODEL), lambda i, f: (0, 0))],
        out_specs=pl.BlockSpec((FFN_TM, D_MODEL), lambda i, f: (i, 0)),
        out_shape=jax.ShapeDtypeStruct((m, D_MODEL), F32),
        compiler_params=pltpu.CompilerParams(
            dimension_semantics=("parallel", "arbitrary"), vmem_limit_bytes=VMEM_LIMIT),
        name="ffn_ln2",
    )(x1, w_up_b, w_down_b, ln_g, ln_b)


def kernel(x, w_in, conv_w, conv_norm_g, w_gate_up, gate_bias, gla_norm_g, w_out,
           ln1_g, ln1_b, w_ff_up, w_ff_down, ln2_g, ln2_b):
    bsz, seq, _ = x.shape
    assert seq % MIX_T == 0 and MIX_T % CHUNK == 0 and MIX_T % ROW_SUB == 0
    x2 = x.reshape(bsz * seq, D_MODEL)
    w_in_t = jnp.swapaxes(w_in, 1, 2)
    for l in range(DEPTH):
        wg_b = w_gate_up[l].astype(BF16)
        gb = gate_bias[l].reshape(1, D_GLA_K)
        conv_w8 = jnp.pad(conv_w[l], ((0, 8 - conv_w.shape[1]), (0, 0)))

        proj, z_low, w_out_b, w_up_b = _in_proj(x2, w_in_t, w_out, w_ff_up, l)
        x1, w_down_b = _mixer(proj, z_low, x2, w_out_b, conv_w8, conv_norm_g[l].reshape(1, D_CONV),
                              wg_b, gb, gla_norm_g[l].reshape(1, D_GLA_V),
                              ln1_g[l].reshape(1, D_MODEL), ln1_b[l].reshape(1, D_MODEL),
                              w_ff_down, l, bsz, seq)
        x2 = _ffn_ln2(x1, w_up_b, w_down_b,
                      ln2_g[l].reshape(1, D_MODEL), ln2_b[l].reshape(1, D_MODEL))
    return x2.reshape(bsz, seq, D_MODEL)
```

```python
import functools

import jax
import jax.numpy as jnp
from jax import lax
from jax.experimental import pallas as pl
from jax.experimental.pallas import tpu as pltpu

F32 = jnp.float32
BF16 = jnp.bfloat16

D_MODEL = 2048
D_CONV = 1024
CONV_GROUPS = 8
CONV_GROUP_WIDTH = D_CONV // CONV_GROUPS
GLA_HEADS = 4
HEAD_K = 128
HEAD_V = 256
D_GLA_K = GLA_HEADS * HEAD_K
D_GLA_V = GLA_HEADS * HEAD_V
GATE_RANK = 16
GATE_TAU = 16.0
CHUNK = 64
D_FF = 4 * D_MODEL
LN_EPS = 1e-5
RMS_EPS = 1e-6
DEPTH = 1
DN_ALPHA = (2.0 * DEPTH) ** 0.25

D_PROJ_MAIN = 3 * D_CONV + 2 * D_GLA_K + 2 * D_GLA_V

COL_B, COL_C, COL_U = 0, D_CONV, 2 * D_CONV
COL_Q = 3 * D_CONV
COL_K = COL_Q + D_GLA_K
COL_V = COL_K + D_GLA_K
COL_R = COL_V + D_GLA_V

VMEM_LIMIT = 60 * 1024 * 1024
SUBLANES = 8

PROJ_TM = 1024
PROJ_HEAD_TM = 2048
PROJ_HEAD_TN = 512
PROJ_TN = 1024
CAST_SLABS = 32
MIX_T = 512
ROW_SUB = 256
OUT_TN = 512
FFN_TM = 512
FFN_TF = 2048

_NT_DIMS = (((1,), (1,)), ((), ()))
_TN_DIMS = (((0,), (0,)), ((), ()))


def _layer_norm(y, g, b):
    mu = jnp.mean(y, axis=-1, keepdims=True)
    yc = y - mu
    var = jnp.mean(yc * yc, axis=-1, keepdims=True)
    return yc * lax.rsqrt(var + LN_EPS) * g + b


def _in_proj_head_kernel(x_ref, w_ref, wz_ref, o_ref, z_ref, wb_ref, wzb_ref, xb_ref):
    @pl.when(pl.program_id(0) == 0)
    def _():
        xb_ref[...] = x_ref[...].astype(BF16)
        wzb_ref[...] = wz_ref[...].astype(BF16)
        z_ref[...] = lax.dot_general(xb_ref[...], wzb_ref[...], _NT_DIMS,
                                     preferred_element_type=F32).astype(z_ref.dtype)

    wb_ref[...] = jnp.transpose(w_ref[...]).astype(BF16)
    o_ref[...] = jnp.dot(xb_ref[...], wb_ref[...], preferred_element_type=F32).astype(o_ref.dtype)


def _in_proj_rest_kernel(x_ref, wb_ref, wzb_ref, proj_hbm_ref, z_hbm_ref, wo_ref, wu_ref,
                         o_ref, z_ref, wo_b_ref, wu_b_ref, xb_ref):
    del proj_hbm_ref, z_hbm_ref

    @pl.when(pl.program_id(1) == 0)
    def _():
        xb_ref[...] = x_ref[...].astype(BF16)
        z_ref[...] = lax.dot_general(xb_ref[...], wzb_ref[...], _NT_DIMS,
                                     preferred_element_type=F32).astype(z_ref.dtype)

    wo_b_ref[...] = wo_ref[...].astype(BF16)
    wu_b_ref[...] = wu_ref[...].astype(BF16)
    o_ref[...] = jnp.dot(xb_ref[...], wb_ref[...], preferred_element_type=F32).astype(o_ref.dtype)


def _in_proj(x2, w_in_t, w_out, w_ff_up, layer):
    m = x2.shape[0]
    i0 = PROJ_HEAD_TM // PROJ_TM
    n_i = m // PROJ_TM - i0
    proj_shape = jax.ShapeDtypeStruct((m, D_PROJ_MAIN), BF16)
    z_shape = jax.ShapeDtypeStruct((m, GATE_RANK), BF16)

    n_jh = D_PROJ_MAIN // PROJ_HEAD_TN
    proj, z_low, w_in_b, wz_b = pl.pallas_call(
        _in_proj_head_kernel,
        grid=(n_jh,),
        in_specs=[pl.BlockSpec((PROJ_HEAD_TM, D_MODEL), lambda j: (0, 0), pipeline_mode=pl.Buffered(1)),
                  pl.BlockSpec((None, PROJ_HEAD_TN, D_MODEL), lambda j: (layer, j, 0)),
                  pl.BlockSpec((None, GATE_RANK, D_MODEL),
                               lambda j: (layer, D_PROJ_MAIN // GATE_RANK, 0))],
        out_specs=[pl.BlockSpec((PROJ_HEAD_TM, PROJ_HEAD_TN), lambda j: (0, j)),
                   pl.BlockSpec((PROJ_HEAD_TM, GATE_RANK), lambda j: (0, 0)),
                   pl.BlockSpec((D_MODEL, PROJ_HEAD_TN), lambda j: (0, j)),
                   pl.BlockSpec((GATE_RANK, D_MODEL), lambda j: (0, 0))],
        out_shape=[proj_shape, z_shape,
                   jax.ShapeDtypeStruct((D_MODEL, D_PROJ_MAIN), BF16),
                   jax.ShapeDtypeStruct((GATE_RANK, D_MODEL), BF16)],
        scratch_shapes=[pltpu.VMEM((PROJ_HEAD_TM, D_MODEL), BF16)],
        compiler_params=pltpu.CompilerParams(
            dimension_semantics=("arbitrary",), vmem_limit_bytes=VMEM_LIMIT),
        name="in_proj_head",
    )(x2, w_in_t, w_in_t)

    n_j = D_PROJ_MAIN // PROJ_TN
    slab_rows = D_MODEL // CAST_SLABS
    slab = lambda i, j: jnp.minimum(i * n_j + j, CAST_SLABS - 1)
    return pl.pallas_call(
        _in_proj_rest_kernel,
        grid=(n_i, n_j),
        in_specs=[pl.BlockSpec((PROJ_TM, D_MODEL), lambda i, j: (i + i0, 0)),
                  pl.BlockSpec((D_MODEL, PROJ_TN), lambda i, j: (0, j)),
                  pl.BlockSpec((GATE_RANK, D_MODEL), lambda i, j: (0, 0)),
                  pl.BlockSpec(memory_space=pl.ANY),
                  pl.BlockSpec(memory_space=pl.ANY),
                  pl.BlockSpec((None, slab_rows, D_MODEL), lambda i, j: (layer, slab(i, j), 0)),
                  pl.BlockSpec((None, slab_rows, D_FF), lambda i, j: (layer, slab(i, j), 0))],
        out_specs=[pl.BlockSpec((PROJ_TM, PROJ_TN), lambda i, j: (i + i0, j)),
                   pl.BlockSpec((PROJ_TM, GATE_RANK), lambda i, j: (i + i0, 0)),
                   pl.BlockSpec((slab_rows, D_MODEL), lambda i, j: (slab(i, j), 0)),
                   pl.BlockSpec((slab_rows, D_FF), lambda i, j: (slab(i, j), 0))],
        out_shape=[proj_shape, z_shape,
                   jax.ShapeDtypeStruct((D_MODEL, D_MODEL), BF16),
                   jax.ShapeDtypeStruct((D_MODEL, D_FF), BF16)],
        input_output_aliases={3: 0, 4: 1},
        scratch_shapes=[pltpu.VMEM((PROJ_TM, D_MODEL), BF16)],
        compiler_params=pltpu.CompilerParams(
            dimension_semantics=("arbitrary", "arbitrary"), vmem_limit_bytes=VMEM_LIMIT),
        name="in_proj_rest",
    )(x2, w_in_b, wz_b, proj, z_low, w_out, w_ff_up)


def _shift_rows(h3, carry3, shift):
    pos = lax.broadcasted_iota(jnp.int32, h3.shape, 1)
    rolled = pltpu.roll(h3, shift, axis=1)
    rolled_prev = jnp.concatenate([pltpu.roll(carry3, shift, axis=1), rolled[:-1]], axis=0)
    return jnp.where(pos < shift, rolled_prev, rolled)


def _conv_piece(grp, proj_ref, w_ref, g_ref, carry_ref, y_ref):
    ts = proj_ref.shape[0]
    lo = grp * CONV_GROUP_WIDTH
    sl = slice(lo, lo + CONV_GROUP_WIDTH)
    h = (proj_ref[:, COL_C + lo:COL_C + lo + CONV_GROUP_WIDTH].astype(F32)
         * proj_ref[:, COL_U + lo:COL_U + lo + CONV_GROUP_WIDTH].astype(F32))
    h3 = h.reshape(ts // SUBLANES, SUBLANES, CONV_GROUP_WIDTH)
    carry3 = carry_ref[:, sl].reshape(1, SUBLANES, CONV_GROUP_WIDTH)
    h1 = _shift_rows(h3, carry3, 1).reshape(h.shape)
    h2 = _shift_rows(h3, carry3, 2).reshape(h.shape)
    carry_ref[:, sl] = h[ts - SUBLANES:, :]
    w = w_ref[:, sl]
    y = (proj_ref[:, COL_B + lo:COL_B + lo + CONV_GROUP_WIDTH].astype(F32)
         * (w[2:3, :] * h + w[1:2, :] * h1 + w[0:1, :] * h2))
    ms = jnp.mean(y * y, axis=-1, keepdims=True)
    y_ref[:, sl] = (y * lax.rsqrt(ms + RMS_EPS) * g_ref[:, sl]).astype(y_ref.dtype)


def _chunk_cumsum(x):
    t_blk, width = x.shape
    x3 = x.reshape(t_blk // SUBLANES, SUBLANES, width)
    pos = lax.broadcasted_iota(jnp.int32, x3.shape, 1)
    shift = 1
    while shift < SUBLANES:
        x3 = x3 + jnp.where(pos >= shift, pltpu.roll(x3, shift, axis=1), 0.0)
        shift *= 2
    vregs_per_chunk = CHUNK // SUBLANES
    x4 = x3.reshape(t_blk // CHUNK, vregs_per_chunk, SUBLANES, width)
    outs = [x4[:, 0]]
    for j in range(1, vregs_per_chunk):
        outs.append(x4[:, j] + outs[-1][:, SUBLANES - 1:SUBLANES, :])
    return jnp.stack(outs, axis=1).reshape(t_blk, width)


def _gla_prepare(head, proj_ref, z_ref, gb_ref, qd_ref, ki_ref, ke_ref, dct_ref):
    t_blk = proj_ref.shape[0]
    n_chunks = t_blk // CHUNK
    col_q, col_k = COL_Q + head * HEAD_K, COL_K + head * HEAD_K
    z = z_ref[:, head * HEAD_K:(head + 1) * HEAD_K] + gb_ref[:, head * HEAD_K:(head + 1) * HEAD_K]
    log_sig = jnp.minimum(z, 0.0) - jnp.log(1.0 + jnp.exp(-jnp.abs(z)))
    bcum = _chunk_cumsum(log_sig * (1.0 / GATE_TAU))
    qd_ref[head] = ((proj_ref[:, col_q:col_q + HEAD_K].astype(F32) * (HEAD_K ** -0.5))
                    * jnp.exp(bcum)).astype(BF16)
    k_inv_f = proj_ref[:, col_k:col_k + HEAD_K].astype(F32) * jnp.exp(-bcum)
    ki_ref[head] = k_inv_f.astype(BF16)
    decay = jnp.exp(bcum.reshape(n_chunks, CHUNK, HEAD_K)[:, CHUNK - 1:CHUNK, :])
    ke_ref[head] = (k_inv_f.reshape(n_chunks, CHUNK, HEAD_K) * decay).astype(BF16).reshape(
        t_blk, HEAD_K)
    pad = jnp.zeros((HEAD_K - n_chunks, HEAD_K), F32)
    dct_ref[head] = jnp.transpose(jnp.concatenate([decay.reshape(n_chunks, HEAD_K), pad], axis=0))


def _gla_scores(head, c, qd_ref, ki_ref, sc_ref):
    rows = slice(c * CHUNK, (c + 1) * CHUNK)
    causal = (lax.broadcasted_iota(jnp.int32, (CHUNK, CHUNK), 0)
              >= lax.broadcasted_iota(jnp.int32, (CHUNK, CHUNK), 1))
    scores = lax.dot_general(qd_ref[head, rows, :], ki_ref[head, rows, :], _NT_DIMS,
                             preferred_element_type=F32)
    sc_ref[head, rows, :] = jnp.where(causal, scores, 0.0).astype(sc_ref.dtype)


def _gla_state_step(head, c, proj_ref, ke_ref, dct_ref, state_ref, st_ref):
    rows = slice(c * CHUNK, (c + 1) * CHUNK)
    col_v = COL_V + head * HEAD_V
    state = state_ref[head]
    st_ref[head, c] = state.astype(st_ref.dtype)
    delta = lax.dot_general(ke_ref[head, rows, :], proj_ref[rows, col_v:col_v + HEAD_V], _TN_DIMS,
                            preferred_element_type=F32)
    state_ref[head] = dct_ref[head, :, c:c + 1] * state + delta


def _gla_output(head, c, proj_ref, g_ref, qd_ref, sc_ref, st_ref, y_ref):
    rows = slice(c * CHUNK, (c + 1) * CHUNK)
    col_v, col_r = COL_V + head * HEAD_V, COL_R + head * HEAD_V
    col_y = D_CONV + head * HEAD_V
    o = (jnp.dot(sc_ref[head, rows, :], proj_ref[rows, col_v:col_v + HEAD_V],
                 preferred_element_type=F32)
         + jnp.dot(qd_ref[head, rows, :], st_ref[head, c], preferred_element_type=F32))
    ms = jnp.mean(o * o, axis=-1, keepdims=True)
    o_n = o * lax.rsqrt(ms + RMS_EPS) * g_ref[:, head * HEAD_V:(head + 1) * HEAD_V]
    r_c = proj_ref[rows, col_r:col_r + HEAD_V].astype(F32)
    y_ref[rows, col_y:col_y + HEAD_V] = (o_n * (r_c * jax.nn.sigmoid(r_c))).astype(y_ref.dtype)


def _mixer_kernel(proj_ref, zl_ref, x_ref, wo_ref, cw_ref, cg_ref, wg_ref, gb_ref, gg_ref,
                  ln_g_ref, ln_b_ref, wd_ref, o_ref, wd_b_ref, y_ref, yp_ref, mix_ref, state_ref,
                  carry_ref, qd_ref, ki_ref, ke_ref, dct_ref, z_ref, sc_ref, st_ref, *, blocks_per_seq):
    s = pl.program_id(0)

    @pl.when(s == 0)
    def _():
        y_ref[...] = jnp.zeros_like(y_ref)

    @pl.when(s % blocks_per_seq == 0)
    def _():
        state_ref[...] = jnp.zeros_like(state_ref)
        carry_ref[...] = jnp.zeros_like(carry_ref)

    yp_ref[...] = y_ref[...]
    wd_b_ref[...] = wd_ref[...].astype(BF16)

    gla_refs = (qd_ref, ki_ref, ke_ref, dct_ref)
    n_chunks = proj_ref.shape[0] // CHUNK
    vector_pieces = [functools.partial(_conv_piece, grp, proj_ref, cw_ref, cg_ref, carry_ref, y_ref)
                     for grp in range(CONV_GROUPS)]
    vector_pieces += [functools.partial(_gla_prepare, head, proj_ref, z_ref, gb_ref, *gla_refs)
                      for head in range(GLA_HEADS)]
    head_chunks = [(head, c) for c in range(n_chunks) for head in range(GLA_HEADS)]
    chunk_pieces = (
        [functools.partial(_gla_scores, h, c, qd_ref, ki_ref, sc_ref) for h, c in head_chunks]
        + [functools.partial(_gla_state_step, h, c, proj_ref, ke_ref, dct_ref, state_ref, st_ref)
           for h, c in head_chunks]
        + [functools.partial(_gla_output, h, c, proj_ref, gg_ref, qd_ref, sc_ref, st_ref, y_ref)
           for h, c in head_chunks])

    def gate_preactivation():
        z_ref[...] = jnp.dot(zl_ref[...], wg_ref[...], preferred_element_type=F32)

    def out_proj_tile(sub, n):
        rows = slice(sub * ROW_SUB, (sub + 1) * ROW_SUB)
        cols = slice(n * OUT_TN, (n + 1) * OUT_TN)
        mix_ref[rows, cols] = jnp.dot(yp_ref[rows, :], wo_ref[:, cols], preferred_element_type=F32)

    def layer_norm_rows(row0, n_rows):
        rows = slice(row0, row0 + n_rows)
        o_ref[rows, :] = _layer_norm(DN_ALPHA * x_ref[rows, :] + mix_ref[rows, :],
                                     ln_g_ref[...], ln_b_ref[...]).astype(o_ref.dtype)

    tiles_per_sub = D_MODEL // OUT_TN
    n_tiles = (o_ref.shape[0] // ROW_SUB) * tiles_per_sub
    ln_rows = ROW_SUB // 2
    ln_queue = []
    pieces = [gate_preactivation]
    for t in range(n_tiles):
        sub, n = divmod(t, tiles_per_sub)
        lo = len(vector_pieces) * t // n_tiles
        hi = len(vector_pieces) * (t + 1) // n_tiles
        pieces += vector_pieces[lo:hi] + ln_queue[:1] + [functools.partial(out_proj_tile, sub, n)]
        del ln_queue[:1]
        if n == tiles_per_sub - 1:
            ln_queue += [functools.partial(layer_norm_rows, sub * ROW_SUB + r, ln_rows)
                         for r in range(0, ROW_SUB, ln_rows)]
    for piece in pieces + chunk_pieces + ln_queue:
        piece()


def _mixer(proj, z_low, x2, w_out_b, conv_w8, conv_g, wg_b, gate_bias, gla_g, ln_g, ln_b,
           w_ff_down, layer, bsz, seq):
    m = x2.shape[0]
    n_blk = m // MIX_T
    cur_blk = lambda s: (jnp.minimum(s, n_blk - 1), 0)
    prev_blk = lambda s: (jnp.maximum(s - 1, 0), 0)
    const2 = lambda s: (0, 0)
    slab_rows = D_FF // n_blk
    return pl.pallas_call(
        functools.partial(_mixer_kernel, blocks_per_seq=seq // MIX_T),
        grid=(n_blk + 1,),
        in_specs=[pl.BlockSpec((MIX_T, D_PROJ_MAIN), cur_blk),
                  pl.BlockSpec((MIX_T, GATE_RANK), cur_blk),
                  pl.BlockSpec((MIX_T, D_MODEL), prev_blk),
                  pl.BlockSpec((D_CONV + D_GLA_V, D_MODEL), const2, pipeline_mode=pl.Buffered(1)),
                  pl.BlockSpec((8, D_CONV), const2),
                  pl.BlockSpec((1, D_CONV), const2),
                  pl.BlockSpec((GATE_RANK, D_GLA_K), const2),
                  pl.BlockSpec((1, D_GLA_K), const2),
                  pl.BlockSpec((1, D_GLA_V), const2),
                  pl.BlockSpec((1, D_MODEL), const2),
                  pl.BlockSpec((1, D_MODEL), const2),
                  pl.BlockSpec((None, slab_rows, D_MODEL),
                               lambda s: (layer, jnp.minimum(s, n_blk - 1), 0))],
        out_specs=[pl.BlockSpec((MIX_T, D_MODEL), prev_blk),
                   pl.BlockSpec((slab_rows, D_MODEL), cur_blk)],
        out_shape=[jax.ShapeDtypeStruct((m, D_MODEL), BF16),
                   jax.ShapeDtypeStruct((D_FF, D_MODEL), BF16)],
        scratch_shapes=[pltpu.VMEM((MIX_T, D_CONV + D_GLA_V), BF16),
                        pltpu.VMEM((MIX_T, D_CONV + D_GLA_V), BF16),
                        pltpu.VMEM((MIX_T, D_MODEL), F32),
                        pltpu.VMEM((GLA_HEADS, HEAD_K, HEAD_V), F32),
                        pltpu.VMEM((8, D_CONV), F32),
                        pltpu.VMEM((GLA_HEADS, MIX_T, HEAD_K), BF16),
                        pltpu.VMEM((GLA_HEADS, MIX_T, HEAD_K), BF16),
                        pltpu.VMEM((GLA_HEADS, MIX_T, HEAD_K), BF16),
                        pltpu.VMEM((GLA_HEADS, HEAD_K, HEAD_K), F32),
                        pltpu.VMEM((MIX_T, D_GLA_K), F32),
                        pltpu.VMEM((GLA_HEADS, MIX_T, CHUNK), BF16),
                        pltpu.VMEM((GLA_HEADS, MIX_T // CHUNK, HEAD_K, HEAD_V), BF16)],
        compiler_params=pltpu.CompilerParams(
            dimension_semantics=("arbitrary",), vmem_limit_bytes=VMEM_LIMIT),
        name="mixer",
    )(proj, z_low, x2, w_out_b, conv_w8, conv_g, wg_b, gate_bias, gla_g, ln_g, ln_b, w_ff_down)


def _ffn_ln2_kernel(x_ref, wu_ref, wd_ref, g_ref, b_ref, o_ref):
    f = pl.program_id(1)
    last = pl.num_programs(1) - 1

    def hidden():
        h = jnp.dot(x_ref[...], wu_ref[...], preferred_element_type=F32)
        h = jnp.maximum(h, 0.0)
        return (h * h).astype(BF16)

    @pl.when(f == 0)
    def _():
        o_ref[...] = jnp.dot(hidden(), wd_ref[...], preferred_element_type=F32)

    @pl.when(jnp.logical_and(f > 0, f < last))
    def _():
        o_ref[...] += jnp.dot(hidden(), wd_ref[...], preferred_element_type=F32)

    @pl.when(f == last)
    def _():
        h = hidden()
        for s in range(o_ref.shape[0] // ROW_SUB):
            rows = slice(s * ROW_SUB, (s + 1) * ROW_SUB)
            ff = o_ref[rows, :] + jnp.dot(h[rows, :], wd_ref[...], preferred_element_type=F32)
            o_ref[rows, :] = _layer_norm(DN_ALPHA * x_ref[rows, :].astype(F32) + ff,
                                         g_ref[...], b_ref[...])


def _ffn_ln2(x1, w_up_b, w_down_b, ln_g, ln_b):
    m = x1.shape[0]
    return pl.pallas_call(
        _ffn_ln2_kernel,
        grid=(m // FFN_TM, D_FF // FFN_TF),
        in_specs=[pl.BlockSpec((FFN_TM, D_MODEL), lambda i, f: (i, 0)),
                  pl.BlockSpec((D_MODEL, FFN_TF), lambda i, f: (0, f)),
                  pl.BlockSpec((FFN_TF, D_MODEL), lambda i, f: (f, 0)),
                  pl.BlockSpec((1, D_MODEL), lambda i, f: (0, 0)),
                  pl.BlockSpec((1, D_MODEL), lambda i, f: (0, 0))],
        out_specs=pl.BlockSpec((FFN_TM, D_MODEL), lambda i, f: (i, 0)),
        out_shape=jax.ShapeDtypeStruct((m, D_MODEL), F32),
        compiler_params=pltpu.CompilerParams(
            dimension_semantics=("parallel", "arbitrary"), vmem_limit_bytes=VMEM_LIMIT),
        name="ffn_ln2",
    )(x1, w_up_b, w_down_b, ln_g, ln_b)


def kernel(x, w_in, conv_w, conv_norm_g, w_gate_up, gate_bias, gla_norm_g, w_out,
           ln1_g, ln1_b, w_ff_up, w_ff_down, ln2_g, ln2_b):
    bsz, seq, _ = x.shape
    assert seq % MIX_T == 0 and MIX_T % CHUNK == 0 and MIX_T % ROW_SUB == 0
    x2 = x.reshape(bsz * seq, D_MODEL)
    w_in_t = jnp.swapaxes(w_in, 1, 2)
    for l in range(DEPTH):
        wg_b = w_gate_up[l].astype(BF16)
        gb = gate_bias[l].reshape(1, D_GLA_K)
        conv_w8 = jnp.pad(conv_w[l], ((0, 8 - conv_w.shape[1]), (0, 0)))

        proj, z_low, w_out_b, w_up_b = _in_proj(x2, w_in_t, w_out, w_ff_up, l)
        x1, w_down_b = _mixer(proj, z_low, x2, w_out_b, conv_w8, conv_norm_g[l].reshape(1, D_CONV),
                              wg_b, gb, gla_norm_g[l].reshape(1, D_GLA_V),
                              ln1_g[l].reshape(1, D_MODEL), ln1_b[l].reshape(1, D_MODEL),
                              w_ff_down, l, bsz, seq)
        x2 = _ffn_ln2(x1, w_up_b, w_down_b,
                      ln2_g[l].reshape(1, D_MODEL), ln2_b[l].reshape(1, D_MODEL))
    return x2.reshape(bsz, seq, D_MODEL)
```

```python
import functools

import jax
import jax.numpy as jnp
from jax import lax
from jax.experimental import pallas as pl
from jax.experimental.pallas import tpu as pltpu

F32 = jnp.float32
BF16 = jnp.bfloat16

D_MODEL = 2048
D_CONV = 1024
CONV_GROUPS = 8
CONV_GROUP_WIDTH = D_CONV // CONV_GROUPS
GLA_HEADS = 4
HEAD_K = 128
HEAD_V = 256
D_GLA_K = GLA_HEADS * HEAD_K
D_GLA_V = GLA_HEADS * HEAD_V
GATE_RANK = 16
GATE_TAU = 16.0
CHUNK = 64
D_FF = 4 * D_MODEL
LN_EPS = 1e-5
RMS_EPS = 1e-6
DEPTH = 1
DN_ALPHA = (2.0 * DEPTH) ** 0.25

D_PROJ_MAIN = 3 * D_CONV + 2 * D_GLA_K + 2 * D_GLA_V

COL_B, COL_C, COL_U = 0, D_CONV, 2 * D_CONV
COL_Q = 3 * D_CONV
COL_K = COL_Q + D_GLA_K
COL_V = COL_K + D_GLA_K
COL_R = COL_V + D_GLA_V

VMEM_LIMIT = 60 * 1024 * 1024
SUBLANES = 8

PROJ_TM = 1024
PROJ_HEAD_TM = 2048
PROJ_HEAD_TN = 512
PROJ_TN = 1024
CAST_SLABS = 32
MIX_T = 512
ROW_SUB = 256
OUT_TN = 512
FFN_HEAD_TM = 1024
FFN_HEAD_TF = 512
FFN_TM = 512
FFN_TF = 2048

_NT_DIMS = (((1,), (1,)), ((), ()))
_TN_DIMS = (((0,), (0,)), ((), ()))


def _layer_norm(y, g, b):
    mu = jnp.mean(y, axis=-1, keepdims=True)
    yc = y - mu
    var = jnp.mean(yc * yc, axis=-1, keepdims=True)
    return yc * lax.rsqrt(var + LN_EPS) * g + b


def _in_proj_head_kernel(x_ref, w_ref, wz_ref, o_ref, z_ref, wb_ref, wzb_ref, xb_ref):
    @pl.when(pl.program_id(0) == 0)
    def _():
        xb_ref[...] = x_ref[...].astype(BF16)
        wzb_ref[...] = wz_ref[...].astype(BF16)
        z_ref[...] = lax.dot_general(xb_ref[...], wzb_ref[...], _NT_DIMS,
                                     preferred_element_type=F32).astype(z_ref.dtype)

    wb_ref[...] = w_ref[...].astype(BF16)
    o_ref[...] = lax.dot_general(xb_ref[...], wb_ref[...], _NT_DIMS,
                                 preferred_element_type=F32).astype(o_ref.dtype)


def _in_proj_rest_kernel(x_ref, wb_ref, wzb_ref, proj_hbm_ref, z_hbm_ref, wo_ref,
                         o_ref, z_ref, wo_b_ref, xb_ref):
    del proj_hbm_ref, z_hbm_ref

    @pl.when(pl.program_id(1) == 0)
    def _():
        xb_ref[...] = x_ref[...].astype(BF16)
        z_ref[...] = lax.dot_general(xb_ref[...], wzb_ref[...], _NT_DIMS,
                                     preferred_element_type=F32).astype(z_ref.dtype)

    wo_b_ref[...] = wo_ref[...].astype(BF16)
    o_ref[...] = lax.dot_general(xb_ref[...], wb_ref[...], _NT_DIMS,
                                 preferred_element_type=F32).astype(o_ref.dtype)


def _in_proj(x2, w_in_t, w_out, layer):
    m = x2.shape[0]
    i0 = PROJ_HEAD_TM // PROJ_TM
    n_i = m // PROJ_TM - i0
    proj_shape = jax.ShapeDtypeStruct((m, D_PROJ_MAIN), BF16)
    z_shape = jax.ShapeDtypeStruct((m, GATE_RANK), BF16)

    n_jh = D_PROJ_MAIN // PROJ_HEAD_TN
    proj, z_low, w_in_b, wz_b = pl.pallas_call(
        _in_proj_head_kernel,
        grid=(n_jh,),
        in_specs=[pl.BlockSpec((PROJ_HEAD_TM, D_MODEL), lambda j: (0, 0), pipeline_mode=pl.Buffered(1)),
                  pl.BlockSpec((None, PROJ_HEAD_TN, D_MODEL), lambda j: (layer, j, 0)),
                  pl.BlockSpec((None, GATE_RANK, D_MODEL),
                               lambda j: (layer, D_PROJ_MAIN // GATE_RANK, 0))],
        out_specs=[pl.BlockSpec((PROJ_HEAD_TM, PROJ_HEAD_TN), lambda j: (0, j)),
                   pl.BlockSpec((PROJ_HEAD_TM, GATE_RANK), lambda j: (0, 0)),
                   pl.BlockSpec((PROJ_HEAD_TN, D_MODEL), lambda j: (j, 0)),
                   pl.BlockSpec((GATE_RANK, D_MODEL), lambda j: (0, 0))],
        out_shape=[proj_shape, z_shape,
                   jax.ShapeDtypeStruct((D_PROJ_MAIN, D_MODEL), BF16),
                   jax.ShapeDtypeStruct((GATE_RANK, D_MODEL), BF16)],
        scratch_shapes=[pltpu.VMEM((PROJ_HEAD_TM, D_MODEL), BF16)],
        compiler_params=pltpu.CompilerParams(
            dimension_semantics=("arbitrary",), vmem_limit_bytes=VMEM_LIMIT),
        name="in_proj_head",
    )(x2, w_in_t, w_in_t)

    n_j = D_PROJ_MAIN // PROJ_TN
    slab_rows = D_MODEL // CAST_SLABS
    slab = lambda i, j: jnp.minimum(i * n_j + j, CAST_SLABS - 1)
    return pl.pallas_call(
        _in_proj_rest_kernel,
        grid=(n_i, n_j),
        in_specs=[pl.BlockSpec((PROJ_TM, D_MODEL), lambda i, j: (i + i0, 0)),
                  pl.BlockSpec((PROJ_TN, D_MODEL), lambda i, j: (j, 0)),
                  pl.BlockSpec((GATE_RANK, D_MODEL), lambda i, j: (0, 0)),
                  pl.BlockSpec(memory_space=pl.ANY),
                  pl.BlockSpec(memory_space=pl.ANY),
                  pl.BlockSpec((None, slab_rows, D_MODEL), lambda i, j: (layer, slab(i, j), 0))],
        out_specs=[pl.BlockSpec((PROJ_TM, PROJ_TN), lambda i, j: (i + i0, j)),
                   pl.BlockSpec((PROJ_TM, GATE_RANK), lambda i, j: (i + i0, 0)),
                   pl.BlockSpec((slab_rows, D_MODEL), lambda i, j: (slab(i, j), 0))],
        out_shape=[proj_shape, z_shape,
                   jax.ShapeDtypeStruct((D_MODEL, D_MODEL), BF16)],
        input_output_aliases={3: 0, 4: 1},
        scratch_shapes=[pltpu.VMEM((PROJ_TM, D_MODEL), BF16)],
        compiler_params=pltpu.CompilerParams(
            dimension_semantics=("arbitrary", "arbitrary"), vmem_limit_bytes=VMEM_LIMIT),
        name="in_proj_rest",
    )(x2, w_in_b, wz_b, proj, z_low, w_out)


def _shift_rows(h3, carry3, shift):
    pos = lax.broadcasted_iota(jnp.int32, h3.shape, 1)
    rolled = pltpu.roll(h3, shift, axis=1)
    rolled_prev = jnp.concatenate([pltpu.roll(carry3, shift, axis=1), rolled[:-1]], axis=0)
    return jnp.where(pos < shift, rolled_prev, rolled)


def _conv_piece(grp, proj_ref, w_ref, g_ref, carry_ref, y_ref):
    ts = proj_ref.shape[0]
    lo = grp * CONV_GROUP_WIDTH
    sl = slice(lo, lo + CONV_GROUP_WIDTH)
    h = (proj_ref[:, COL_C + lo:COL_C + lo + CONV_GROUP_WIDTH].astype(F32)
         * proj_ref[:, COL_U + lo:COL_U + lo + CONV_GROUP_WIDTH].astype(F32))
    h3 = h.reshape(ts // SUBLANES, SUBLANES, CONV_GROUP_WIDTH)
    carry3 = carry_ref[:, sl].reshape(1, SUBLANES, CONV_GROUP_WIDTH)
    h1 = _shift_rows(h3, carry3, 1).reshape(h.shape)
    h2 = _shift_rows(h3, carry3, 2).reshape(h.shape)
    carry_ref[:, sl] = h[ts - SUBLANES:, :]
    w = w_ref[:, sl]
    y = (proj_ref[:, COL_B + lo:COL_B + lo + CONV_GROUP_WIDTH].astype(F32)
         * (w[2:3, :] * h + w[1:2, :] * h1 + w[0:1, :] * h2))
    ms = jnp.mean(y * y, axis=-1, keepdims=True)
    y_ref[:, sl] = (y * lax.rsqrt(ms + RMS_EPS) * g_ref[:, sl]).astype(y_ref.dtype)


def _chunk_cumsum(x):
    t_blk, width = x.shape
    x3 = x.reshape(t_blk // SUBLANES, SUBLANES, width)
    pos = lax.broadcasted_iota(jnp.int32, x3.shape, 1)
    shift = 1
    while shift < SUBLANES:
        x3 = x3 + jnp.where(pos >= shift, pltpu.roll(x3, shift, axis=1), 0.0)
        shift *= 2
    vregs_per_chunk = CHUNK // SUBLANES
    x4 = x3.reshape(t_blk // CHUNK, vregs_per_chunk, SUBLANES, width)
    outs = [x4[:, 0]]
    for j in range(1, vregs_per_chunk):
        outs.append(x4[:, j] + outs[-1][:, SUBLANES - 1:SUBLANES, :])
    return jnp.stack(outs, axis=1).reshape(t_blk, width)


def _gla_prepare(head, proj_ref, z_ref, gb_ref, qd_ref, ki_ref, ke_ref, dct_ref):
    t_blk = proj_ref.shape[0]
    n_chunks = t_blk // CHUNK
    col_q, col_k = COL_Q + head * HEAD_K, COL_K + head * HEAD_K
    z = z_ref[:, head * HEAD_K:(head + 1) * HEAD_K] + gb_ref[:, head * HEAD_K:(head + 1) * HEAD_K]
    log_sig = jnp.minimum(z, 0.0) - jnp.log(1.0 + jnp.exp(-jnp.abs(z)))
    bcum = _chunk_cumsum(log_sig * (1.0 / GATE_TAU))
    qd_ref[head] = ((proj_ref[:, col_q:col_q + HEAD_K].astype(F32) * (HEAD_K ** -0.5))
                    * jnp.exp(bcum)).astype(BF16)
    k_inv_f = proj_ref[:, col_k:col_k + HEAD_K].astype(F32) * jnp.exp(-bcum)
    ki_ref[head] = k_inv_f.astype(BF16)
    decay = jnp.exp(bcum.reshape(n_chunks, CHUNK, HEAD_K)[:, CHUNK - 1:CHUNK, :])
    ke_ref[head] = (k_inv_f.reshape(n_chunks, CHUNK, HEAD_K) * decay).astype(BF16).reshape(
        t_blk, HEAD_K)
    pad = jnp.zeros((HEAD_K - n_chunks, HEAD_K), F32)
    dct_ref[head] = jnp.transpose(jnp.concatenate([decay.reshape(n_chunks, HEAD_K), pad], axis=0))


def _gla_scores(head, c, qd_ref, ki_ref, sc_ref):
    rows = slice(c * CHUNK, (c + 1) * CHUNK)
    causal = (lax.broadcasted_iota(jnp.int32, (CHUNK, CHUNK), 0)
              >= lax.broadcasted_iota(jnp.int32, (CHUNK, CHUNK), 1))
    scores = lax.dot_general(qd_ref[head, rows, :], ki_ref[head, rows, :], _NT_DIMS,
                             preferred_element_type=F32)
    sc_ref[head, rows, :] = jnp.where(causal, scores, 0.0).astype(sc_ref.dtype)


def _gla_state_step(head, c, proj_ref, ke_ref, dct_ref, state_ref, st_ref):
    rows = slice(c * CHUNK, (c + 1) * CHUNK)
    col_v = COL_V + head * HEAD_V
    state = state_ref[head]
    st_ref[head, c] = state.astype(st_ref.dtype)
    delta = lax.dot_general(ke_ref[head, rows, :], proj_ref[rows, col_v:col_v + HEAD_V], _TN_DIMS,
                            preferred_element_type=F32)
    state_ref[head] = dct_ref[head, :, c:c + 1] * state + delta


def _gla_output(head, c, proj_ref, g_ref, qd_ref, sc_ref, st_ref, y_ref):
    rows = slice(c * CHUNK, (c + 1) * CHUNK)
    col_v, col_r = COL_V + head * HEAD_V, COL_R + head * HEAD_V
    col_y = D_CONV + head * HEAD_V
    o = (jnp.dot(sc_ref[head, rows, :], proj_ref[rows, col_v:col_v + HEAD_V],
                 preferred_element_type=F32)
         + jnp.dot(qd_ref[head, rows, :], st_ref[head, c], preferred_element_type=F32))
    ms = jnp.mean(o * o, axis=-1, keepdims=True)
    o_n = o * lax.rsqrt(ms + RMS_EPS) * g_ref[:, head * HEAD_V:(head + 1) * HEAD_V]
    r_c = proj_ref[rows, col_r:col_r + HEAD_V].astype(F32)
    y_ref[rows, col_y:col_y + HEAD_V] = (o_n * (r_c * jax.nn.sigmoid(r_c))).astype(y_ref.dtype)


def _mixer_kernel(proj_ref, zl_ref, x_ref, wo_ref, cw_ref, cg_ref, wg_ref, gb_ref, gg_ref,
                  ln_g_ref, ln_b_ref, o_ref, y_ref, yp_ref, mix_ref, state_ref,
                  carry_ref, qd_ref, ki_ref, ke_ref, dct_ref, z_ref, sc_ref, st_ref, *, blocks_per_seq):
    s = pl.program_id(0)

    @pl.when(s == 0)
    def _():
        y_ref[...] = jnp.zeros_like(y_ref)

    @pl.when(s % blocks_per_seq == 0)
    def _():
        state_ref[...] = jnp.zeros_like(state_ref)
        carry_ref[...] = jnp.zeros_like(carry_ref)

    yp_ref[...] = y_ref[...]

    gla_refs = (qd_ref, ki_ref, ke_ref, dct_ref)
    n_chunks = proj_ref.shape[0] // CHUNK
    vector_pieces = [functools.partial(_conv_piece, grp, proj_ref, cw_ref, cg_ref, carry_ref, y_ref)
                     for grp in range(CONV_GROUPS)]
    vector_pieces += [functools.partial(_gla_prepare, head, proj_ref, z_ref, gb_ref, *gla_refs)
                      for head in range(GLA_HEADS)]
    head_chunks = [(head, c) for c in range(n_chunks) for head in range(GLA_HEADS)]
    chunk_pieces = (
        [functools.partial(_gla_scores, h, c, qd_ref, ki_ref, sc_ref) for h, c in head_chunks]
        + [functools.partial(_gla_state_step, h, c, proj_ref, ke_ref, dct_ref, state_ref, st_ref)
           for h, c in head_chunks]
        + [functools.partial(_gla_output, h, c, proj_ref, gg_ref, qd_ref, sc_ref, st_ref, y_ref)
           for h, c in head_chunks])

    def gate_preactivation():
        z_ref[...] = jnp.dot(zl_ref[...], wg_ref[...], preferred_element_type=F32)

    def out_proj_tile(sub, n):
        rows = slice(sub * ROW_SUB, (sub + 1) * ROW_SUB)
        cols = slice(n * OUT_TN, (n + 1) * OUT_TN)
        mix_ref[rows, cols] = jnp.dot(yp_ref[rows, :], wo_ref[:, cols], preferred_element_type=F32)

    def layer_norm_rows(row0, n_rows):
        rows = slice(row0, row0 + n_rows)
        o_ref[rows, :] = _layer_norm(DN_ALPHA * x_ref[rows, :] + mix_ref[rows, :],
                                     ln_g_ref[...], ln_b_ref[...]).astype(o_ref.dtype)

    tiles_per_sub = D_MODEL // OUT_TN
    n_tiles = (o_ref.shape[0] // ROW_SUB) * tiles_per_sub
    ln_rows = ROW_SUB // 2
    ln_queue = []
    pieces = [gate_preactivation]
    for t in range(n_tiles):
        sub, n = divmod(t, tiles_per_sub)
        lo = len(vector_pieces) * t // n_tiles
        hi = len(vector_pieces) * (t + 1) // n_tiles
        pieces += vector_pieces[lo:hi] + ln_queue[:1] + [functools.partial(out_proj_tile, sub, n)]
        del ln_queue[:1]
        if n == tiles_per_sub - 1:
            ln_queue += [functools.partial(layer_norm_rows, sub * ROW_SUB + r, ln_rows)
                         for r in range(0, ROW_SUB, ln_rows)]
    for piece in pieces + chunk_pieces + ln_queue:
        piece()


def _mixer(proj, z_low, x2, w_out_b, conv_w8, conv_g, wg_b, gate_bias, gla_g, ln_g, ln_b, bsz, seq):
    m = x2.shape[0]
    n_blk = m // MIX_T
    cur_blk = lambda s: (jnp.minimum(s, n_blk - 1), 0)
    prev_blk = lambda s: (jnp.maximum(s - 1, 0), 0)
    const2 = lambda s: (0, 0)
    return pl.pallas_call(
        functools.partial(_mixer_kernel, blocks_per_seq=seq // MIX_T),
        grid=(n_blk + 1,),
        in_specs=[pl.BlockSpec((MIX_T, D_PROJ_MAIN), cur_blk),
                  pl.BlockSpec((MIX_T, GATE_RANK), cur_blk),
                  pl.BlockSpec((MIX_T, D_MODEL), prev_blk),
                  pl.BlockSpec((D_CONV + D_GLA_V, D_MODEL), const2, pipeline_mode=pl.Buffered(1)),
                  pl.BlockSpec((8, D_CONV), const2),
                  pl.BlockSpec((1, D_CONV), const2),
                  pl.BlockSpec((GATE_RANK, D_GLA_K), const2),
                  pl.BlockSpec((1, D_GLA_K), const2),
                  pl.BlockSpec((1, D_GLA_V), const2),
                  pl.BlockSpec((1, D_MODEL), const2),
                  pl.BlockSpec((1, D_MODEL), const2)],
        out_specs=pl.BlockSpec((MIX_T, D_MODEL), prev_blk),
        out_shape=jax.ShapeDtypeStruct((m, D_MODEL), BF16),
        scratch_shapes=[pltpu.VMEM((MIX_T, D_CONV + D_GLA_V), BF16),
                        pltpu.VMEM((MIX_T, D_CONV + D_GLA_V), BF16),
                        pltpu.VMEM((MIX_T, D_MODEL), F32),
                        pltpu.VMEM((GLA_HEADS, HEAD_K, HEAD_V), F32),
                        pltpu.VMEM((8, D_CONV), F32),
                        pltpu.VMEM((GLA_HEADS, MIX_T, HEAD_K), BF16),
                        pltpu.VMEM((GLA_HEADS, MIX_T, HEAD_K), BF16),
                        pltpu.VMEM((GLA_HEADS, MIX_T, HEAD_K), BF16),
                        pltpu.VMEM((GLA_HEADS, HEAD_K, HEAD_K), F32),
                        pltpu.VMEM((MIX_T, D_GLA_K), F32),
                        pltpu.VMEM((GLA_HEADS, MIX_T, CHUNK), BF16),
                        pltpu.VMEM((GLA_HEADS, MIX_T // CHUNK, HEAD_K, HEAD_V), BF16)],
        compiler_params=pltpu.CompilerParams(
            dimension_semantics=("arbitrary",), vmem_limit_bytes=VMEM_LIMIT),
        name="mixer",
    )(proj, z_low, x2, w_out_b, conv_w8, conv_g, wg_b, gate_bias, gla_g, ln_g, ln_b)


def _ffn_body(f, last, x_ref, up_tile, down_tile, g_ref, b_ref, o_ref):
    def hidden():
        h = jnp.dot(x_ref[...], up_tile(), preferred_element_type=F32)
        h = jnp.maximum(h, 0.0)
        return (h * h).astype(BF16)

    @pl.when(f == 0)
    def _():
        o_ref[...] = jnp.dot(hidden(), down_tile(), preferred_element_type=F32)

    @pl.when(jnp.logical_and(f > 0, f < last))
    def _():
        o_ref[...] += jnp.dot(hidden(), down_tile(), preferred_element_type=F32)

    @pl.when(f == last)
    def _():
        h = hidden()
        wd = down_tile()
        for s in range(o_ref.shape[0] // ROW_SUB):
            rows = slice(s * ROW_SUB, (s + 1) * ROW_SUB)
            ff = o_ref[rows, :] + jnp.dot(h[rows, :], wd, preferred_element_type=F32)
            o_ref[rows, :] = _layer_norm(DN_ALPHA * x_ref[rows, :].astype(F32) + ff,
                                         g_ref[...], b_ref[...])


def _ffn_head_kernel(x_ref, wu_ref, wd_ref, g_ref, b_ref, o_ref, wu_b_ref, wd_b_ref):
    def up_tile():
        wu_b_ref[...] = wu_ref[...].astype(BF16)
        return wu_b_ref[...]

    def down_tile():
        wd_b_ref[...] = wd_ref[...].astype(BF16)
        return wd_b_ref[...]

    _ffn_body(pl.program_id(0), pl.num_programs(0) - 1, x_ref, up_tile, down_tile, g_ref, b_ref, o_ref)


def _ffn_rest_kernel(x_ref, wu_ref, wd_ref, g_ref, b_ref, out_hbm_ref, o_ref):
    del out_hbm_ref
    _ffn_body(pl.program_id(1), pl.num_programs(1) - 1, x_ref, lambda: wu_ref[...],
              lambda: wd_ref[...], g_ref, b_ref, o_ref)


def _ffn_ln2(x1, w_ff_up, w_ff_down, ln_g, ln_b, layer):
    m = x1.shape[0]
    out_shape = jax.ShapeDtypeStruct((m, D_MODEL), F32)

    out, w_up_b, w_down_b = pl.pallas_call(
        _ffn_head_kernel,
        grid=(D_FF // FFN_HEAD_TF,),
        in_specs=[pl.BlockSpec((FFN_HEAD_TM, D_MODEL), lambda f: (0, 0), pipeline_mode=pl.Buffered(1)),
                  pl.BlockSpec((None, D_MODEL, FFN_HEAD_TF), lambda f: (layer, 0, f)),
                  pl.BlockSpec((None, FFN_HEAD_TF, D_MODEL), lambda f: (layer, f, 0)),
                  pl.BlockSpec((1, D_MODEL), lambda f: (0, 0)),
                  pl.BlockSpec((1, D_MODEL), lambda f: (0, 0))],
        out_specs=[pl.BlockSpec((FFN_HEAD_TM, D_MODEL), lambda f: (0, 0)),
                   pl.BlockSpec((D_MODEL, FFN_HEAD_TF), lambda f: (0, f)),
                   pl.BlockSpec((FFN_HEAD_TF, D_MODEL), lambda f: (f, 0))],
        out_shape=[out_shape,
                   jax.ShapeDtypeStruct((D_MODEL, D_FF), BF16),
                   jax.ShapeDtypeStruct((D_FF, D_MODEL), BF16)],
        compiler_params=pltpu.CompilerParams(
            dimension_semantics=("arbitrary",), vmem_limit_bytes=VMEM_LIMIT),
        name="ffn_head",
    )(x1, w_ff_up, w_ff_down, ln_g, ln_b)

    i0 = FFN_HEAD_TM // FFN_TM
    return pl.pallas_call(
        _ffn_rest_kernel,
        grid=(m // FFN_TM - i0, D_FF // FFN_TF),
        in_specs=[pl.BlockSpec((FFN_TM, D_MODEL), lambda i, f: (i + i0, 0)),
                  pl.BlockSpec((D_MODEL, FFN_TF), lambda i, f: (0, f)),
                  pl.BlockSpec((FFN_TF, D_MODEL), lambda i, f: (f, 0)),
                  pl.BlockSpec((1, D_MODEL), lambda i, f: (0, 0)),
                  pl.BlockSpec((1, D_MODEL), lambda i, f: (0, 0)),
                  pl.BlockSpec(memory_space=pl.ANY)],
        out_specs=pl.BlockSpec((FFN_TM, D_MODEL), lambda i, f: (i + i0, 0)),
        out_shape=out_shape,
        input_output_aliases={5: 0},
        compiler_params=pltpu.CompilerParams(
            dimension_semantics=("arbitrary", "arbitrary"), vmem_limit_bytes=VMEM_LIMIT),
        name="ffn_rest",
    )(x1, w_up_b, w_down_b, ln_g, ln_b, out)


def kernel(x, w_in, conv_w, conv_norm_g, w_gate_up, gate_bias, gla_norm_g, w_out,
           ln1_g, ln1_b, w_ff_up, w_ff_down, ln2_g, ln2_b):
    bsz, seq, _ = x.shape
    assert seq % MIX_T == 0 and MIX_T % CHUNK == 0 and MIX_T % ROW_SUB == 0
    x2 = x.reshape(bsz * seq, D_MODEL)
    w_in_t = jnp.swapaxes(w_in, 1, 2)
    for l in range(DEPTH):
        wg_b = w_gate_up[l].astype(BF16)
        gb = gate_bias[l].reshape(1, D_GLA_K)
        conv_w8 = jnp.pad(conv_w[l], ((0, 8 - conv_w.shape[1]), (0, 0)))

        proj, z_low, w_out_b = _in_proj(x2, w_in_t, w_out, l)
        x1 = _mixer(proj, z_low, x2, w_out_b, conv_w8, conv_norm_g[l].reshape(1, D_CONV),
                    wg_b, gb, gla_norm_g[l].reshape(1, D_GLA_V),
                    ln1_g[l].reshape(1, D_MODEL), ln1_b[l].reshape(1, D_MODEL), bsz, seq)
        x2 = _ffn_ln2(x1, w_ff_up, w_ff_down,
                      ln2_g[l].reshape(1, D_MODEL), ln2_b[l].reshape(1, D_MODEL), l)
    return x2.reshape(bsz, seq, D_MODEL)
```

```python
import functools

import jax
import jax.numpy as jnp
from jax import lax
from jax.experimental import pallas as pl
from jax.experimental.pallas import tpu as pltpu

F32 = jnp.float32
BF16 = jnp.bfloat16

D_MODEL = 2048
D_CONV = 1024
CONV_GROUPS = 8
CONV_GROUP_WIDTH = D_CONV // CONV_GROUPS
GLA_HEADS = 4
HEAD_K = 128
HEAD_V = 256
D_GLA_K = GLA_HEADS * HEAD_K
D_GLA_V = GLA_HEADS * HEAD_V
GATE_RANK = 16
GATE_TAU = 16.0
CHUNK = 64
D_FF = 4 * D_MODEL
LN_EPS = 1e-5
RMS_EPS = 1e-6
DEPTH = 1
DN_ALPHA = (2.0 * DEPTH) ** 0.25

D_PROJ_MAIN = 3 * D_CONV + 2 * D_GLA_K + 2 * D_GLA_V

COL_B, COL_C, COL_U = 0, D_CONV, 2 * D_CONV
COL_Q = 3 * D_CONV
COL_K = COL_Q + D_GLA_K
COL_V = COL_K + D_GLA_K
COL_R = COL_V + D_GLA_V

VMEM_LIMIT = 60 * 1024 * 1024
SUBLANES = 8

PROJ_TM = 1024
PROJ_HEAD_TM = 2048
PROJ_HEAD_TN = 512
PROJ_TN = 1536
CAST_SLABS = 16
MIX_T = 512
ROW_SUB = 256
OUT_TN = 512
FFN_HEAD_TM = 1024
FFN_HEAD_TF = 512
FFN_TM = 512
FFN_TF = 2048

_NT_DIMS = (((1,), (1,)), ((), ()))
_TN_DIMS = (((0,), (0,)), ((), ()))


def _layer_norm(y, g, b):
    mu = jnp.mean(y, axis=-1, keepdims=True)
    yc = y - mu
    var = jnp.mean(yc * yc, axis=-1, keepdims=True)
    return yc * lax.rsqrt(var + LN_EPS) * g + b


def _in_proj_head_kernel(x_ref, w_ref, wz_ref, o_ref, z_ref, wb_ref, wzb_ref, xb_ref):
    @pl.when(pl.program_id(0) == 0)
    def _():
        xb_ref[...] = x_ref[...].astype(BF16)
        wzb_ref[...] = wz_ref[...].astype(BF16)
        z_ref[...] = lax.dot_general(xb_ref[...], wzb_ref[...], _NT_DIMS,
                                     preferred_element_type=F32).astype(z_ref.dtype)

    wb_ref[...] = w_ref[...].astype(BF16)
    o_ref[...] = lax.dot_general(xb_ref[...], wb_ref[...], _NT_DIMS,
                                 preferred_element_type=F32).astype(o_ref.dtype)


def _in_proj_rest_kernel(x_ref, wb_ref, wzb_ref, proj_hbm_ref, z_hbm_ref, wo_ref,
                         o_ref, z_ref, wo_b_ref, xb_ref):
    del proj_hbm_ref, z_hbm_ref

    @pl.when(pl.program_id(1) == 0)
    def _():
        xb_ref[...] = x_ref[...].astype(BF16)
        z_ref[...] = lax.dot_general(xb_ref[...], wzb_ref[...], _NT_DIMS,
                                     preferred_element_type=F32).astype(z_ref.dtype)

    wo_b_ref[...] = wo_ref[...].astype(BF16)
    o_ref[...] = lax.dot_general(xb_ref[...], wb_ref[...], _NT_DIMS,
                                 preferred_element_type=F32).astype(o_ref.dtype)


def _in_proj(x2, w_in_t, w_out, layer):
    m = x2.shape[0]
    i0 = PROJ_HEAD_TM // PROJ_TM
    n_i = m // PROJ_TM - i0
    proj_shape = jax.ShapeDtypeStruct((m, D_PROJ_MAIN), BF16)
    z_shape = jax.ShapeDtypeStruct((m, GATE_RANK), BF16)

    n_jh = D_PROJ_MAIN // PROJ_HEAD_TN
    proj, z_low, w_in_b, wz_b = pl.pallas_call(
        _in_proj_head_kernel,
        grid=(n_jh,),
        in_specs=[pl.BlockSpec((PROJ_HEAD_TM, D_MODEL), lambda j: (0, 0), pipeline_mode=pl.Buffered(1)),
                  pl.BlockSpec((None, PROJ_HEAD_TN, D_MODEL), lambda j: (layer, j, 0)),
                  pl.BlockSpec((None, GATE_RANK, D_MODEL),
                               lambda j: (layer, D_PROJ_MAIN // GATE_RANK, 0))],
        out_specs=[pl.BlockSpec((PROJ_HEAD_TM, PROJ_HEAD_TN), lambda j: (0, j)),
                   pl.BlockSpec((PROJ_HEAD_TM, GATE_RANK), lambda j: (0, 0)),
                   pl.BlockSpec((PROJ_HEAD_TN, D_MODEL), lambda j: (j, 0)),
                   pl.BlockSpec((GATE_RANK, D_MODEL), lambda j: (0, 0))],
        out_shape=[proj_shape, z_shape,
                   jax.ShapeDtypeStruct((D_PROJ_MAIN, D_MODEL), BF16),
                   jax.ShapeDtypeStruct((GATE_RANK, D_MODEL), BF16)],
        scratch_shapes=[pltpu.VMEM((PROJ_HEAD_TM, D_MODEL), BF16)],
        compiler_params=pltpu.CompilerParams(
            dimension_semantics=("arbitrary",), vmem_limit_bytes=VMEM_LIMIT),
        name="in_proj_head",
    )(x2, w_in_t, w_in_t)

    n_j = D_PROJ_MAIN // PROJ_TN
    slab_rows = D_MODEL // CAST_SLABS
    slab = lambda i, j: jnp.minimum(i * n_j + j, CAST_SLABS - 1)
    return pl.pallas_call(
        _in_proj_rest_kernel,
        grid=(n_i, n_j),
        in_specs=[pl.BlockSpec((PROJ_TM, D_MODEL), lambda i, j: (i + i0, 0)),
                  pl.BlockSpec((PROJ_TN, D_MODEL), lambda i, j: (j, 0)),
                  pl.BlockSpec((GATE_RANK, D_MODEL), lambda i, j: (0, 0)),
                  pl.BlockSpec(memory_space=pl.ANY),
                  pl.BlockSpec(memory_space=pl.ANY),
                  pl.BlockSpec((None, slab_rows, D_MODEL), lambda i, j: (layer, slab(i, j), 0))],
        out_specs=[pl.BlockSpec((PROJ_TM, PROJ_TN), lambda i, j: (i + i0, j)),
                   pl.BlockSpec((PROJ_TM, GATE_RANK), lambda i, j: (i + i0, 0)),
                   pl.BlockSpec((slab_rows, D_MODEL), lambda i, j: (slab(i, j), 0))],
        out_shape=[proj_shape, z_shape,
                   jax.ShapeDtypeStruct((D_MODEL, D_MODEL), BF16)],
        input_output_aliases={3: 0, 4: 1},
        scratch_shapes=[pltpu.VMEM((PROJ_TM, D_MODEL), BF16)],
        compiler_params=pltpu.CompilerParams(
            dimension_semantics=("arbitrary", "arbitrary"), vmem_limit_bytes=VMEM_LIMIT),
        name="in_proj_rest",
    )(x2, w_in_b, wz_b, proj, z_low, w_out)


def _shift_rows(h3, carry3, shift):
    pos = lax.broadcasted_iota(jnp.int32, h3.shape, 1)
    rolled = pltpu.roll(h3, shift, axis=1)
    rolled_prev = jnp.concatenate([pltpu.roll(carry3, shift, axis=1), rolled[:-1]], axis=0)
    return jnp.where(pos < shift, rolled_prev, rolled)


def _conv_piece(grp, proj_ref, w_ref, g_ref, carry_ref, y_ref):
    ts = proj_ref.shape[0]
    lo = grp * CONV_GROUP_WIDTH
    sl = slice(lo, lo + CONV_GROUP_WIDTH)
    h = (proj_ref[:, COL_C + lo:COL_C + lo + CONV_GROUP_WIDTH].astype(F32)
         * proj_ref[:, COL_U + lo:COL_U + lo + CONV_GROUP_WIDTH].astype(F32))
    h3 = h.reshape(ts // SUBLANES, SUBLANES, CONV_GROUP_WIDTH)
    carry3 = carry_ref[:, sl].reshape(1, SUBLANES, CONV_GROUP_WIDTH)
    h1 = _shift_rows(h3, carry3, 1).reshape(h.shape)
    h2 = _shift_rows(h3, carry3, 2).reshape(h.shape)
    carry_ref[:, sl] = h[ts - SUBLANES:, :]
    w = w_ref[:, sl]
    y = (proj_ref[:, COL_B + lo:COL_B + lo + CONV_GROUP_WIDTH].astype(F32)
         * (w[2:3, :] * h + w[1:2, :] * h1 + w[0:1, :] * h2))
    ms = jnp.mean(y * y, axis=-1, keepdims=True)
    y_ref[:, sl] = (y * lax.rsqrt(ms + RMS_EPS) * g_ref[:, sl]).astype(y_ref.dtype)


def _chunk_cumsum(x):
    t_blk, width = x.shape
    x3 = x.reshape(t_blk // SUBLANES, SUBLANES, width)
    pos = lax.broadcasted_iota(jnp.int32, x3.shape, 1)
    shift = 1
    while shift < SUBLANES:
        x3 = x3 + jnp.where(pos >= shift, pltpu.roll(x3, shift, axis=1), 0.0)
        shift *= 2
    vregs_per_chunk = CHUNK // SUBLANES
    x4 = x3.reshape(t_blk // CHUNK, vregs_per_chunk, SUBLANES, width)
    outs = [x4[:, 0]]
    for j in range(1, vregs_per_chunk):
        outs.append(x4[:, j] + outs[-1][:, SUBLANES - 1:SUBLANES, :])
    return jnp.stack(outs, axis=1).reshape(t_blk, width)


def _gla_prepare(head, proj_ref, z_ref, gb_ref, qd_ref, ki_ref, ke_ref, dct_ref):
    t_blk = proj_ref.shape[0]
    n_chunks = t_blk // CHUNK
    col_q, col_k = COL_Q + head * HEAD_K, COL_K + head * HEAD_K
    z = z_ref[:, head * HEAD_K:(head + 1) * HEAD_K] + gb_ref[:, head * HEAD_K:(head + 1) * HEAD_K]
    log_sig = jnp.minimum(z, 0.0) - jnp.log(1.0 + jnp.exp(-jnp.abs(z)))
    bcum = _chunk_cumsum(log_sig * (1.0 / GATE_TAU))
    qd_ref[head] = ((proj_ref[:, col_q:col_q + HEAD_K].astype(F32) * (HEAD_K ** -0.5))
                    * jnp.exp(bcum)).astype(BF16)
    k_inv_f = proj_ref[:, col_k:col_k + HEAD_K].astype(F32) * jnp.exp(-bcum)
    ki_ref[head] = k_inv_f.astype(BF16)
    decay = jnp.exp(bcum.reshape(n_chunks, CHUNK, HEAD_K)[:, CHUNK - 1:CHUNK, :])
    ke_ref[head] = (k_inv_f.reshape(n_chunks, CHUNK, HEAD_K) * decay).astype(BF16).reshape(
        t_blk, HEAD_K)
    pad = jnp.zeros((HEAD_K - n_chunks, HEAD_K), F32)
    dct_ref[head] = jnp.transpose(jnp.concatenate([decay.reshape(n_chunks, HEAD_K), pad], axis=0))


def _gla_scores(head, c, qd_ref, ki_ref, sc_ref):
    rows = slice(c * CHUNK, (c + 1) * CHUNK)
    causal = (lax.broadcasted_iota(jnp.int32, (CHUNK, CHUNK), 0)
              >= lax.broadcasted_iota(jnp.int32, (CHUNK, CHUNK), 1))
    scores = lax.dot_general(qd_ref[head, rows, :], ki_ref[head, rows, :], _NT_DIMS,
                             preferred_element_type=F32)
    sc_ref[head, rows, :] = jnp.where(causal, scores, 0.0).astype(sc_ref.dtype)


def _gla_state_step(head, c, proj_ref, ke_ref, dct_ref, state_ref, st_ref):
    rows = slice(c * CHUNK, (c + 1) * CHUNK)
    col_v = COL_V + head * HEAD_V
    state = state_ref[head]
    st_ref[head, c] = state.astype(st_ref.dtype)
    delta = lax.dot_general(ke_ref[head, rows, :], proj_ref[rows, col_v:col_v + HEAD_V], _TN_DIMS,
                            preferred_element_type=F32)
    state_ref[head] = dct_ref[head, :, c:c + 1] * state + delta


def _gla_output(head, c, proj_ref, g_ref, qd_ref, sc_ref, st_ref, y_ref):
    rows = slice(c * CHUNK, (c + 1) * CHUNK)
    col_v, col_r = COL_V + head * HEAD_V, COL_R + head * HEAD_V
    col_y = D_CONV + head * HEAD_V
    o = (jnp.dot(sc_ref[head, rows, :], proj_ref[rows, col_v:col_v + HEAD_V],
                 preferred_element_type=F32)
         + jnp.dot(qd_ref[head, rows, :], st_ref[head, c], preferred_element_type=F32))
    ms = jnp.mean(o * o, axis=-1, keepdims=True)
    o_n = o * lax.rsqrt(ms + RMS_EPS) * g_ref[:, head * HEAD_V:(head + 1) * HEAD_V]
    r_c = proj_ref[rows, col_r:col_r + HEAD_V].astype(F32)
    y_ref[rows, col_y:col_y + HEAD_V] = (o_n * (r_c * jax.nn.sigmoid(r_c))).astype(y_ref.dtype)


def _mixer_kernel(proj_ref, zl_ref, x_ref, wo_ref, cw_ref, cg_ref, wg_ref, gb_ref, gg_ref,
                  ln_g_ref, ln_b_ref, o_ref, y_ref, yp_ref, mix_ref, state_ref,
                  carry_ref, qd_ref, ki_ref, ke_ref, dct_ref, z_ref, sc_ref, st_ref, *, blocks_per_seq):
    s = pl.program_id(0)

    @pl.when(s == 0)
    def _():
        y_ref[...] = jnp.zeros_like(y_ref)

    @pl.when(s % blocks_per_seq == 0)
    def _():
        state_ref[...] = jnp.zeros_like(state_ref)
        carry_ref[...] = jnp.zeros_like(carry_ref)

    yp_ref[...] = y_ref[...]

    gla_refs = (qd_ref, ki_ref, ke_ref, dct_ref)
    n_chunks = proj_ref.shape[0] // CHUNK
    vector_pieces = [functools.partial(_conv_piece, grp, proj_ref, cw_ref, cg_ref, carry_ref, y_ref)
                     for grp in range(CONV_GROUPS)]
    vector_pieces += [functools.partial(_gla_prepare, head, proj_ref, z_ref, gb_ref, *gla_refs)
                      for head in range(GLA_HEADS)]
    head_chunks = [(head, c) for c in range(n_chunks) for head in range(GLA_HEADS)]
    chunk_pieces = (
        [functools.partial(_gla_scores, h, c, qd_ref, ki_ref, sc_ref) for h, c in head_chunks]
        + [functools.partial(_gla_state_step, h, c, proj_ref, ke_ref, dct_ref, state_ref, st_ref)
           for h, c in head_chunks]
        + [functools.partial(_gla_output, h, c, proj_ref, gg_ref, qd_ref, sc_ref, st_ref, y_ref)
           for h, c in head_chunks])

    def gate_preactivation():
        z_ref[...] = jnp.dot(zl_ref[...], wg_ref[...], preferred_element_type=F32)

    def out_proj_tile(sub, n):
        rows = slice(sub * ROW_SUB, (sub + 1) * ROW_SUB)
        cols = slice(n * OUT_TN, (n + 1) * OUT_TN)
        mix_ref[rows, cols] = jnp.dot(yp_ref[rows, :], wo_ref[:, cols], preferred_element_type=F32)

    def layer_norm_rows(row0, n_rows):
        rows = slice(row0, row0 + n_rows)
        o_ref[rows, :] = _layer_norm(DN_ALPHA * x_ref[rows, :] + mix_ref[rows, :],
                                     ln_g_ref[...], ln_b_ref[...]).astype(o_ref.dtype)

    tiles_per_sub = D_MODEL // OUT_TN
    n_tiles = (o_ref.shape[0] // ROW_SUB) * tiles_per_sub
    ln_rows = ROW_SUB // 2
    ln_queue = []
    pieces = [gate_preactivation]
    for t in range(n_tiles):
        sub, n = divmod(t, tiles_per_sub)
        lo = len(vector_pieces) * t // n_tiles
        hi = len(vector_pieces) * (t + 1) // n_tiles
        pieces += vector_pieces[lo:hi] + ln_queue[:1] + [functools.partial(out_proj_tile, sub, n)]
        del ln_queue[:1]
        if n == tiles_per_sub - 1:
            ln_queue += [functools.partial(layer_norm_rows, sub * ROW_SUB + r, ln_rows)
                         for r in range(0, ROW_SUB, ln_rows)]
    for piece in pieces + chunk_pieces + ln_queue:
        piece()


def _mixer(proj, z_low, x2, w_out_b, conv_w8, conv_g, wg_b, gate_bias, gla_g, ln_g, ln_b, bsz, seq):
    m = x2.shape[0]
    n_blk = m // MIX_T
    cur_blk = lambda s: (jnp.minimum(s, n_blk - 1), 0)
    prev_blk = lambda s: (jnp.maximum(s - 1, 0), 0)
    const2 = lambda s: (0, 0)
    return pl.pallas_call(
        functools.partial(_mixer_kernel, blocks_per_seq=seq // MIX_T),
        grid=(n_blk + 1,),
        in_specs=[pl.BlockSpec((MIX_T, D_PROJ_MAIN), cur_blk),
                  pl.BlockSpec((MIX_T, GATE_RANK), cur_blk),
                  pl.BlockSpec((MIX_T, D_MODEL), prev_blk),
                  pl.BlockSpec((D_CONV + D_GLA_V, D_MODEL), const2, pipeline_mode=pl.Buffered(1)),
                  pl.BlockSpec((8, D_CONV), const2),
                  pl.BlockSpec((1, D_CONV), const2),
                  pl.BlockSpec((GATE_RANK, D_GLA_K), const2),
                  pl.BlockSpec((1, D_GLA_K), const2),
                  pl.BlockSpec((1, D_GLA_V), const2),
                  pl.BlockSpec((1, D_MODEL), const2),
                  pl.BlockSpec((1, D_MODEL), const2)],
        out_specs=pl.BlockSpec((MIX_T, D_MODEL), prev_blk),
        out_shape=jax.ShapeDtypeStruct((m, D_MODEL), BF16),
        scratch_shapes=[pltpu.VMEM((MIX_T, D_CONV + D_GLA_V), BF16),
                        pltpu.VMEM((MIX_T, D_CONV + D_GLA_V), BF16),
                        pltpu.VMEM((MIX_T, D_MODEL), F32),
                        pltpu.VMEM((GLA_HEADS, HEAD_K, HEAD_V), F32),
                        pltpu.VMEM((8, D_CONV), F32),
                        pltpu.VMEM((GLA_HEADS, MIX_T, HEAD_K), BF16),
                        pltpu.VMEM((GLA_HEADS, MIX_T, HEAD_K), BF16),
                        pltpu.VMEM((GLA_HEADS, MIX_T, HEAD_K), BF16),
                        pltpu.VMEM((GLA_HEADS, HEAD_K, HEAD_K), F32),
                        pltpu.VMEM((MIX_T, D_GLA_K), F32),
                        pltpu.VMEM((GLA_HEADS, MIX_T, CHUNK), BF16),
                        pltpu.VMEM((GLA_HEADS, MIX_T // CHUNK, HEAD_K, HEAD_V), BF16)],
        compiler_params=pltpu.CompilerParams(
            dimension_semantics=("arbitrary",), vmem_limit_bytes=VMEM_LIMIT),
        name="mixer",
    )(proj, z_low, x2, w_out_b, conv_w8, conv_g, wg_b, gate_bias, gla_g, ln_g, ln_b)


def _ffn_body(f, last, x_ref, up_tile, down_tile, g_ref, b_ref, o_ref):
    def hidden():
        h = jnp.dot(x_ref[...], up_tile(), preferred_element_type=F32)
        h = jnp.maximum(h, 0.0)
        return (h * h).astype(BF16)

    @pl.when(f == 0)
    def _():
        o_ref[...] = jnp.dot(hidden(), down_tile(), preferred_element_type=F32)

    @pl.when(jnp.logical_and(f > 0, f < last))
    def _():
        o_ref[...] += jnp.dot(hidden(), down_tile(), preferred_element_type=F32)

    @pl.when(f == last)
    def _():
        h = hidden()
        wd = down_tile()
        for s in range(o_ref.shape[0] // ROW_SUB):
            rows = slice(s * ROW_SUB, (s + 1) * ROW_SUB)
            ff = o_ref[rows, :] + jnp.dot(h[rows, :], wd, preferred_element_type=F32)
            o_ref[rows, :] = _layer_norm(DN_ALPHA * x_ref[rows, :].astype(F32) + ff,
                                         g_ref[...], b_ref[...])


def _ffn_head_kernel(x_ref, wu_ref, wd_ref, g_ref, b_ref, o_ref, wu_b_ref, wd_b_ref):
    def up_tile():
        wu_b_ref[...] = wu_ref[...].astype(BF16)
        return wu_b_ref[...]

    def down_tile():
        wd_b_ref[...] = wd_ref[...].astype(BF16)
        return wd_b_ref[...]

    _ffn_body(pl.program_id(0), pl.num_programs(0) - 1, x_ref, up_tile, down_tile, g_ref, b_ref, o_ref)


def _ffn_rest_kernel(x_ref, wu_ref, wd_ref, g_ref, b_ref, out_hbm_ref, o_ref):
    del out_hbm_ref
    _ffn_body(pl.program_id(1), pl.num_programs(1) - 1, x_ref, lambda: wu_ref[...],
              lambda: wd_ref[...], g_ref, b_ref, o_ref)


def _ffn_ln2(x1, w_ff_up, w_ff_down, ln_g, ln_b, layer):
    m = x1.shape[0]
    out_shape = jax.ShapeDtypeStruct((m, D_MODEL), F32)

    out, w_up_b, w_down_b = pl.pallas_call(
        _ffn_head_kernel,
        grid=(D_FF // FFN_HEAD_TF,),
        in_specs=[pl.BlockSpec((FFN_HEAD_TM, D_MODEL), lambda f: (0, 0), pipeline_mode=pl.Buffered(1)),
                  pl.BlockSpec((None, D_MODEL, FFN_HEAD_TF), lambda f: (layer, 0, f)),
                  pl.BlockSpec((None, FFN_HEAD_TF, D_MODEL), lambda f: (layer, f, 0)),
                  pl.BlockSpec((1, D_MODEL), lambda f: (0, 0)),
                  pl.BlockSpec((1, D_MODEL), lambda f: (0, 0))],
        out_specs=[pl.BlockSpec((FFN_HEAD_TM, D_MODEL), lambda f: (0, 0)),
                   pl.BlockSpec((D_MODEL, FFN_HEAD_TF), lambda f: (0, f)),
                   pl.BlockSpec((FFN_HEAD_TF, D_MODEL), lambda f: (f, 0))],
        out_shape=[out_shape,
                   jax.ShapeDtypeStruct((D_MODEL, D_FF), BF16),
                   jax.ShapeDtypeStruct((D_FF, D_MODEL), BF16)],
        compiler_params=pltpu.CompilerParams(
            dimension_semantics=("arbitrary",), vmem_limit_bytes=VMEM_LIMIT),
        name="ffn_head",
    )(x1, w_ff_up, w_ff_down, ln_g, ln_b)

    i0 = FFN_HEAD_TM // FFN_TM
    return pl.pallas_call(
        _ffn_rest_kernel,
        grid=(m // FFN_TM - i0, D_FF // FFN_TF),
        in_specs=[pl.BlockSpec((FFN_TM, D_MODEL), lambda i, f: (i + i0, 0)),
                  pl.BlockSpec((D_MODEL, FFN_TF), lambda i, f: (0, f)),
                  pl.BlockSpec((FFN_TF, D_MODEL), lambda i, f: (f, 0)),
                  pl.BlockSpec((1, D_MODEL), lambda i, f: (0, 0)),
                  pl.BlockSpec((1, D_MODEL), lambda i, f: (0, 0)),
                  pl.BlockSpec(memory_space=pl.ANY)],
        out_specs=pl.BlockSpec((FFN_TM, D_MODEL), lambda i, f: (i + i0, 0)),
        out_shape=out_shape,
        input_output_aliases={5: 0},
        compiler_params=pltpu.CompilerParams(
            dimension_semantics=("arbitrary", "arbitrary"), vmem_limit_bytes=VMEM_LIMIT),
        name="ffn_rest",
    )(x1, w_up_b, w_down_b, ln_g, ln_b, out)


def kernel(x, w_in, conv_w, conv_norm_g, w_gate_up, gate_bias, gla_norm_g, w_out,
           ln1_g, ln1_b, w_ff_up, w_ff_down, ln2_g, ln2_b):
    bsz, seq, _ = x.shape
    assert seq % MIX_T == 0 and MIX_T % CHUNK == 0 and MIX_T % ROW_SUB == 0
    x2 = x.reshape(bsz * seq, D_MODEL)
    w_in_t = jnp.swapaxes(w_in, 1, 2)
    for l in range(DEPTH):
        wg_b = w_gate_up[l].astype(BF16)
        gb = gate_bias[l].reshape(1, D_GLA_K)
        conv_w8 = jnp.pad(conv_w[l], ((0, 8 - conv_w.shape[1]), (0, 0)))

        proj, z_low, w_out_b = _in_proj(x2, w_in_t, w_out, l)
        x1 = _mixer(proj, z_low, x2, w_out_b, conv_w8, conv_norm_g[l].reshape(1, D_CONV),
                    wg_b, gb, gla_norm_g[l].reshape(1, D_GLA_V),
                    ln1_g[l].reshape(1, D_MODEL), ln1_b[l].reshape(1, D_MODEL), bsz, seq)
        x2 = _ffn_ln2(x1, w_ff_up, w_ff_down,
                      ln2_g[l].reshape(1, D_MODEL), ln2_b[l].reshape(1, D_MODEL), l)
    return x2.reshape(bsz, seq, D_MODEL)
```

```python
import functools

import jax
import jax.numpy as jnp
from jax import lax
from jax.experimental import pallas as pl
from jax.experimental.pallas import tpu as pltpu

F32 = jnp.float32
BF16 = jnp.bfloat16

D_MODEL = 2048
D_CONV = 1024
CONV_GROUPS = 8
CONV_GROUP_WIDTH = D_CONV // CONV_GROUPS
GLA_HEADS = 4
HEAD_K = 128
HEAD_V = 256
D_GLA_K = GLA_HEADS * HEAD_K
D_GLA_V = GLA_HEADS * HEAD_V
GATE_RANK = 16
GATE_TAU = 16.0
CHUNK = 64
D_FF = 4 * D_MODEL
LN_EPS = 1e-5
RMS_EPS = 1e-6
DEPTH = 1
DN_ALPHA = (2.0 * DEPTH) ** 0.25

D_PROJ_MAIN = 3 * D_CONV + 2 * D_GLA_K + 2 * D_GLA_V

COL_B, COL_C, COL_U = 0, D_CONV, 2 * D_CONV
COL_Q = 3 * D_CONV
COL_K = COL_Q + D_GLA_K
COL_V = COL_K + D_GLA_K
COL_R = COL_V + D_GLA_V

VMEM_LIMIT = 60 * 1024 * 1024
SUBLANES = 8

PROJ_TM = 1024
PROJ_HEAD_TM = 2048
PROJ_HEAD_TN = 512
PROJ_TN = 2048
CAST_SLABS = 16
MIX_T = 512
ROW_SUB = 256
OUT_TN = 512
FFN_HEAD_TM = 1024
FFN_HEAD_TF = 512
FFN_TM = 512
FFN_TF = 2048

_NT_DIMS = (((1,), (1,)), ((), ()))
_TN_DIMS = (((0,), (0,)), ((), ()))


def _layer_norm(y, g, b):
    mu = jnp.mean(y, axis=-1, keepdims=True)
    yc = y - mu
    var = jnp.mean(yc * yc, axis=-1, keepdims=True)
    return yc * lax.rsqrt(var + LN_EPS) * g + b


def _in_proj_head_kernel(x_ref, w_ref, wz_ref, o_ref, z_ref, wb_ref, wzb_ref, xb_ref):
    @pl.when(pl.program_id(0) == 0)
    def _():
        xb_ref[...] = x_ref[...].astype(BF16)
        wzb_ref[...] = wz_ref[...].astype(BF16)
        z_ref[...] = lax.dot_general(xb_ref[...], wzb_ref[...], _NT_DIMS,
                                     preferred_element_type=F32).astype(z_ref.dtype)

    wb_ref[...] = w_ref[...].astype(BF16)
    o_ref[...] = lax.dot_general(xb_ref[...], wb_ref[...], _NT_DIMS,
                                 preferred_element_type=F32).astype(o_ref.dtype)


def _in_proj_rest_kernel(x_ref, wb_ref, wzb_ref, proj_hbm_ref, z_hbm_ref, wo_ref,
                         o_ref, z_ref, wo_b_ref, xb_ref):
    del proj_hbm_ref, z_hbm_ref

    @pl.when(pl.program_id(1) == 0)
    def _():
        xb_ref[...] = x_ref[...].astype(BF16)
        z_ref[...] = lax.dot_general(xb_ref[...], wzb_ref[...], _NT_DIMS,
                                     preferred_element_type=F32).astype(z_ref.dtype)

    wo_b_ref[...] = wo_ref[...].astype(BF16)
    o_ref[...] = lax.dot_general(xb_ref[...], wb_ref[...], _NT_DIMS,
                                 preferred_element_type=F32).astype(o_ref.dtype)


def _in_proj(x2, w_in_t, w_out, layer):
    m = x2.shape[0]
    i0 = PROJ_HEAD_TM // PROJ_TM
    n_i = m // PROJ_TM - i0
    proj_shape = jax.ShapeDtypeStruct((m, D_PROJ_MAIN), BF16)
    z_shape = jax.ShapeDtypeStruct((m, GATE_RANK), BF16)

    n_jh = D_PROJ_MAIN // PROJ_HEAD_TN
    proj, z_low, w_in_b, wz_b = pl.pallas_call(
        _in_proj_head_kernel,
        grid=(n_jh,),
        in_specs=[pl.BlockSpec((PROJ_HEAD_TM, D_MODEL), lambda j: (0, 0), pipeline_mode=pl.Buffered(1)),
                  pl.BlockSpec((None, PROJ_HEAD_TN, D_MODEL), lambda j: (layer, j, 0)),
                  pl.BlockSpec((None, GATE_RANK, D_MODEL),
                               lambda j: (layer, D_PROJ_MAIN // GATE_RANK, 0))],
        out_specs=[pl.BlockSpec((PROJ_HEAD_TM, PROJ_HEAD_TN), lambda j: (0, j)),
                   pl.BlockSpec((PROJ_HEAD_TM, GATE_RANK), lambda j: (0, 0)),
                   pl.BlockSpec((PROJ_HEAD_TN, D_MODEL), lambda j: (j, 0)),
                   pl.BlockSpec((GATE_RANK, D_MODEL), lambda j: (0, 0))],
        out_shape=[proj_shape, z_shape,
                   jax.ShapeDtypeStruct((D_PROJ_MAIN, D_MODEL), BF16),
                   jax.ShapeDtypeStruct((GATE_RANK, D_MODEL), BF16)],
        scratch_shapes=[pltpu.VMEM((PROJ_HEAD_TM, D_MODEL), BF16)],
        compiler_params=pltpu.CompilerParams(
            dimension_semantics=("arbitrary",), vmem_limit_bytes=VMEM_LIMIT),
        name="in_proj_head",
    )(x2, w_in_t, w_in_t)

    n_j = D_PROJ_MAIN // PROJ_TN
    slab_rows = D_MODEL // CAST_SLABS
    slab = lambda i, j: jnp.minimum(i * n_j + j, CAST_SLABS - 1)
    return pl.pallas_call(
        _in_proj_rest_kernel,
        grid=(n_i, n_j),
        in_specs=[pl.BlockSpec((PROJ_TM, D_MODEL), lambda i, j: (i + i0, 0)),
                  pl.BlockSpec((PROJ_TN, D_MODEL), lambda i, j: (j, 0)),
                  pl.BlockSpec((GATE_RANK, D_MODEL), lambda i, j: (0, 0)),
                  pl.BlockSpec(memory_space=pl.ANY),
                  pl.BlockSpec(memory_space=pl.ANY),
                  pl.BlockSpec((None, slab_rows, D_MODEL), lambda i, j: (layer, slab(i, j), 0))],
        out_specs=[pl.BlockSpec((PROJ_TM, PROJ_TN), lambda i, j: (i + i0, j)),
                   pl.BlockSpec((PROJ_TM, GATE_RANK), lambda i, j: (i + i0, 0)),
                   pl.BlockSpec((slab_rows, D_MODEL), lambda i, j: (slab(i, j), 0))],
        out_shape=[proj_shape, z_shape,
                   jax.ShapeDtypeStruct((D_MODEL, D_MODEL), BF16)],
        input_output_aliases={3: 0, 4: 1},
        scratch_shapes=[pltpu.VMEM((PROJ_TM, D_MODEL), BF16)],
        compiler_params=pltpu.CompilerParams(
            dimension_semantics=("arbitrary", "arbitrary"), vmem_limit_bytes=VMEM_LIMIT),
        name="in_proj_rest",
    )(x2, w_in_b, wz_b, proj, z_low, w_out)


def _shift_rows(h3, carry3, shift):
    pos = lax.broadcasted_iota(jnp.int32, h3.shape, 1)
    rolled = pltpu.roll(h3, shift, axis=1)
    rolled_prev = jnp.concatenate([pltpu.roll(carry3, shift, axis=1), rolled[:-1]], axis=0)
    return jnp.where(pos < shift, rolled_prev, rolled)


def _conv_piece(grp, proj_ref, w_ref, g_ref, carry_ref, y_ref):
    ts = proj_ref.shape[0]
    lo = grp * CONV_GROUP_WIDTH
    sl = slice(lo, lo + CONV_GROUP_WIDTH)
    h = (proj_ref[:, COL_C + lo:COL_C + lo + CONV_GROUP_WIDTH].astype(F32)
         * proj_ref[:, COL_U + lo:COL_U + lo + CONV_GROUP_WIDTH].astype(F32))
    h3 = h.reshape(ts // SUBLANES, SUBLANES, CONV_GROUP_WIDTH)
    carry3 = carry_ref[:, sl].reshape(1, SUBLANES, CONV_GROUP_WIDTH)
    h1 = _shift_rows(h3, carry3, 1).reshape(h.shape)
    h2 = _shift_rows(h3, carry3, 2).reshape(h.shape)
    carry_ref[:, sl] = h[ts - SUBLANES:, :]
    w = w_ref[:, sl]
    y = (proj_ref[:, COL_B + lo:COL_B + lo + CONV_GROUP_WIDTH].astype(F32)
         * (w[2:3, :] * h + w[1:2, :] * h1 + w[0:1, :] * h2))
    ms = jnp.mean(y * y, axis=-1, keepdims=True)
    y_ref[:, sl] = (y * lax.rsqrt(ms + RMS_EPS) * g_ref[:, sl]).astype(y_ref.dtype)


def _chunk_cumsum(x):
    t_blk, width = x.shape
    x3 = x.reshape(t_blk // SUBLANES, SUBLANES, width)
    pos = lax.broadcasted_iota(jnp.int32, x3.shape, 1)
    shift = 1
    while shift < SUBLANES:
        x3 = x3 + jnp.where(pos >= shift, pltpu.roll(x3, shift, axis=1), 0.0)
        shift *= 2
    vregs_per_chunk = CHUNK // SUBLANES
    x4 = x3.reshape(t_blk // CHUNK, vregs_per_chunk, SUBLANES, width)
    outs = [x4[:, 0]]
    for j in range(1, vregs_per_chunk):
        outs.append(x4[:, j] + outs[-1][:, SUBLANES - 1:SUBLANES, :])
    return jnp.stack(outs, axis=1).reshape(t_blk, width)


def _gla_prepare(head, proj_ref, z_ref, gb_ref, qd_ref, ki_ref, ke_ref, dct_ref):
    t_blk = proj_ref.shape[0]
    n_chunks = t_blk // CHUNK
    col_q, col_k = COL_Q + head * HEAD_K, COL_K + head * HEAD_K
    z = z_ref[:, head * HEAD_K:(head + 1) * HEAD_K] + gb_ref[:, head * HEAD_K:(head + 1) * HEAD_K]
    log_sig = jnp.minimum(z, 0.0) - jnp.log(1.0 + jnp.exp(-jnp.abs(z)))
    bcum = _chunk_cumsum(log_sig * (1.0 / GATE_TAU))
    qd_ref[head] = ((proj_ref[:, col_q:col_q + HEAD_K].astype(F32) * (HEAD_K ** -0.5))
                    * jnp.exp(bcum)).astype(BF16)
    k_inv_f = proj_ref[:, col_k:col_k + HEAD_K].astype(F32) * jnp.exp(-bcum)
    ki_ref[head] = k_inv_f.astype(BF16)
    decay = jnp.exp(bcum.reshape(n_chunks, CHUNK, HEAD_K)[:, CHUNK - 1:CHUNK, :])
    ke_ref[head] = (k_inv_f.reshape(n_chunks, CHUNK, HEAD_K) * decay).astype(BF16).reshape(
        t_blk, HEAD_K)
    pad = jnp.zeros((HEAD_K - n_chunks, HEAD_K), F32)
    dct_ref[head] = jnp.transpose(jnp.concatenate([decay.reshape(n_chunks, HEAD_K), pad], axis=0))


def _gla_scores(head, c, qd_ref, ki_ref, sc_ref):
    rows = slice(c * CHUNK, (c + 1) * CHUNK)
    causal = (lax.broadcasted_iota(jnp.int32, (CHUNK, CHUNK), 0)
              >= lax.broadcasted_iota(jnp.int32, (CHUNK, CHUNK), 1))
    scores = lax.dot_general(qd_ref[head, rows, :], ki_ref[head, rows, :], _NT_DIMS,
                             preferred_element_type=F32)
    sc_ref[head, rows, :] = jnp.where(causal, scores, 0.0).astype(sc_ref.dtype)


def _gla_state_step(head, c, proj_ref, ke_ref, dct_ref, state_ref, st_ref):
    rows = slice(c * CHUNK, (c + 1) * CHUNK)
    col_v = COL_V + head * HEAD_V
    state = state_ref[head]
    st_ref[head, c] = state.astype(st_ref.dtype)
    delta = lax.dot_general(ke_ref[head, rows, :], proj_ref[rows, col_v:col_v + HEAD_V], _TN_DIMS,
                            preferred_element_type=F32)
    state_ref[head] = dct_ref[head, :, c:c + 1] * state + delta


def _gla_output(head, c, proj_ref, g_ref, qd_ref, sc_ref, st_ref, y_ref):
    rows = slice(c * CHUNK, (c + 1) * CHUNK)
    col_v, col_r = COL_V + head * HEAD_V, COL_R + head * HEAD_V
    col_y = D_CONV + head * HEAD_V
    o = (jnp.dot(sc_ref[head, rows, :], proj_ref[rows, col_v:col_v + HEAD_V],
                 preferred_element_type=F32)
         + jnp.dot(qd_ref[head, rows, :], st_ref[head, c], preferred_element_type=F32))
    ms = jnp.mean(o * o, axis=-1, keepdims=True)
    o_n = o * lax.rsqrt(ms + RMS_EPS) * g_ref[:, head * HEAD_V:(head + 1) * HEAD_V]
    r_c = proj_ref[rows, col_r:col_r + HEAD_V].astype(F32)
    y_ref[rows, col_y:col_y + HEAD_V] = (o_n * (r_c * jax.nn.sigmoid(r_c))).astype(y_ref.dtype)


def _mixer_kernel(proj_ref, zl_ref, x_ref, wo_ref, cw_ref, cg_ref, wg_ref, gb_ref, gg_ref,
                  ln_g_ref, ln_b_ref, o_ref, y_ref, yp_ref, mix_ref, state_ref,
                  carry_ref, qd_ref, ki_ref, ke_ref, dct_ref, z_ref, sc_ref, st_ref, *, blocks_per_seq):
    s = pl.program_id(0)

    @pl.when(s == 0)
    def _():
        y_ref[...] = jnp.zeros_like(y_ref)

    @pl.when(s % blocks_per_seq == 0)
    def _():
        state_ref[...] = jnp.zeros_like(state_ref)
        carry_ref[...] = jnp.zeros_like(carry_ref)

    yp_ref[...] = y_ref[...]

    gla_refs = (qd_ref, ki_ref, ke_ref, dct_ref)
    n_chunks = proj_ref.shape[0] // CHUNK
    vector_pieces = [functools.partial(_conv_piece, grp, proj_ref, cw_ref, cg_ref, carry_ref, y_ref)
                     for grp in range(CONV_GROUPS)]
    vector_pieces += [functools.partial(_gla_prepare, head, proj_ref, z_ref, gb_ref, *gla_refs)
                      for head in range(GLA_HEADS)]
    head_chunks = [(head, c) for c in range(n_chunks) for head in range(GLA_HEADS)]
    chunk_pieces = (
        [functools.partial(_gla_scores, h, c, qd_ref, ki_ref, sc_ref) for h, c in head_chunks]
        + [functools.partial(_gla_state_step, h, c, proj_ref, ke_ref, dct_ref, state_ref, st_ref)
           for h, c in head_chunks]
        + [functools.partial(_gla_output, h, c, proj_ref, gg_ref, qd_ref, sc_ref, st_ref, y_ref)
           for h, c in head_chunks])

    def gate_preactivation():
        z_ref[...] = jnp.dot(zl_ref[...], wg_ref[...], preferred_element_type=F32)

    def out_proj_tile(sub, n):
        rows = slice(sub * ROW_SUB, (sub + 1) * ROW_SUB)
        cols = slice(n * OUT_TN, (n + 1) * OUT_TN)
        mix_ref[rows, cols] = jnp.dot(yp_ref[rows, :], wo_ref[:, cols], preferred_element_type=F32)

    def layer_norm_rows(row0, n_rows):
        rows = slice(row0, row0 + n_rows)
        o_ref[rows, :] = _layer_norm(DN_ALPHA * x_ref[rows, :] + mix_ref[rows, :],
                                     ln_g_ref[...], ln_b_ref[...]).astype(o_ref.dtype)

    tiles_per_sub = D_MODEL // OUT_TN
    n_tiles = (o_ref.shape[0] // ROW_SUB) * tiles_per_sub
    ln_rows = ROW_SUB // 2
    ln_queue = []
    pieces = [gate_preactivation]
    for t in range(n_tiles):
        sub, n = divmod(t, tiles_per_sub)
        lo = len(vector_pieces) * t // n_tiles
        hi = len(vector_pieces) * (t + 1) // n_tiles
        pieces += vector_pieces[lo:hi] + ln_queue[:1] + [functools.partial(out_proj_tile, sub, n)]
        del ln_queue[:1]
        if n == tiles_per_sub - 1:
            ln_queue += [functools.partial(layer_norm_rows, sub * ROW_SUB + r, ln_rows)
                         for r in range(0, ROW_SUB, ln_rows)]
    for piece in pieces + chunk_pieces + ln_queue:
        piece()


def _mixer(proj, z_low, x2, w_out_b, conv_w8, conv_g, wg_b, gate_bias, gla_g, ln_g, ln_b, bsz, seq):
    m = x2.shape[0]
    n_blk = m // MIX_T
    cur_blk = lambda s: (jnp.minimum(s, n_blk - 1), 0)
    prev_blk = lambda s: (jnp.maximum(s - 1, 0), 0)
    const2 = lambda s: (0, 0)
    return pl.pallas_call(
        functools.partial(_mixer_kernel, blocks_per_seq=seq // MIX_T),
        grid=(n_blk + 1,),
        in_specs=[pl.BlockSpec((MIX_T, D_PROJ_MAIN), cur_blk),
                  pl.BlockSpec((MIX_T, GATE_RANK), cur_blk),
                  pl.BlockSpec((MIX_T, D_MODEL), prev_blk),
                  pl.BlockSpec((D_CONV + D_GLA_V, D_MODEL), const2, pipeline_mode=pl.Buffered(1)),
                  pl.BlockSpec((8, D_CONV), const2),
                  pl.BlockSpec((1, D_CONV), const2),
                  pl.BlockSpec((GATE_RANK, D_GLA_K), const2),
                  pl.BlockSpec((1, D_GLA_K), const2),
                  pl.BlockSpec((1, D_GLA_V), const2),
                  pl.BlockSpec((1, D_MODEL), const2),
                  pl.BlockSpec((1, D_MODEL), const2)],
        out_specs=pl.BlockSpec((MIX_T, D_MODEL), prev_blk),
        out_shape=jax.ShapeDtypeStruct((m, D_MODEL), BF16),
        scratch_shapes=[pltpu.VMEM((MIX_T, D_CONV + D_GLA_V), BF16),
                        pltpu.VMEM((MIX_T, D_CONV + D_GLA_V), BF16),
                        pltpu.VMEM((MIX_T, D_MODEL), F32),
                        pltpu.VMEM((GLA_HEADS, HEAD_K, HEAD_V), F32),
                        pltpu.VMEM((8, D_CONV), F32),
                        pltpu.VMEM((GLA_HEADS, MIX_T, HEAD_K), BF16),
                        pltpu.VMEM((GLA_HEADS, MIX_T, HEAD_K), BF16),
                        pltpu.VMEM((GLA_HEADS, MIX_T, HEAD_K), BF16),
                        pltpu.VMEM((GLA_HEADS, HEAD_K, HEAD_K), F32),
                        pltpu.VMEM((MIX_T, D_GLA_K), F32),
                        pltpu.VMEM((GLA_HEADS, MIX_T, CHUNK), BF16),
                        pltpu.VMEM((GLA_HEADS, MIX_T // CHUNK, HEAD_K, HEAD_V), BF16)],
        compiler_params=pltpu.CompilerParams(
            dimension_semantics=("arbitrary",), vmem_limit_bytes=VMEM_LIMIT),
        name="mixer",
    )(proj, z_low, x2, w_out_b, conv_w8, conv_g, wg_b, gate_bias, gla_g, ln_g, ln_b)


def _ffn_body(f, last, x_ref, up_tile, down_tile, g_ref, b_ref, o_ref):
    def hidden():
        h = jnp.dot(x_ref[...], up_tile(), preferred_element_type=F32)
        h = jnp.maximum(h, 0.0)
        return (h * h).astype(BF16)

    @pl.when(f == 0)
    def _():
        o_ref[...] = jnp.dot(hidden(), down_tile(), preferred_element_type=F32)

    @pl.when(jnp.logical_and(f > 0, f < last))
    def _():
        o_ref[...] += jnp.dot(hidden(), down_tile(), preferred_element_type=F32)

    @pl.when(f == last)
    def _():
        h = hidden()
        wd = down_tile()
        for s in range(o_ref.shape[0] // ROW_SUB):
            rows = slice(s * ROW_SUB, (s + 1) * ROW_SUB)
            ff = o_ref[rows, :] + jnp.dot(h[rows, :], wd, preferred_element_type=F32)
            o_ref[rows, :] = _layer_norm(DN_ALPHA * x_ref[rows, :].astype(F32) + ff,
                                         g_ref[...], b_ref[...])


def _ffn_head_kernel(x_ref, wu_ref, wd_ref, g_ref, b_ref, o_ref, wu_b_ref, wd_b_ref):
    def up_tile():
        wu_b_ref[...] = wu_ref[...].astype(BF16)
        return wu_b_ref[...]

    def down_tile():
        wd_b_ref[...] = wd_ref[...].astype(BF16)
        return wd_b_ref[...]

    _ffn_body(pl.program_id(0), pl.num_programs(0) - 1, x_ref, up_tile, down_tile, g_ref, b_ref, o_ref)


def _ffn_rest_kernel(x_ref, wu_ref, wd_ref, g_ref, b_ref, out_hbm_ref, o_ref):
    del out_hbm_ref
    _ffn_body(pl.program_id(1), pl.num_programs(1) - 1, x_ref, lambda: wu_ref[...],
              lambda: wd_ref[...], g_ref, b_ref, o_ref)


def _ffn_ln2(x1, w_ff_up, w_ff_down, ln_g, ln_b, layer):
    m = x1.shape[0]
    out_shape = jax.ShapeDtypeStruct((m, D_MODEL), F32)

    out, w_up_b, w_down_b = pl.pallas_call(
        _ffn_head_kernel,
        grid=(D_FF // FFN_HEAD_TF,),
        in_specs=[pl.BlockSpec((FFN_HEAD_TM, D_MODEL), lambda f: (0, 0), pipeline_mode=pl.Buffered(1)),
                  pl.BlockSpec((None, D_MODEL, FFN_HEAD_TF), lambda f: (layer, 0, f)),
                  pl.BlockSpec((None, FFN_HEAD_TF, D_MODEL), lambda f: (layer, f, 0)),
                  pl.BlockSpec((1, D_MODEL), lambda f: (0, 0)),
                  pl.BlockSpec((1, D_MODEL), lambda f: (0, 0))],
        out_specs=[pl.BlockSpec((FFN_HEAD_TM, D_MODEL), lambda f: (0, 0)),
                   pl.BlockSpec((D_MODEL, FFN_HEAD_TF), lambda f: (0, f)),
                   pl.BlockSpec((FFN_HEAD_TF, D_MODEL), lambda f: (f, 0))],
        out_shape=[out_shape,
                   jax.ShapeDtypeStruct((D_MODEL, D_FF), BF16),
                   jax.ShapeDtypeStruct((D_FF, D_MODEL), BF16)],
        compiler_params=pltpu.CompilerParams(
            dimension_semantics=("arbitrary",), vmem_limit_bytes=VMEM_LIMIT),
        name="ffn_head",
    )(x1, w_ff_up, w_ff_down, ln_g, ln_b)

    i0 = FFN_HEAD_TM // FFN_TM
    return pl.pallas_call(
        _ffn_rest_kernel,
        grid=(m // FFN_TM - i0, D_FF // FFN_TF),
        in_specs=[pl.BlockSpec((FFN_TM, D_MODEL), lambda i, f: (i + i0, 0)),
                  pl.BlockSpec((D_MODEL, FFN_TF), lambda i, f: (0, f)),
                  pl.BlockSpec((FFN_TF, D_MODEL), lambda i, f: (f, 0)),
                  pl.BlockSpec((1, D_MODEL), lambda i, f: (0, 0)),
                  pl.BlockSpec((1, D_MODEL), lambda i, f: (0, 0)),
                  pl.BlockSpec(memory_space=pl.ANY)],
        out_specs=pl.BlockSpec((FFN_TM, D_MODEL), lambda i, f: (i + i0, 0)),
        out_shape=out_shape,
        input_output_aliases={5: 0},
        compiler_params=pltpu.CompilerParams(
            dimension_semantics=("arbitrary", "arbitrary"), vmem_limit_bytes=VMEM_LIMIT),
        name="ffn_rest",
    )(x1, w_up_b, w_down_b, ln_g, ln_b, out)


def kernel(x, w_in, conv_w, conv_norm_g, w_gate_up, gate_bias, gla_norm_g, w_out,
           ln1_g, ln1_b, w_ff_up, w_ff_down, ln2_g, ln2_b):
    bsz, seq, _ = x.shape
    assert seq % MIX_T == 0 and MIX_T % CHUNK == 0 and MIX_T % ROW_SUB == 0
    x2 = x.reshape(bsz * seq, D_MODEL)
    w_in_t = jnp.swapaxes(w_in, 1, 2)
    for l in range(DEPTH):
        wg_b = w_gate_up[l].astype(BF16)
        gb = gate_bias[l].reshape(1, D_GLA_K)
        conv_w8 = jnp.pad(conv_w[l], ((0, 8 - conv_w.shape[1]), (0, 0)))

        proj, z_low, w_out_b = _in_proj(x2, w_in_t, w_out, l)
        x1 = _mixer(proj, z_low, x2, w_out_b, conv_w8, conv_norm_g[l].reshape(1, D_CONV),
                    wg_b, gb, gla_norm_g[l].reshape(1, D_GLA_V),
                    ln1_g[l].reshape(1, D_MODEL), ln1_b[l].reshape(1, D_MODEL), bsz, seq)
        x2 = _ffn_ln2(x1, w_ff_up, w_ff_down,
                      ln2_g[l].reshape(1, D_MODEL), ln2_b[l].reshape(1, D_MODEL), l)
    return x2.reshape(bsz, seq, D_MODEL)
```

```python
import functools

import jax
import jax.numpy as jnp
from jax import lax
from jax.experimental import pallas as pl
from jax.experimental.pallas import tpu as pltpu

F32 = jnp.float32
BF16 = jnp.bfloat16

D_MODEL = 2048
D_CONV = 1024
CONV_GROUPS = 8
CONV_GROUP_WIDTH = D_CONV // CONV_GROUPS
GLA_HEADS = 4
HEAD_K = 128
HEAD_V = 256
D_GLA_K = GLA_HEADS * HEAD_K
D_GLA_V = GLA_HEADS * HEAD_V
GATE_RANK = 16
GATE_TAU = 16.0
CHUNK = 64
D_FF = 4 * D_MODEL
LN_EPS = 1e-5
RMS_EPS = 1e-6
DEPTH = 1
DN_ALPHA = (2.0 * DEPTH) ** 0.25

D_PROJ_MAIN = 3 * D_CONV + 2 * D_GLA_K + 2 * D_GLA_V

COL_B, COL_C, COL_U = 0, D_CONV, 2 * D_CONV
COL_Q = 3 * D_CONV
COL_K = COL_Q + D_GLA_K
COL_V = COL_K + D_GLA_K
COL_R = COL_V + D_GLA_V

VMEM_LIMIT = 60 * 1024 * 1024
SUBLANES = 8

PROJ_TM = 1024
PROJ_HEAD_TM = 2048
PROJ_HEAD_TN = 512
PROJ_TN = 1536
CAST_SLABS = 16
MIX_T = 512
ROW_SUB = 256
OUT_TN = 512
FFN_HEAD_TM = 1024
FFN_HEAD_TF = 512
FFN_TM = 512
FFN_TF = 2048

_NT_DIMS = (((1,), (1,)), ((), ()))
_TN_DIMS = (((0,), (0,)), ((), ()))


def _layer_norm(y, g, b):
    mu = jnp.mean(y, axis=-1, keepdims=True)
    yc = y - mu
    var = jnp.mean(yc * yc, axis=-1, keepdims=True)
    return yc * lax.rsqrt(var + LN_EPS) * g + b


def _in_proj_head_kernel(x_ref, w_ref, wz_ref, o_ref, z_ref, wb_ref, wzb_ref, xb_ref):
    @pl.when(pl.program_id(0) == 0)
    def _():
        xb_ref[...] = x_ref[...].astype(BF16)
        wzb_ref[...] = wz_ref[...].astype(BF16)
        z_ref[...] = lax.dot_general(xb_ref[...], wzb_ref[...], _NT_DIMS,
                                     preferred_element_type=F32).astype(z_ref.dtype)

    wb_ref[...] = w_ref[...].astype(BF16)
    o_ref[...] = lax.dot_general(xb_ref[...], wb_ref[...], _NT_DIMS,
                                 preferred_element_type=F32).astype(o_ref.dtype)


def _in_proj_rest_kernel(x_ref, wb_ref, wzb_ref, pa_ref, za_ref, wo_ref,
                         o_ref, z_ref, wo_b_ref, xb_ref, *, head_tiles):
    i, j = pl.program_id(0), pl.program_id(1)

    wo_b_ref[...] = wo_ref[...].astype(BF16)

    @pl.when(i < head_tiles)
    def _():
        o_ref[...] = pa_ref[...]

        @pl.when(j == 0)
        def _():
            z_ref[...] = za_ref[...]

    @pl.when(i >= head_tiles)
    def _():
        @pl.when(j == 0)
        def _():
            xb_ref[...] = x_ref[...].astype(BF16)
            z_ref[...] = lax.dot_general(xb_ref[...], wzb_ref[...], _NT_DIMS,
                                         preferred_element_type=F32).astype(z_ref.dtype)

        o_ref[...] = lax.dot_general(xb_ref[...], wb_ref[...], _NT_DIMS,
                                     preferred_element_type=F32).astype(o_ref.dtype)


def _in_proj(x2, w_in_t, w_out, layer):
    m = x2.shape[0]
    i0 = PROJ_HEAD_TM // PROJ_TM
    n_i = m // PROJ_TM

    n_jh = D_PROJ_MAIN // PROJ_HEAD_TN
    proj_a, z_a, w_in_b, wz_b = pl.pallas_call(
        _in_proj_head_kernel,
        grid=(n_jh,),
        in_specs=[pl.BlockSpec((PROJ_HEAD_TM, D_MODEL), lambda j: (0, 0), pipeline_mode=pl.Buffered(1)),
                  pl.BlockSpec((None, PROJ_HEAD_TN, D_MODEL), lambda j: (layer, j, 0)),
                  pl.BlockSpec((None, GATE_RANK, D_MODEL),
                               lambda j: (layer, D_PROJ_MAIN // GATE_RANK, 0))],
        out_specs=[pl.BlockSpec((PROJ_HEAD_TM, PROJ_HEAD_TN), lambda j: (0, j)),
                   pl.BlockSpec((PROJ_HEAD_TM, GATE_RANK), lambda j: (0, 0)),
                   pl.BlockSpec((PROJ_HEAD_TN, D_MODEL), lambda j: (j, 0)),
                   pl.BlockSpec((GATE_RANK, D_MODEL), lambda j: (0, 0))],
        out_shape=[jax.ShapeDtypeStruct((PROJ_HEAD_TM, D_PROJ_MAIN), BF16),
                   jax.ShapeDtypeStruct((PROJ_HEAD_TM, GATE_RANK), BF16),
                   jax.ShapeDtypeStruct((D_PROJ_MAIN, D_MODEL), BF16),
                   jax.ShapeDtypeStruct((GATE_RANK, D_MODEL), BF16)],
        scratch_shapes=[pltpu.VMEM((PROJ_HEAD_TM, D_MODEL), BF16)],
        compiler_params=pltpu.CompilerParams(
            dimension_semantics=("arbitrary",), vmem_limit_bytes=VMEM_LIMIT),
        name="in_proj_head",
    )(x2, w_in_t, w_in_t)

    n_j = D_PROJ_MAIN // PROJ_TN
    slab_rows = D_MODEL // CAST_SLABS
    slab = lambda i, j: jnp.minimum(i * n_j + j, CAST_SLABS - 1)
    copying = lambda i: i < i0
    return pl.pallas_call(
        functools.partial(_in_proj_rest_kernel, head_tiles=i0),
        grid=(n_i, n_j),
        in_specs=[pl.BlockSpec((PROJ_TM, D_MODEL), lambda i, j: (jnp.maximum(i, i0), 0)),
                  pl.BlockSpec((PROJ_TN, D_MODEL), lambda i, j: (jnp.where(copying(i), 0, j), 0)),
                  pl.BlockSpec((GATE_RANK, D_MODEL), lambda i, j: (0, 0)),
                  pl.BlockSpec((PROJ_TM, PROJ_TN),
                               lambda i, j: (jnp.minimum(i, i0 - 1), jnp.where(copying(i), j, n_j - 1))),
                  pl.BlockSpec((PROJ_TM, GATE_RANK), lambda i, j: (jnp.minimum(i, i0 - 1), 0)),
                  pl.BlockSpec((None, slab_rows, D_MODEL), lambda i, j: (layer, slab(i, j), 0))],
        out_specs=[pl.BlockSpec((PROJ_TM, PROJ_TN), lambda i, j: (i, j)),
                   pl.BlockSpec((PROJ_TM, GATE_RANK), lambda i, j: (i, 0)),
                   pl.BlockSpec((slab_rows, D_MODEL), lambda i, j: (slab(i, j), 0))],
        out_shape=[jax.ShapeDtypeStruct((m, D_PROJ_MAIN), BF16),
                   jax.ShapeDtypeStruct((m, GATE_RANK), BF16),
                   jax.ShapeDtypeStruct((D_MODEL, D_MODEL), BF16)],
        scratch_shapes=[pltpu.VMEM((PROJ_TM, D_MODEL), BF16)],
        compiler_params=pltpu.CompilerParams(
            dimension_semantics=("arbitrary", "arbitrary"), vmem_limit_bytes=VMEM_LIMIT),
        name="in_proj_rest",
    )(x2, w_in_b, wz_b, proj_a, z_a, w_out)


def _shift_rows(h3, carry3, shift):
    pos = lax.broadcasted_iota(jnp.int32, h3.shape, 1)
    rolled = pltpu.roll(h3, shift, axis=1)
    rolled_prev = jnp.concatenate([pltpu.roll(carry3, shift, axis=1), rolled[:-1]], axis=0)
    return jnp.where(pos < shift, rolled_prev, rolled)


def _conv_piece(grp, proj_ref, w_ref, g_ref, carry_ref, y_ref):
    ts = proj_ref.shape[0]
    lo = grp * CONV_GROUP_WIDTH
    sl = slice(lo, lo + CONV_GROUP_WIDTH)
    h = (proj_ref[:, COL_C + lo:COL_C + lo + CONV_GROUP_WIDTH].astype(F32)
         * proj_ref[:, COL_U + lo:COL_U + lo + CONV_GROUP_WIDTH].astype(F32))
    h3 = h.reshape(ts // SUBLANES, SUBLANES, CONV_GROUP_WIDTH)
    carry3 = carry_ref[:, sl].reshape(1, SUBLANES, CONV_GROUP_WIDTH)
    h1 = _shift_rows(h3, carry3, 1).reshape(h.shape)
    h2 = _shift_rows(h3, carry3, 2).reshape(h.shape)
    carry_ref[:, sl] = h[ts - SUBLANES:, :]
    w = w_ref[:, sl]
    y = (proj_ref[:, COL_B + lo:COL_B + lo + CONV_GROUP_WIDTH].astype(F32)
         * (w[2:3, :] * h + w[1:2, :] * h1 + w[0:1, :] * h2))
    ms = jnp.mean(y * y, axis=-1, keepdims=True)
    y_ref[:, sl] = (y * lax.rsqrt(ms + RMS_EPS) * g_ref[:, sl]).astype(y_ref.dtype)


def _chunk_cumsum(x):
    t_blk, width = x.shape
    x3 = x.reshape(t_blk // SUBLANES, SUBLANES, width)
    pos = lax.broadcasted_iota(jnp.int32, x3.shape, 1)
    shift = 1
    while shift < SUBLANES:
        x3 = x3 + jnp.where(pos >= shift, pltpu.roll(x3, shift, axis=1), 0.0)
        shift *= 2
    vregs_per_chunk = CHUNK // SUBLANES
    x4 = x3.reshape(t_blk // CHUNK, vregs_per_chunk, SUBLANES, width)
    outs = [x4[:, 0]]
    for j in range(1, vregs_per_chunk):
        outs.append(x4[:, j] + outs[-1][:, SUBLANES - 1:SUBLANES, :])
    return jnp.stack(outs, axis=1).reshape(t_blk, width)


def _gla_prepare(head, proj_ref, z_ref, gb_ref, qd_ref, ki_ref, ke_ref, dct_ref):
    t_blk = proj_ref.shape[0]
    n_chunks = t_blk // CHUNK
    col_q, col_k = COL_Q + head * HEAD_K, COL_K + head * HEAD_K
    z = z_ref[:, head * HEAD_K:(head + 1) * HEAD_K] + gb_ref[:, head * HEAD_K:(head + 1) * HEAD_K]
    log_sig = jnp.minimum(z, 0.0) - jnp.log(1.0 + jnp.exp(-jnp.abs(z)))
    bcum = _chunk_cumsum(log_sig * (1.0 / GATE_TAU))
    qd_ref[head] = ((proj_ref[:, col_q:col_q + HEAD_K].astype(F32) * (HEAD_K ** -0.5))
                    * jnp.exp(bcum)).astype(BF16)
    k_inv_f = proj_ref[:, col_k:col_k + HEAD_K].astype(F32) * jnp.exp(-bcum)
    ki_ref[head] = k_inv_f.astype(BF16)
    decay = jnp.exp(bcum.reshape(n_chunks, CHUNK, HEAD_K)[:, CHUNK - 1:CHUNK, :])
    ke_ref[head] = (k_inv_f.reshape(n_chunks, CHUNK, HEAD_K) * decay).astype(BF16).reshape(
        t_blk, HEAD_K)
    pad = jnp.zeros((HEAD_K - n_chunks, HEAD_K), F32)
    dct_ref[head] = jnp.transpose(jnp.concatenate([decay.reshape(n_chunks, HEAD_K), pad], axis=0))


def _gla_scores(head, c, qd_ref, ki_ref, sc_ref):
    rows = slice(c * CHUNK, (c + 1) * CHUNK)
    causal = (lax.broadcasted_iota(jnp.int32, (CHUNK, CHUNK), 0)
              >= lax.broadcasted_iota(jnp.int32, (CHUNK, CHUNK), 1))
    scores = lax.dot_general(qd_ref[head, rows, :], ki_ref[head, rows, :], _NT_DIMS,
                             preferred_element_type=F32)
    sc_ref[head, rows, :] = jnp.where(causal, scores, 0.0).astype(sc_ref.dtype)


def _gla_state_step(head, c, proj_ref, ke_ref, dct_ref, state_ref, st_ref):
    rows = slice(c * CHUNK, (c + 1) * CHUNK)
    col_v = COL_V + head * HEAD_V
    state = state_ref[head]
    st_ref[head, c] = state.astype(st_ref.dtype)
    delta = lax.dot_general(ke_ref[head, rows, :], proj_ref[rows, col_v:col_v + HEAD_V], _TN_DIMS,
                            preferred_element_type=F32)
    state_ref[head] = dct_ref[head, :, c:c + 1] * state + delta


def _gla_output(head, c, proj_ref, g_ref, qd_ref, sc_ref, st_ref, y_ref):
    rows = slice(c * CHUNK, (c + 1) * CHUNK)
    col_v, col_r = COL_V + head * HEAD_V, COL_R + head * HEAD_V
    col_y = D_CONV + head * HEAD_V
    o = (jnp.dot(sc_ref[head, rows, :], proj_ref[rows, col_v:col_v + HEAD_V],
                 preferred_element_type=F32)
         + jnp.dot(qd_ref[head, rows, :], st_ref[head, c], preferred_element_type=F32))
    ms = jnp.mean(o * o, axis=-1, keepdims=True)
    o_n = o * lax.rsqrt(ms + RMS_EPS) * g_ref[:, head * HEAD_V:(head + 1) * HEAD_V]
    r_c = proj_ref[rows, col_r:col_r + HEAD_V].astype(F32)
    y_ref[rows, col_y:col_y + HEAD_V] = (o_n * (r_c * jax.nn.sigmoid(r_c))).astype(y_ref.dtype)


def _mixer_kernel(proj_ref, zl_ref, x_ref, wo_ref, cw_ref, cg_ref, wg_ref, gb_ref, gg_ref,
                  ln_g_ref, ln_b_ref, o_ref, y_ref, yp_ref, mix_ref, state_ref,
                  carry_ref, qd_ref, ki_ref, ke_ref, dct_ref, z_ref, sc_ref, st_ref, *, blocks_per_seq):
    s = pl.program_id(0)

    @pl.when(s == 0)
    def _():
        y_ref[...] = jnp.zeros_like(y_ref)

    @pl.when(s % blocks_per_seq == 0)
    def _():
        state_ref[...] = jnp.zeros_like(state_ref)
        carry_ref[...] = jnp.zeros_like(carry_ref)

    yp_ref[...] = y_ref[...]

    gla_refs = (qd_ref, ki_ref, ke_ref, dct_ref)
    n_chunks = proj_ref.shape[0] // CHUNK
    vector_pieces = [functools.partial(_conv_piece, grp, proj_ref, cw_ref, cg_ref, carry_ref, y_ref)
                     for grp in range(CONV_GROUPS)]
    vector_pieces += [functools.partial(_gla_prepare, head, proj_ref, z_ref, gb_ref, *gla_refs)
                      for head in range(GLA_HEADS)]
    head_chunks = [(head, c) for c in range(n_chunks) for head in range(GLA_HEADS)]
    chunk_pieces = (
        [functools.partial(_gla_scores, h, c, qd_ref, ki_ref, sc_ref) for h, c in head_chunks]
        + [functools.partial(_gla_state_step, h, c, proj_ref, ke_ref, dct_ref, state_ref, st_ref)
           for h, c in head_chunks]
        + [functools.partial(_gla_output, h, c, proj_ref, gg_ref, qd_ref, sc_ref, st_ref, y_ref)
           for h, c in head_chunks])

    def gate_preactivation():
        z_ref[...] = jnp.dot(zl_ref[...], wg_ref[...], preferred_element_type=F32)

    def out_proj_tile(sub, n):
        rows = slice(sub * ROW_SUB, (sub + 1) * ROW_SUB)
        cols = slice(n * OUT_TN, (n + 1) * OUT_TN)
        mix_ref[rows, cols] = jnp.dot(yp_ref[rows, :], wo_ref[:, cols], preferred_element_type=F32)

    def layer_norm_rows(row0, n_rows):
        rows = slice(row0, row0 + n_rows)
        o_ref[rows, :] = _layer_norm(DN_ALPHA * x_ref[rows, :] + mix_ref[rows, :],
                                     ln_g_ref[...], ln_b_ref[...]).astype(o_ref.dtype)

    tiles_per_sub = D_MODEL // OUT_TN
    n_tiles = (o_ref.shape[0] // ROW_SUB) * tiles_per_sub
    ln_rows = ROW_SUB // 2
    ln_queue = []
    pieces = [gate_preactivation]
    for t in range(n_tiles):
        sub, n = divmod(t, tiles_per_sub)
        lo = len(vector_pieces) * t // n_tiles
        hi = len(vector_pieces) * (t + 1) // n_tiles
        pieces += vector_pieces[lo:hi] + ln_queue[:1] + [functools.partial(out_proj_tile, sub, n)]
        del ln_queue[:1]
        if n == tiles_per_sub - 1:
            ln_queue += [functools.partial(layer_norm_rows, sub * ROW_SUB + r, ln_rows)
                         for r in range(0, ROW_SUB, ln_rows)]
    for piece in pieces + chunk_pieces + ln_queue:
        piece()


def _mixer(proj, z_low, x2, w_out_b, conv_w8, conv_g, wg_b, gate_bias, gla_g, ln_g, ln_b, bsz, seq):
    m = x2.shape[0]
    n_blk = m // MIX_T
    cur_blk = lambda s: (jnp.minimum(s, n_blk - 1), 0)
    prev_blk = lambda s: (jnp.maximum(s - 1, 0), 0)
    const2 = lambda s: (0, 0)
    return pl.pallas_call(
        functools.partial(_mixer_kernel, blocks_per_seq=seq // MIX_T),
        grid=(n_blk + 1,),
        in_specs=[pl.BlockSpec((MIX_T, D_PROJ_MAIN), cur_blk),
                  pl.BlockSpec((MIX_T, GATE_RANK), cur_blk),
                  pl.BlockSpec((MIX_T, D_MODEL), prev_blk),
                  pl.BlockSpec((D_CONV + D_GLA_V, D_MODEL), const2, pipeline_mode=pl.Buffered(1)),
                  pl.BlockSpec((8, D_CONV), const2),
                  pl.BlockSpec((1, D_CONV), const2),
                  pl.BlockSpec((GATE_RANK, D_GLA_K), const2),
                  pl.BlockSpec((1, D_GLA_K), const2),
                  pl.BlockSpec((1, D_GLA_V), const2),
                  pl.BlockSpec((1, D_MODEL), const2),
                  pl.BlockSpec((1, D_MODEL), const2)],
        out_specs=pl.BlockSpec((MIX_T, D_MODEL), prev_blk),
        out_shape=jax.ShapeDtypeStruct((m, D_MODEL), BF16),
        scratch_shapes=[pltpu.VMEM((MIX_T, D_CONV + D_GLA_V), BF16),
                        pltpu.VMEM((MIX_T, D_CONV + D_GLA_V), BF16),
                        pltpu.VMEM((MIX_T, D_MODEL), F32),
                        pltpu.VMEM((GLA_HEADS, HEAD_K, HEAD_V), F32),
                        pltpu.VMEM((8, D_CONV), F32),
                        pltpu.VMEM((GLA_HEADS, MIX_T, HEAD_K), BF16),
                        pltpu.VMEM((GLA_HEADS, MIX_T, HEAD_K), BF16),
                        pltpu.VMEM((GLA_HEADS, MIX_T, HEAD_K), BF16),
                        pltpu.VMEM((GLA_HEADS, HEAD_K, HEAD_K), F32),
                        pltpu.VMEM((MIX_T, D_GLA_K), F32),
                        pltpu.VMEM((GLA_HEADS, MIX_T, CHUNK), BF16),
                        pltpu.VMEM((GLA_HEADS, MIX_T // CHUNK, HEAD_K, HEAD_V), BF16)],
        compiler_params=pltpu.CompilerParams(
            dimension_semantics=("arbitrary",), vmem_limit_bytes=VMEM_LIMIT),
        name="mixer",
    )(proj, z_low, x2, w_out_b, conv_w8, conv_g, wg_b, gate_bias, gla_g, ln_g, ln_b)


def _ffn_body(f, last, x_ref, up_tile, down_tile, g_ref, b_ref, o_ref):
    def hidden():
        h = jnp.dot(x_ref[...], up_tile(), preferred_element_type=F32)
        h = jnp.maximum(h, 0.0)
        return (h * h).astype(BF16)

    @pl.when(f == 0)
    def _():
        o_ref[...] = jnp.dot(hidden(), down_tile(), preferred_element_type=F32)

    @pl.when(jnp.logical_and(f > 0, f < last))
    def _():
        o_ref[...] += jnp.dot(hidden(), down_tile(), preferred_element_type=F32)

    @pl.when(f == last)
    def _():
        h = hidden()
        wd = down_tile()
        for s in range(o_ref.shape[0] // ROW_SUB):
            rows = slice(s * ROW_SUB, (s + 1) * ROW_SUB)
            ff = o_ref[rows, :] + jnp.dot(h[rows, :], wd, preferred_element_type=F32)
            o_ref[rows, :] = _layer_norm(DN_ALPHA * x_ref[rows, :].astype(F32) + ff,
                                         g_ref[...], b_ref[...])


def _ffn_head_kernel(x_ref, wu_ref, wd_ref, g_ref, b_ref, o_ref, wu_b_ref, wd_b_ref):
    def up_tile():
        wu_b_ref[...] = wu_ref[...].astype(BF16)
        return wu_b_ref[...]

    def down_tile():
        wd_b_ref[...] = wd_ref[...].astype(BF16)
        return wd_b_ref[...]

    _ffn_body(pl.program_id(0), pl.num_programs(0) - 1, x_ref, up_tile, down_tile, g_ref, b_ref, o_ref)


def _ffn_rest_kernel(x_ref, wu_ref, wd_ref, g_ref, b_ref, oa_ref, o_ref, *, head_tiles):
    i, f = pl.program_id(0), pl.program_id(1)
    last = pl.num_programs(1) - 1

    @pl.when(jnp.logical_and(i < head_tiles, f == last))
    def _():
        o_ref[...] = oa_ref[...]

    @pl.when(i >= head_tiles)
    def _():
        _ffn_body(f, last, x_ref, lambda: wu_ref[...], lambda: wd_ref[...], g_ref, b_ref, o_ref)


def _ffn_ln2(x1, w_ff_up, w_ff_down, ln_g, ln_b, layer):
    m = x1.shape[0]

    out_a, w_up_b, w_down_b = pl.pallas_call(
        _ffn_head_kernel,
        grid=(D_FF // FFN_HEAD_TF,),
        in_specs=[pl.BlockSpec((FFN_HEAD_TM, D_MODEL), lambda f: (0, 0), pipeline_mode=pl.Buffered(1)),
                  pl.BlockSpec((None, D_MODEL, FFN_HEAD_TF), lambda f: (layer, 0, f)),
                  pl.BlockSpec((None, FFN_HEAD_TF, D_MODEL), lambda f: (layer, f, 0)),
                  pl.BlockSpec((1, D_MODEL), lambda f: (0, 0)),
                  pl.BlockSpec((1, D_MODEL), lambda f: (0, 0))],
        out_specs=[pl.BlockSpec((FFN_HEAD_TM, D_MODEL), lambda f: (0, 0)),
                   pl.BlockSpec((D_MODEL, FFN_HEAD_TF), lambda f: (0, f)),
                   pl.BlockSpec((FFN_HEAD_TF, D_MODEL), lambda f: (f, 0))],
        out_shape=[jax.ShapeDtypeStruct((FFN_HEAD_TM, D_MODEL), F32),
                   jax.ShapeDtypeStruct((D_MODEL, D_FF), BF16),
                   jax.ShapeDtypeStruct((D_FF, D_MODEL), BF16)],
        compiler_params=pltpu.CompilerParams(
            dimension_semantics=("arbitrary",), vmem_limit_bytes=VMEM_LIMIT),
        name="ffn_head",
    )(x1, w_ff_up, w_ff_down, ln_g, ln_b)

    i0 = FFN_HEAD_TM // FFN_TM
    copying = lambda i: i < i0
    return pl.pallas_call(
        functools.partial(_ffn_rest_kernel, head_tiles=i0),
        grid=(m // FFN_TM, D_FF // FFN_TF),
        in_specs=[pl.BlockSpec((FFN_TM, D_MODEL), lambda i, f: (jnp.maximum(i, i0), 0)),
                  pl.BlockSpec((D_MODEL, FFN_TF), lambda i, f: (0, jnp.where(copying(i), 0, f))),
                  pl.BlockSpec((FFN_TF, D_MODEL), lambda i, f: (jnp.where(copying(i), 0, f), 0)),
                  pl.BlockSpec((1, D_MODEL), lambda i, f: (0, 0)),
                  pl.BlockSpec((1, D_MODEL), lambda i, f: (0, 0)),
                  pl.BlockSpec((FFN_TM, D_MODEL), lambda i, f: (jnp.minimum(i, i0 - 1), 0),
                               pipeline_mode=pl.Buffered(1))],
        out_specs=pl.BlockSpec((FFN_TM, D_MODEL), lambda i, f: (i, 0)),
        out_shape=jax.ShapeDtypeStruct((m, D_MODEL), F32),
        compiler_params=pltpu.CompilerParams(
            dimension_semantics=("arbitrary", "arbitrary"), vmem_limit_bytes=VMEM_LIMIT),
        name="ffn_rest",
    )(x1, w_up_b, w_down_b, ln_g, ln_b, out_a)


def kernel(x, w_in, conv_w, conv_norm_g, w_gate_up, gate_bias, gla_norm_g, w_out,
           ln1_g, ln1_b, w_ff_up, w_ff_down, ln2_g, ln2_b):
    bsz, seq, _ = x.shape
    assert seq % MIX_T == 0 and MIX_T % CHUNK == 0 and MIX_T % ROW_SUB == 0
    x2 = x.reshape(bsz * seq, D_MODEL)
    w_in_t = jnp.swapaxes(w_in, 1, 2)
    for l in range(DEPTH):
        wg_b = w_gate_up[l].astype(BF16)
        gb = gate_bias[l].reshape(1, D_GLA_K)
        conv_w8 = jnp.pad(conv_w[l], ((0, 8 - conv_w.shape[1]), (0, 0)))

        proj, z_low, w_out_b = _in_proj(x2, w_in_t, w_out, l)
        x1 = _mixer(proj, z_low, x2, w_out_b, conv_w8, conv_norm_g[l].reshape(1, D_CONV),
                    wg_b, gb, gla_norm_g[l].reshape(1, D_GLA_V),
                    ln1_g[l].reshape(1, D_MODEL), ln1_b[l].reshape(1, D_MODEL), bsz, seq)
        x2 = _ffn_ln2(x1, w_ff_up, w_ff_down,
                      ln2_g[l].reshape(1, D_MODEL), ln2_b[l].reshape(1, D_MODEL), l)
    return x2.reshape(bsz, seq, D_MODEL)
```

```python
import functools

import jax
import jax.numpy as jnp
from jax import lax
from jax.experimental import pallas as pl
from jax.experimental.pallas import tpu as pltpu

F32 = jnp.float32
BF16 = jnp.bfloat16

D_MODEL = 2048
D_CONV = 1024
CONV_GROUPS = 8
CONV_GROUP_WIDTH = D_CONV // CONV_GROUPS
GLA_HEADS = 4
HEAD_K = 128
HEAD_V = 256
D_GLA_K = GLA_HEADS * HEAD_K
D_GLA_V = GLA_HEADS * HEAD_V
GATE_RANK = 16
GATE_TAU = 16.0
CHUNK = 64
D_FF = 4 * D_MODEL
LN_EPS = 1e-5
RMS_EPS = 1e-6
DEPTH = 1
DN_ALPHA = (2.0 * DEPTH) ** 0.25

D_PROJ_MAIN = 3 * D_CONV + 2 * D_GLA_K + 2 * D_GLA_V

COL_B, COL_C, COL_U = 0, D_CONV, 2 * D_CONV
COL_Q = 3 * D_CONV
COL_K = COL_Q + D_GLA_K
COL_V = COL_K + D_GLA_K
COL_R = COL_V + D_GLA_V

VMEM_LIMIT = 60 * 1024 * 1024
SUBLANES = 8

PROJ_TM = 1024
PROJ_HEAD_TM = 1024
PROJ_HEAD_TN = 1024
PROJ_TN = 2048
CAST_SLABS = 16
MIX_T = 512
ROW_SUB = 256
OUT_TN = 512
FFN_HEAD_TM = 1024
FFN_HEAD_TF = 512
FFN_TM = 512
FFN_TF = 2048

_NT_DIMS = (((1,), (1,)), ((), ()))
_TN_DIMS = (((0,), (0,)), ((), ()))


def _layer_norm(y, g, b):
    mu = jnp.mean(y, axis=-1, keepdims=True)
    yc = y - mu
    var = jnp.mean(yc * yc, axis=-1, keepdims=True)
    return yc * lax.rsqrt(var + LN_EPS) * g + b


def _in_proj_head_kernel(x_ref, w_ref, wz_ref, o_ref, z_ref, wb_ref, wzb_ref, xb_ref):
    @pl.when(pl.program_id(0) == 0)
    def _():
        xb_ref[...] = x_ref[...].astype(BF16)
        wzb_ref[...] = wz_ref[...].astype(BF16)
        z_ref[...] = lax.dot_general(xb_ref[...], wzb_ref[...], _NT_DIMS,
                                     preferred_element_type=F32).astype(z_ref.dtype)

    wb_ref[...] = w_ref[...].astype(BF16)
    o_ref[...] = lax.dot_general(xb_ref[...], wb_ref[...], _NT_DIMS,
                                 preferred_element_type=F32).astype(o_ref.dtype)


def _in_proj_rest_kernel(x_ref, wb_ref, wzb_ref, pa_ref, za_ref, wo_ref,
                         o_ref, z_ref, wo_b_ref, xb_ref, *, head_tiles):
    i, j = pl.program_id(0), pl.program_id(1)

    wo_b_ref[...] = wo_ref[...].astype(BF16)

    @pl.when(i < head_tiles)
    def _():
        o_ref[...] = pa_ref[...]

        @pl.when(j == 0)
        def _():
            z_ref[...] = za_ref[...]

    @pl.when(i >= head_tiles)
    def _():
        @pl.when(j == 0)
        def _():
            xb_ref[...] = x_ref[...].astype(BF16)
            z_ref[...] = lax.dot_general(xb_ref[...], wzb_ref[...], _NT_DIMS,
                                         preferred_element_type=F32).astype(z_ref.dtype)

        o_ref[...] = lax.dot_general(xb_ref[...], wb_ref[...], _NT_DIMS,
                                     preferred_element_type=F32).astype(o_ref.dtype)


def _in_proj(x2, w_in_t, w_out, layer):
    m = x2.shape[0]
    i0 = PROJ_HEAD_TM // PROJ_TM
    n_i = m // PROJ_TM

    n_jh = D_PROJ_MAIN // PROJ_HEAD_TN
    proj_a, z_a, w_in_b, wz_b = pl.pallas_call(
        _in_proj_head_kernel,
        grid=(n_jh,),
        in_specs=[pl.BlockSpec((PROJ_HEAD_TM, D_MODEL), lambda j: (0, 0), pipeline_mode=pl.Buffered(1)),
                  pl.BlockSpec((None, PROJ_HEAD_TN, D_MODEL), lambda j: (layer, j, 0)),
                  pl.BlockSpec((None, GATE_RANK, D_MODEL),
                               lambda j: (layer, D_PROJ_MAIN // GATE_RANK, 0))],
        out_specs=[pl.BlockSpec((PROJ_HEAD_TM, PROJ_HEAD_TN), lambda j: (0, j)),
                   pl.BlockSpec((PROJ_HEAD_TM, GATE_RANK), lambda j: (0, 0)),
                   pl.BlockSpec((PROJ_HEAD_TN, D_MODEL), lambda j: (j, 0)),
                   pl.BlockSpec((GATE_RANK, D_MODEL), lambda j: (0, 0))],
        out_shape=[jax.ShapeDtypeStruct((PROJ_HEAD_TM, D_PROJ_MAIN), BF16),
                   jax.ShapeDtypeStruct((PROJ_HEAD_TM, GATE_RANK), BF16),
                   jax.ShapeDtypeStruct((D_PROJ_MAIN, D_MODEL), BF16),
                   jax.ShapeDtypeStruct((GATE_RANK, D_MODEL), BF16)],
        scratch_shapes=[pltpu.VMEM((PROJ_HEAD_TM, D_MODEL), BF16)],
        compiler_params=pltpu.CompilerParams(
            dimension_semantics=("arbitrary",), vmem_limit_bytes=VMEM_LIMIT),
        name="in_proj_head",
    )(x2, w_in_t, w_in_t)

    n_j = D_PROJ_MAIN // PROJ_TN
    slab_rows = D_MODEL // CAST_SLABS
    slab = lambda i, j: jnp.minimum(i * n_j + j, CAST_SLABS - 1)
    copying = lambda i: i < i0
    return pl.pallas_call(
        functools.partial(_in_proj_rest_kernel, head_tiles=i0),
        grid=(n_i, n_j),
        in_specs=[pl.BlockSpec((PROJ_TM, D_MODEL), lambda i, j: (jnp.maximum(i, i0), 0)),
                  pl.BlockSpec((PROJ_TN, D_MODEL), lambda i, j: (jnp.where(copying(i), 0, j), 0)),
                  pl.BlockSpec((GATE_RANK, D_MODEL), lambda i, j: (0, 0)),
                  pl.BlockSpec((PROJ_TM, PROJ_TN),
                               lambda i, j: (jnp.minimum(i, i0 - 1), jnp.where(copying(i), j, n_j - 1)),
                               pipeline_mode=pl.Buffered(1)),
                  pl.BlockSpec((PROJ_TM, GATE_RANK), lambda i, j: (jnp.minimum(i, i0 - 1), 0)),
                  pl.BlockSpec((None, slab_rows, D_MODEL), lambda i, j: (layer, slab(i, j), 0))],
        out_specs=[pl.BlockSpec((PROJ_TM, PROJ_TN), lambda i, j: (i, j)),
                   pl.BlockSpec((PROJ_TM, GATE_RANK), lambda i, j: (i, 0)),
                   pl.BlockSpec((slab_rows, D_MODEL), lambda i, j: (slab(i, j), 0))],
        out_shape=[jax.ShapeDtypeStruct((m, D_PROJ_MAIN), BF16),
                   jax.ShapeDtypeStruct((m, GATE_RANK), BF16),
                   jax.ShapeDtypeStruct((D_MODEL, D_MODEL), BF16)],
        scratch_shapes=[pltpu.VMEM((PROJ_TM, D_MODEL), BF16)],
        compiler_params=pltpu.CompilerParams(
            dimension_semantics=("arbitrary", "arbitrary"), vmem_limit_bytes=VMEM_LIMIT),
        name="in_proj_rest",
    )(x2, w_in_b, wz_b, proj_a, z_a, w_out)


def _shift_rows(h3, carry3, shift):
    pos = lax.broadcasted_iota(jnp.int32, h3.shape, 1)
    rolled = pltpu.roll(h3, shift, axis=1)
    rolled_prev = jnp.concatenate([pltpu.roll(carry3, shift, axis=1), rolled[:-1]], axis=0)
    return jnp.where(pos < shift, rolled_prev, rolled)


def _conv_piece(grp, proj_ref, w_ref, g_ref, carry_ref, y_ref):
    ts = proj_ref.shape[0]
    lo = grp * CONV_GROUP_WIDTH
    sl = slice(lo, lo + CONV_GROUP_WIDTH)
    h = (proj_ref[:, COL_C + lo:COL_C + lo + CONV_GROUP_WIDTH].astype(F32)
         * proj_ref[:, COL_U + lo:COL_U + lo + CONV_GROUP_WIDTH].astype(F32))
    h3 = h.reshape(ts // SUBLANES, SUBLANES, CONV_GROUP_WIDTH)
    carry3 = carry_ref[:, sl].reshape(1, SUBLANES, CONV_GROUP_WIDTH)
    h1 = _shift_rows(h3, carry3, 1).reshape(h.shape)
    h2 = _shift_rows(h3, carry3, 2).reshape(h.shape)
    carry_ref[:, sl] = h[ts - SUBLANES:, :]
    w = w_ref[:, sl]
    y = (proj_ref[:, COL_B + lo:COL_B + lo + CONV_GROUP_WIDTH].astype(F32)
         * (w[2:3, :] * h + w[1:2, :] * h1 + w[0:1, :] * h2))
    ms = jnp.mean(y * y, axis=-1, keepdims=True)
    y_ref[:, sl] = (y * lax.rsqrt(ms + RMS_EPS) * g_ref[:, sl]).astype(y_ref.dtype)


def _chunk_cumsum(x):
    t_blk, width = x.shape
    x3 = x.reshape(t_blk // SUBLANES, SUBLANES, width)
    pos = lax.broadcasted_iota(jnp.int32, x3.shape, 1)
    shift = 1
    while shift < SUBLANES:
        x3 = x3 + jnp.where(pos >= shift, pltpu.roll(x3, shift, axis=1), 0.0)
        shift *= 2
    vregs_per_chunk = CHUNK // SUBLANES
    x4 = x3.reshape(t_blk // CHUNK, vregs_per_chunk, SUBLANES, width)
    outs = [x4[:, 0]]
    for j in range(1, vregs_per_chunk):
        outs.append(x4[:, j] + outs[-1][:, SUBLANES - 1:SUBLANES, :])
    return jnp.stack(outs, axis=1).reshape(t_blk, width)


def _gla_prepare(head, proj_ref, z_ref, gb_ref, qd_ref, ki_ref, ke_ref, dct_ref):
    t_blk = proj_ref.shape[0]
    n_chunks = t_blk // CHUNK
    col_q, col_k = COL_Q + head * HEAD_K, COL_K + head * HEAD_K
    z = z_ref[:, head * HEAD_K:(head + 1) * HEAD_K] + gb_ref[:, head * HEAD_K:(head + 1) * HEAD_K]
    log_sig = jnp.minimum(z, 0.0) - jnp.log(1.0 + jnp.exp(-jnp.abs(z)))
    bcum = _chunk_cumsum(log_sig * (1.0 / GATE_TAU))
    qd_ref[head] = ((proj_ref[:, col_q:col_q + HEAD_K].astype(F32) * (HEAD_K ** -0.5))
                    * jnp.exp(bcum)).astype(BF16)
    k_inv_f = proj_ref[:, col_k:col_k + HEAD_K].astype(F32) * jnp.exp(-bcum)
    ki_ref[head] = k_inv_f.astype(BF16)
    decay = jnp.exp(bcum.reshape(n_chunks, CHUNK, HEAD_K)[:, CHUNK - 1:CHUNK, :])
    ke_ref[head] = (k_inv_f.reshape(n_chunks, CHUNK, HEAD_K) * decay).astype(BF16).reshape(
        t_blk, HEAD_K)
    pad = jnp.zeros((HEAD_K - n_chunks, HEAD_K), F32)
    dct_ref[head] = jnp.transpose(jnp.concatenate([decay.reshape(n_chunks, HEAD_K), pad], axis=0))


def _gla_scores(head, c, qd_ref, ki_ref, sc_ref):
    rows = slice(c * CHUNK, (c + 1) * CHUNK)
    causal = (lax.broadcasted_iota(jnp.int32, (CHUNK, CHUNK), 0)
              >= lax.broadcasted_iota(jnp.int32, (CHUNK, CHUNK), 1))
    scores = lax.dot_general(qd_ref[head, rows, :], ki_ref[head, rows, :], _NT_DIMS,
                             preferred_element_type=F32)
    sc_ref[head, rows, :] = jnp.where(causal, scores, 0.0).astype(sc_ref.dtype)


def _gla_state_step(head, c, proj_ref, ke_ref, dct_ref, state_ref, st_ref):
    rows = slice(c * CHUNK, (c + 1) * CHUNK)
    col_v = COL_V + head * HEAD_V
    state = state_ref[head]
    st_ref[head, c] = state.astype(st_ref.dtype)
    delta = lax.dot_general(ke_ref[head, rows, :], proj_ref[rows, col_v:col_v + HEAD_V], _TN_DIMS,
                            preferred_element_type=F32)
    state_ref[head] = dct_ref[head, :, c:c + 1] * state + delta


def _gla_output(head, c, proj_ref, g_ref, qd_ref, sc_ref, st_ref, y_ref):
    rows = slice(c * CHUNK, (c + 1) * CHUNK)
    col_v, col_r = COL_V + head * HEAD_V, COL_R + head * HEAD_V
    col_y = D_CONV + head * HEAD_V
    o = (jnp.dot(sc_ref[head, rows, :], proj_ref[rows, col_v:col_v + HEAD_V],
                 preferred_element_type=F32)
         + jnp.dot(qd_ref[head, rows, :], st_ref[head, c], preferred_element_type=F32))
    ms = jnp.mean(o * o, axis=-1, keepdims=True)
    o_n = o * lax.rsqrt(ms + RMS_EPS) * g_ref[:, head * HEAD_V:(head + 1) * HEAD_V]
    r_c = proj_ref[rows, col_r:col_r + HEAD_V].astype(F32)
    y_ref[rows, col_y:col_y + HEAD_V] = (o_n * (r_c * jax.nn.sigmoid(r_c))).astype(y_ref.dtype)


def _mixer_kernel(proj_ref, zl_ref, x_ref, wo_ref, cw_ref, cg_ref, wg_ref, gb_ref, gg_ref,
                  ln_g_ref, ln_b_ref, o_ref, y_ref, yp_ref, mix_ref, state_ref,
                  carry_ref, qd_ref, ki_ref, ke_ref, dct_ref, z_ref, sc_ref, st_ref, *, blocks_per_seq):
    s = pl.program_id(0)

    @pl.when(s == 0)
    def _():
        y_ref[...] = jnp.zeros_like(y_ref)

    @pl.when(s % blocks_per_seq == 0)
    def _():
        state_ref[...] = jnp.zeros_like(state_ref)
        carry_ref[...] = jnp.zeros_like(carry_ref)

    yp_ref[...] = y_ref[...]

    gla_refs = (qd_ref, ki_ref, ke_ref, dct_ref)
    n_chunks = proj_ref.shape[0] // CHUNK
    vector_pieces = [functools.partial(_conv_piece, grp, proj_ref, cw_ref, cg_ref, carry_ref, y_ref)
                     for grp in range(CONV_GROUPS)]
    vector_pieces += [functools.partial(_gla_prepare, head, proj_ref, z_ref, gb_ref, *gla_refs)
                      for head in range(GLA_HEADS)]
    head_chunks = [(head, c) for c in range(n_chunks) for head in range(GLA_HEADS)]
    chunk_pieces = (
        [functools.partial(_gla_scores, h, c, qd_ref, ki_ref, sc_ref) for h, c in head_chunks]
        + [functools.partial(_gla_state_step, h, c, proj_ref, ke_ref, dct_ref, state_ref, st_ref)
           for h, c in head_chunks]
        + [functools.partial(_gla_output, h, c, proj_ref, gg_ref, qd_ref, sc_ref, st_ref, y_ref)
           for h, c in head_chunks])

    def gate_preactivation():
        z_ref[...] = jnp.dot(zl_ref[...], wg_ref[...], preferred_element_type=F32)

    def out_proj_tile(sub, n):
        rows = slice(sub * ROW_SUB, (sub + 1) * ROW_SUB)
        cols = slice(n * OUT_TN, (n + 1) * OUT_TN)
        mix_ref[rows, cols] = jnp.dot(yp_ref[rows, :], wo_ref[:, cols], preferred_element_type=F32)

    def layer_norm_rows(row0, n_rows):
        rows = slice(row0, row0 + n_rows)
        o_ref[rows, :] = _layer_norm(DN_ALPHA * x_ref[rows, :] + mix_ref[rows, :],
                                     ln_g_ref[...], ln_b_ref[...]).astype(o_ref.dtype)

    tiles_per_sub = D_MODEL // OUT_TN
    n_tiles = (o_ref.shape[0] // ROW_SUB) * tiles_per_sub
    ln_rows = ROW_SUB // 2
    ln_queue = []
    pieces = [gate_preactivation]
    for t in range(n_tiles):
        sub, n = divmod(t, tiles_per_sub)
        lo = len(vector_pieces) * t // n_tiles
        hi = len(vector_pieces) * (t + 1) // n_tiles
        pieces += vector_pieces[lo:hi] + ln_queue[:1] + [functools.partial(out_proj_tile, sub, n)]
        del ln_queue[:1]
        if n == tiles_per_sub - 1:
            ln_queue += [functools.partial(layer_norm_rows, sub * ROW_SUB + r, ln_rows)
                         for r in range(0, ROW_SUB, ln_rows)]
    for piece in pieces + chunk_pieces + ln_queue:
        piece()


def _mixer(proj, z_low, x2, w_out_b, conv_w8, conv_g, wg_b, gate_bias, gla_g, ln_g, ln_b, bsz, seq):
    m = x2.shape[0]
    n_blk = m // MIX_T
    cur_blk = lambda s: (jnp.minimum(s, n_blk - 1), 0)
    prev_blk = lambda s: (jnp.maximum(s - 1, 0), 0)
    const2 = lambda s: (0, 0)
    return pl.pallas_call(
        functools.partial(_mixer_kernel, blocks_per_seq=seq // MIX_T),
        grid=(n_blk + 1,),
        in_specs=[pl.BlockSpec((MIX_T, D_PROJ_MAIN), cur_blk),
                  pl.BlockSpec((MIX_T, GATE_RANK), cur_blk),
                  pl.BlockSpec((MIX_T, D_MODEL), prev_blk),
                  pl.BlockSpec((D_CONV + D_GLA_V, D_MODEL), const2, pipeline_mode=pl.Buffered(1)),
                  pl.BlockSpec((8, D_CONV), const2),
                  pl.BlockSpec((1, D_CONV), const2),
                  pl.BlockSpec((GATE_RANK, D_GLA_K), const2),
                  pl.BlockSpec((1, D_GLA_K), const2),
                  pl.BlockSpec((1, D_GLA_V), const2),
                  pl.BlockSpec((1, D_MODEL), const2),
                  pl.BlockSpec((1, D_MODEL), const2)],
        out_specs=pl.BlockSpec((MIX_T, D_MODEL), prev_blk),
        out_shape=jax.ShapeDtypeStruct((m, D_MODEL), BF16),
        scratch_shapes=[pltpu.VMEM((MIX_T, D_CONV + D_GLA_V), BF16),
                        pltpu.VMEM((MIX_T, D_CONV + D_GLA_V), BF16),
                        pltpu.VMEM((MIX_T, D_MODEL), F32),
                        pltpu.VMEM((GLA_HEADS, HEAD_K, HEAD_V), F32),
                        pltpu.VMEM((8, D_CONV), F32),
                        pltpu.VMEM((GLA_HEADS, MIX_T, HEAD_K), BF16),
                        pltpu.VMEM((GLA_HEADS, MIX_T, HEAD_K), BF16),
                        pltpu.VMEM((GLA_HEADS, MIX_T, HEAD_K), BF16),
                        pltpu.VMEM((GLA_HEADS, HEAD_K, HEAD_K), F32),
                        pltpu.VMEM((MIX_T, D_GLA_K), F32),
                        pltpu.VMEM((GLA_HEADS, MIX_T, CHUNK), BF16),
                        pltpu.VMEM((GLA_HEADS, MIX_T // CHUNK, HEAD_K, HEAD_V), BF16)],
        compiler_params=pltpu.CompilerParams(
            dimension_semantics=("arbitrary",), vmem_limit_bytes=VMEM_LIMIT),
        name="mixer",
    )(proj, z_low, x2, w_out_b, conv_w8, conv_g, wg_b, gate_bias, gla_g, ln_g, ln_b)


def _ffn_body(f, last, x_ref, up_tile, down_tile, g_ref, b_ref, o_ref):
    def hidden():
        h = jnp.dot(x_ref[...], up_tile(), preferred_element_type=F32)
        h = jnp.maximum(h, 0.0)
        return (h * h).astype(BF16)

    @pl.when(f == 0)
    def _():
        o_ref[...] = jnp.dot(hidden(), down_tile(), preferred_element_type=F32)

    @pl.when(jnp.logical_and(f > 0, f < last))
    def _():
        o_ref[...] += jnp.dot(hidden(), down_tile(), preferred_element_type=F32)

    @pl.when(f == last)
    def _():
        h = hidden()
        wd = down_tile()
        for s in range(o_ref.shape[0] // ROW_SUB):
            rows = slice(s * ROW_SUB, (s + 1) * ROW_SUB)
            ff = o_ref[rows, :] + jnp.dot(h[rows, :], wd, preferred_element_type=F32)
            o_ref[rows, :] = _layer_norm(DN_ALPHA * x_ref[rows, :].astype(F32) + ff,
                                         g_ref[...], b_ref[...])


def _ffn_head_kernel(x_ref, wu_ref, wd_ref, g_ref, b_ref, o_ref, wu_b_ref, wd_b_ref):
    def up_tile():
        wu_b_ref[...] = wu_ref[...].astype(BF16)
        return wu_b_ref[...]

    def down_tile():
        wd_b_ref[...] = wd_ref[...].astype(BF16)
        return wd_b_ref[...]

    _ffn_body(pl.program_id(0), pl.num_programs(0) - 1, x_ref, up_tile, down_tile, g_ref, b_ref, o_ref)


def _ffn_rest_kernel(x_ref, wu_ref, wd_ref, g_ref, b_ref, oa_ref, o_ref, *, head_tiles):
    i, f = pl.program_id(0), pl.program_id(1)
    last = pl.num_programs(1) - 1

    @pl.when(jnp.logical_and(i < head_tiles, f == last))
    def _():
        o_ref[...] = oa_ref[...]

    @pl.when(i >= head_tiles)
    def _():
        _ffn_body(f, last, x_ref, lambda: wu_ref[...], lambda: wd_ref[...], g_ref, b_ref, o_ref)


def _ffn_ln2(x1, w_ff_up, w_ff_down, ln_g, ln_b, layer):
    m = x1.shape[0]

    out_a, w_up_b, w_down_b = pl.pallas_call(
        _ffn_head_kernel,
        grid=(D_FF // FFN_HEAD_TF,),
        in_specs=[pl.BlockSpec((FFN_HEAD_TM, D_MODEL), lambda f: (0, 0), pipeline_mode=pl.Buffered(1)),
                  pl.BlockSpec((None, D_MODEL, FFN_HEAD_TF), lambda f: (layer, 0, f)),
                  pl.BlockSpec((None, FFN_HEAD_TF, D_MODEL), lambda f: (layer, f, 0)),
                  pl.BlockSpec((1, D_MODEL), lambda f: (0, 0)),
                  pl.BlockSpec((1, D_MODEL), lambda f: (0, 0))],
        out_specs=[pl.BlockSpec((FFN_HEAD_TM, D_MODEL), lambda f: (0, 0)),
                   pl.BlockSpec((D_MODEL, FFN_HEAD_TF), lambda f: (0, f)),
                   pl.BlockSpec((FFN_HEAD_TF, D_MODEL), lambda f: (f, 0))],
        out_shape=[jax.ShapeDtypeStruct((FFN_HEAD_TM, D_MODEL), F32),
                   jax.ShapeDtypeStruct((D_MODEL, D_FF), BF16),
                   jax.ShapeDtypeStruct((D_FF, D_MODEL), BF16)],
        compiler_params=pltpu.CompilerParams(
            dimension_semantics=("arbitrary",), vmem_limit_bytes=VMEM_LIMIT),
        name="ffn_head",
    )(x1, w_ff_up, w_ff_down, ln_g, ln_b)

    i0 = FFN_HEAD_TM // FFN_TM
    copying = lambda i: i < i0
    return pl.pallas_call(
        functools.partial(_ffn_rest_kernel, head_tiles=i0),
        grid=(m // FFN_TM, D_FF // FFN_TF),
        in_specs=[pl.BlockSpec((FFN_TM, D_MODEL), lambda i, f: (jnp.maximum(i, i0), 0)),
                  pl.BlockSpec((D_MODEL, FFN_TF), lambda i, f: (0, jnp.where(copying(i), 0, f))),
                  pl.BlockSpec((FFN_TF, D_MODEL), lambda i, f: (jnp.where(copying(i), 0, f), 0)),
                  pl.BlockSpec((1, D_MODEL), lambda i, f: (0, 0)),
                  pl.BlockSpec((1, D_MODEL), lambda i, f: (0, 0)),
                  pl.BlockSpec((FFN_TM, D_MODEL), lambda i, f: (jnp.minimum(i, i0 - 1), 0),
                               pipeline_mode=pl.Buffered(1))],
        out_specs=pl.BlockSpec((FFN_TM, D_MODEL), lambda i, f: (i, 0)),
        out_shape=jax.ShapeDtypeStruct((m, D_MODEL), F32),
        compiler_params=pltpu.CompilerParams(
            dimension_semantics=("arbitrary", "arbitrary"), vmem_limit_bytes=VMEM_LIMIT),
        name="ffn_rest",
    )(x1, w_up_b, w_down_b, ln_g, ln_b, out_a)


def kernel(x, w_in, conv_w, conv_norm_g, w_gate_up, gate_bias, gla_norm_g, w_out,
           ln1_g, ln1_b, w_ff_up, w_ff_down, ln2_g, ln2_b):
    bsz, seq, _ = x.shape
    assert seq % MIX_T == 0 and MIX_T % CHUNK == 0 and MIX_T % ROW_SUB == 0
    x2 = x.reshape(bsz * seq, D_MODEL)
    w_in_t = jnp.swapaxes(w_in, 1, 2)
    for l in range(DEPTH):
        wg_b = w_gate_up[l].astype(BF16)
        gb = gate_bias[l].reshape(1, D_GLA_K)
        conv_w8 = jnp.pad(conv_w[l], ((0, 8 - conv_w.shape[1]), (0, 0)))

        proj, z_low, w_out_b = _in_proj(x2, w_in_t, w_out, l)
        x1 = _mixer(proj, z_low, x2, w_out_b, conv_w8, conv_norm_g[l].reshape(1, D_CONV),
                    wg_b, gb, gla_norm_g[l].reshape(1, D_GLA_V),
                    ln1_g[l].reshape(1, D_MODEL), ln1_b[l].reshape(1, D_MODEL), bsz, seq)
        x2 = _ffn_ln2(x1, w_ff_up, w_ff_down,
                      ln2_g[l].reshape(1, D_MODEL), ln2_b[l].reshape(1, D_MODEL), l)
    return x2.reshape(bsz, seq, D_MODEL)
```

```python
import functools

import jax
import jax.numpy as jnp
from jax import lax
from jax.experimental import pallas as pl
from jax.experimental.pallas import tpu as pltpu

F32 = jnp.float32
BF16 = jnp.bfloat16

D_MODEL = 2048
D_CONV = 1024
CONV_GROUPS = 8
CONV_GROUP_WIDTH = D_CONV // CONV_GROUPS
GLA_HEADS = 4
HEAD_K = 128
HEAD_V = 256
D_GLA_K = GLA_HEADS * HEAD_K
D_GLA_V = GLA_HEADS * HEAD_V
GATE_RANK = 16
GATE_TAU = 16.0
CHUNK = 64
D_FF = 4 * D_MODEL
LN_EPS = 1e-5
RMS_EPS = 1e-6
DEPTH = 1
DN_ALPHA = (2.0 * DEPTH) ** 0.25

D_PROJ_MAIN = 3 * D_CONV + 2 * D_GLA_K + 2 * D_GLA_V

COL_B, COL_C, COL_U = 0, D_CONV, 2 * D_CONV
COL_Q = 3 * D_CONV
COL_K = COL_Q + D_GLA_K
COL_V = COL_K + D_GLA_K
COL_R = COL_V + D_GLA_V

VMEM_LIMIT = 60 * 1024 * 1024
SUBLANES = 8

PROJ_TM = 1024
PROJ_HEAD_TM = 1024
PROJ_HEAD_TN = 1024
PROJ_TN = 2048
CAST_SLABS = 16
MIX_T = 512
ROW_SUB = 256
OUT_TN = 512
FFN_HEAD_TM = 1024
FFN_HEAD_TF = 512
FFN_TM = 512
FFN_TF = 2048

_NT_DIMS = (((1,), (1,)), ((), ()))
_TN_DIMS = (((0,), (0,)), ((), ()))


def _layer_norm(y, g, b):
    mu = jnp.mean(y, axis=-1, keepdims=True)
    yc = y - mu
    var = jnp.mean(yc * yc, axis=-1, keepdims=True)
    return yc * lax.rsqrt(var + LN_EPS) * g + b


def _in_proj_head_kernel(x_ref, w_ref, wz_ref, o_ref, z_ref, wb_ref, wzb_ref, xb_ref):
    @pl.when(pl.program_id(0) == 0)
    def _():
        xb_ref[...] = x_ref[...].astype(BF16)
        wzb_ref[...] = wz_ref[...].astype(BF16)
        z_ref[...] = lax.dot_general(xb_ref[...], wzb_ref[...], _NT_DIMS,
                                     preferred_element_type=F32).astype(z_ref.dtype)

    wb_ref[...] = w_ref[...].astype(BF16)
    o_ref[...] = lax.dot_general(xb_ref[...], wb_ref[...], _NT_DIMS,
                                 preferred_element_type=F32).astype(o_ref.dtype)


def _in_proj_rest_kernel(x_ref, wb_ref, wzb_ref, pa_ref, za_ref, wo_ref,
                         o_ref, z_ref, wo_b_ref, xb_ref, *, head_tiles):
    i, j = pl.program_id(0), pl.program_id(1)

    wo_b_ref[...] = wo_ref[...].astype(BF16)

    @pl.when(i < head_tiles)
    def _():
        o_ref[...] = pa_ref[...]

        @pl.when(j == 0)
        def _():
            z_ref[...] = za_ref[...]

    @pl.when(i >= head_tiles)
    def _():
        @pl.when(j == 0)
        def _():
            xb_ref[...] = x_ref[...].astype(BF16)
            z_ref[...] = lax.dot_general(xb_ref[...], wzb_ref[...], _NT_DIMS,
                                         preferred_element_type=F32).astype(z_ref.dtype)

        o_ref[...] = lax.dot_general(xb_ref[...], wb_ref[...], _NT_DIMS,
                                     preferred_element_type=F32).astype(o_ref.dtype)


def _in_proj(x2, w_in_t, w_out, layer):
    m = x2.shape[0]
    i0 = PROJ_HEAD_TM // PROJ_TM
    n_i = m // PROJ_TM

    n_jh = D_PROJ_MAIN // PROJ_HEAD_TN
    proj_a, z_a, w_in_b, wz_b = pl.pallas_call(
        _in_proj_head_kernel,
        grid=(n_jh,),
        in_specs=[pl.BlockSpec((PROJ_HEAD_TM, D_MODEL), lambda j: (0, 0), pipeline_mode=pl.Buffered(1)),
                  pl.BlockSpec((None, PROJ_HEAD_TN, D_MODEL), lambda j: (layer, j, 0)),
                  pl.BlockSpec((None, GATE_RANK, D_MODEL),
                               lambda j: (layer, D_PROJ_MAIN // GATE_RANK, 0))],
        out_specs=[pl.BlockSpec((PROJ_HEAD_TM, PROJ_HEAD_TN), lambda j: (0, j)),
                   pl.BlockSpec((PROJ_HEAD_TM, GATE_RANK), lambda j: (0, 0)),
                   pl.BlockSpec((PROJ_HEAD_TN, D_MODEL), lambda j: (j, 0)),
                   pl.BlockSpec((GATE_RANK, D_MODEL), lambda j: (0, 0))],
        out_shape=[jax.ShapeDtypeStruct((PROJ_HEAD_TM, D_PROJ_MAIN), BF16),
                   jax.ShapeDtypeStruct((PROJ_HEAD_TM, GATE_RANK), BF16),
                   jax.ShapeDtypeStruct((D_PROJ_MAIN, D_MODEL), BF16),
                   jax.ShapeDtypeStruct((GATE_RANK, D_MODEL), BF16)],
        scratch_shapes=[pltpu.VMEM((PROJ_HEAD_TM, D_MODEL), BF16)],
        compiler_params=pltpu.CompilerParams(
            dimension_semantics=("arbitrary",), vmem_limit_bytes=VMEM_LIMIT),
        name="in_proj_head",
    )(x2, w_in_t, w_in_t)

    n_j = D_PROJ_MAIN // PROJ_TN
    slab_rows = D_MODEL // CAST_SLABS
    slab = lambda i, j: jnp.minimum(i * n_j + j, CAST_SLABS - 1)
    copying = lambda i: i < i0
    return pl.pallas_call(
        functools.partial(_in_proj_rest_kernel, head_tiles=i0),
        grid=(n_i, n_j),
        in_specs=[pl.BlockSpec((PROJ_TM, D_MODEL), lambda i, j: (jnp.maximum(i, i0), 0)),
                  pl.BlockSpec((PROJ_TN, D_MODEL), lambda i, j: (jnp.where(copying(i), 0, j), 0)),
                  pl.BlockSpec((GATE_RANK, D_MODEL), lambda i, j: (0, 0)),
                  pl.BlockSpec((PROJ_TM, PROJ_TN),
                               lambda i, j: (jnp.minimum(i, i0 - 1), jnp.where(copying(i), j, n_j - 1)),
                               pipeline_mode=pl.Buffered(1)),
                  pl.BlockSpec((PROJ_TM, GATE_RANK), lambda i, j: (jnp.minimum(i, i0 - 1), 0)),
                  pl.BlockSpec((None, slab_rows, D_MODEL), lambda i, j: (layer, slab(i, j), 0))],
        out_specs=[pl.BlockSpec((PROJ_TM, PROJ_TN), lambda i, j: (i, j)),
                   pl.BlockSpec((PROJ_TM, GATE_RANK), lambda i, j: (i, 0)),
                   pl.BlockSpec((slab_rows, D_MODEL), lambda i, j: (slab(i, j), 0))],
        out_shape=[jax.ShapeDtypeStruct((m, D_PROJ_MAIN), BF16),
                   jax.ShapeDtypeStruct((m, GATE_RANK), BF16),
                   jax.ShapeDtypeStruct((D_MODEL, D_MODEL), BF16)],
        scratch_shapes=[pltpu.VMEM((PROJ_TM, D_MODEL), BF16)],
        compiler_params=pltpu.CompilerParams(
            dimension_semantics=("arbitrary", "arbitrary"), vmem_limit_bytes=VMEM_LIMIT),
        name="in_proj_rest",
    )(x2, w_in_b, wz_b, proj_a, z_a, w_out)


def _shift_rows(h3, carry3, shift):
    pos = lax.broadcasted_iota(jnp.int32, h3.shape, 1)
    rolled = pltpu.roll(h3, shift, axis=1)
    rolled_prev = jnp.concatenate([pltpu.roll(carry3, shift, axis=1), rolled[:-1]], axis=0)
    return jnp.where(pos < shift, rolled_prev, rolled)


def _conv_piece(grp, proj_ref, w_ref, g_ref, carry_ref, y_ref):
    ts = proj_ref.shape[0]
    lo = grp * CONV_GROUP_WIDTH
    sl = slice(lo, lo + CONV_GROUP_WIDTH)
    h = (proj_ref[:, COL_C + lo:COL_C + lo + CONV_GROUP_WIDTH].astype(F32)
         * proj_ref[:, COL_U + lo:COL_U + lo + CONV_GROUP_WIDTH].astype(F32))
    h3 = h.reshape(ts // SUBLANES, SUBLANES, CONV_GROUP_WIDTH)
    carry3 = carry_ref[:, sl].reshape(1, SUBLANES, CONV_GROUP_WIDTH)
    h1 = _shift_rows(h3, carry3, 1).reshape(h.shape)
    h2 = _shift_rows(h3, carry3, 2).reshape(h.shape)
    carry_ref[:, sl] = h[ts - SUBLANES:, :]
    w = w_ref[:, sl]
    y = (proj_ref[:, COL_B + lo:COL_B + lo + CONV_GROUP_WIDTH].astype(F32)
         * (w[2:3, :] * h + w[1:2, :] * h1 + w[0:1, :] * h2))
    ms = jnp.mean(y * y, axis=-1, keepdims=True)
    y_ref[:, sl] = (y * lax.rsqrt(ms + RMS_EPS) * g_ref[:, sl]).astype(y_ref.dtype)


def _chunk_cumsum(x):
    t_blk, width = x.shape
    x3 = x.reshape(t_blk // SUBLANES, SUBLANES, width)
    pos = lax.broadcasted_iota(jnp.int32, x3.shape, 1)
    shift = 1
    while shift < SUBLANES:
        x3 = x3 + jnp.where(pos >= shift, pltpu.roll(x3, shift, axis=1), 0.0)
        shift *= 2
    vregs_per_chunk = CHUNK // SUBLANES
    x4 = x3.reshape(t_blk // CHUNK, vregs_per_chunk, SUBLANES, width)
    outs = [x4[:, 0]]
    for j in range(1, vregs_per_chunk):
        outs.append(x4[:, j] + outs[-1][:, SUBLANES - 1:SUBLANES, :])
    return jnp.stack(outs, axis=1).reshape(t_blk, width)


def _gla_prepare(head, proj_ref, z_ref, gb_ref, qd_ref, ki_ref, ke_ref, dct_ref):
    t_blk = proj_ref.shape[0]
    n_chunks = t_blk // CHUNK
    col_q, col_k = COL_Q + head * HEAD_K, COL_K + head * HEAD_K
    z = z_ref[:, head * HEAD_K:(head + 1) * HEAD_K] + gb_ref[:, head * HEAD_K:(head + 1) * HEAD_K]
    log_sig = jnp.minimum(z, 0.0) - jnp.log(1.0 + jnp.exp(-jnp.abs(z)))
    bcum = _chunk_cumsum(log_sig * (1.0 / GATE_TAU))
    qd_ref[head] = ((proj_ref[:, col_q:col_q + HEAD_K].astype(F32) * (HEAD_K ** -0.5))
                    * jnp.exp(bcum)).astype(BF16)
    k_inv_f = proj_ref[:, col_k:col_k + HEAD_K].astype(F32) * jnp.exp(-bcum)
    ki_ref[head] = k_inv_f.astype(BF16)
    decay = jnp.exp(bcum.reshape(n_chunks, CHUNK, HEAD_K)[:, CHUNK - 1:CHUNK, :])
    ke_ref[head] = (k_inv_f.reshape(n_chunks, CHUNK, HEAD_K) * decay).astype(BF16).reshape(
        t_blk, HEAD_K)
    pad = jnp.zeros((HEAD_K - n_chunks, HEAD_K), F32)
    dct_ref[head] = jnp.transpose(jnp.concatenate([decay.reshape(n_chunks, HEAD_K), pad], axis=0))


def _gla_scores(head, c, qd_ref, ki_ref, sc_ref):
    rows = slice(c * CHUNK, (c + 1) * CHUNK)
    causal = (lax.broadcasted_iota(jnp.int32, (CHUNK, CHUNK), 0)
              >= lax.broadcasted_iota(jnp.int32, (CHUNK, CHUNK), 1))
    scores = lax.dot_general(qd_ref[head, rows, :], ki_ref[head, rows, :], _NT_DIMS,
                             preferred_element_type=F32)
    sc_ref[head, rows, :] = jnp.where(causal, scores, 0.0).astype(sc_ref.dtype)


def _gla_state_step(head, c, proj_ref, ke_ref, dct_ref, state_ref, st_ref):
    rows = slice(c * CHUNK, (c + 1) * CHUNK)
    col_v = COL_V + head * HEAD_V
    state = state_ref[head]
    st_ref[head, c] = state.astype(st_ref.dtype)
    delta = lax.dot_general(ke_ref[head, rows, :], proj_ref[rows, col_v:col_v + HEAD_V], _TN_DIMS,
                            preferred_element_type=F32)
    state_ref[head] = dct_ref[head, :, c:c + 1] * state + delta


def _gla_output(head, c, proj_ref, g_ref, qd_ref, sc_ref, st_ref, y_ref):
    rows = slice(c * CHUNK, (c + 1) * CHUNK)
    col_v, col_r = COL_V + head * HEAD_V, COL_R + head * HEAD_V
    col_y = D_CONV + head * HEAD_V
    o = (jnp.dot(sc_ref[head, rows, :], proj_ref[rows, col_v:col_v + HEAD_V],
                 preferred_element_type=F32)
         + jnp.dot(qd_ref[head, rows, :], st_ref[head, c], preferred_element_type=F32))
    ms = jnp.mean(o * o, axis=-1, keepdims=True)
    o_n = o * lax.rsqrt(ms + RMS_EPS) * g_ref[:, head * HEAD_V:(head + 1) * HEAD_V]
    r_c = proj_ref[rows, col_r:col_r + HEAD_V].astype(F32)
    y_ref[rows, col_y:col_y + HEAD_V] = (o_n * (r_c * jax.nn.sigmoid(r_c))).astype(y_ref.dtype)


def _mixer_kernel(proj_ref, zl_ref, x_ref, wo_ref, cw_ref, cg_ref, wg_ref, gb_ref, gg_ref,
                  ln_g_ref, ln_b_ref, o_ref, y_ref, yp_ref, mix_ref, state_ref,
                  carry_ref, qd_ref, ki_ref, ke_ref, dct_ref, z_ref, sc_ref, st_ref, *, blocks_per_seq):
    s = pl.program_id(0)

    @pl.when(s == 0)
    def _():
        y_ref[...] = jnp.zeros_like(y_ref)

    @pl.when(s % blocks_per_seq == 0)
    def _():
        state_ref[...] = jnp.zeros_like(state_ref)
        carry_ref[...] = jnp.zeros_like(carry_ref)

    yp_ref[...] = y_ref[...]

    gla_refs = (qd_ref, ki_ref, ke_ref, dct_ref)
    n_chunks = proj_ref.shape[0] // CHUNK
    vector_pieces = [functools.partial(_conv_piece, grp, proj_ref, cw_ref, cg_ref, carry_ref, y_ref)
                     for grp in range(CONV_GROUPS)]
    vector_pieces += [functools.partial(_gla_prepare, head, proj_ref, z_ref, gb_ref, *gla_refs)
                      for head in range(GLA_HEADS)]
    head_chunks = [(head, c) for c in range(n_chunks) for head in range(GLA_HEADS)]
    chunk_pieces = (
        [functools.partial(_gla_scores, h, c, qd_ref, ki_ref, sc_ref) for h, c in head_chunks]
        + [functools.partial(_gla_state_step, h, c, proj_ref, ke_ref, dct_ref, state_ref, st_ref)
           for h, c in head_chunks]
        + [functools.partial(_gla_output, h, c, proj_ref, gg_ref, qd_ref, sc_ref, st_ref, y_ref)
           for h, c in head_chunks])

    def gate_preactivation():
        z_ref[...] = jnp.dot(zl_ref[...], wg_ref[...], preferred_element_type=F32)

    def out_proj_tile(sub, n):
        rows = slice(sub * ROW_SUB, (sub + 1) * ROW_SUB)
        cols = slice(n * OUT_TN, (n + 1) * OUT_TN)
        mix_ref[rows, cols] = jnp.dot(yp_ref[rows, :], wo_ref[:, cols], preferred_element_type=F32)

    def layer_norm_rows(row0, n_rows):
        rows = slice(row0, row0 + n_rows)
        o_ref[rows, :] = _layer_norm(DN_ALPHA * x_ref[rows, :] + mix_ref[rows, :],
                                     ln_g_ref[...], ln_b_ref[...]).astype(o_ref.dtype)

    tiles_per_sub = D_MODEL // OUT_TN
    n_tiles = (o_ref.shape[0] // ROW_SUB) * tiles_per_sub
    ln_rows = ROW_SUB // 2
    ln_queue = []
    pieces = [gate_preactivation]
    for t in range(n_tiles):
        sub, n = divmod(t, tiles_per_sub)
        lo = len(vector_pieces) * t // n_tiles
        hi = len(vector_pieces) * (t + 1) // n_tiles
        pieces += vector_pieces[lo:hi] + ln_queue[:1] + [functools.partial(out_proj_tile, sub, n)]
        del ln_queue[:1]
        if n == tiles_per_sub - 1:
            ln_queue += [functools.partial(layer_norm_rows, sub * ROW_SUB + r, ln_rows)
                         for r in range(0, ROW_SUB, ln_rows)]
    everything = pieces + chunk_pieces + ln_queue
    mixing = [gate_preactivation] + vector_pieces + chunk_pieces
    last = pl.num_programs(0) - 1

    @pl.when(s < last)
    def _():
        for piece in everything:
            piece()

    @pl.when(s == last)
    def _():
        for piece in everything:
            if piece not in mixing:
                piece()


def _mixer(proj, z_low, x2, w_out_b, conv_w8, conv_g, wg_b, gate_bias, gla_g, ln_g, ln_b, bsz, seq):
    m = x2.shape[0]
    n_blk = m // MIX_T
    cur_blk = lambda s: (jnp.minimum(s, n_blk - 1), 0)
    prev_blk = lambda s: (jnp.maximum(s - 1, 0), 0)
    const2 = lambda s: (0, 0)
    return pl.pallas_call(
        functools.partial(_mixer_kernel, blocks_per_seq=seq // MIX_T),
        grid=(n_blk + 1,),
        in_specs=[pl.BlockSpec((MIX_T, D_PROJ_MAIN), cur_blk),
                  pl.BlockSpec((MIX_T, GATE_RANK), cur_blk),
                  pl.BlockSpec((MIX_T, D_MODEL), prev_blk),
                  pl.BlockSpec((D_CONV + D_GLA_V, D_MODEL), const2, pipeline_mode=pl.Buffered(1)),
                  pl.BlockSpec((8, D_CONV), const2),
                  pl.BlockSpec((1, D_CONV), const2),
                  pl.BlockSpec((GATE_RANK, D_GLA_K), const2),
                  pl.BlockSpec((1, D_GLA_K), const2),
                  pl.BlockSpec((1, D_GLA_V), const2),
                  pl.BlockSpec((1, D_MODEL), const2),
                  pl.BlockSpec((1, D_MODEL), const2)],
        out_specs=pl.BlockSpec((MIX_T, D_MODEL), prev_blk),
        out_shape=jax.ShapeDtypeStruct((m, D_MODEL), BF16),
        scratch_shapes=[pltpu.VMEM((MIX_T, D_CONV + D_GLA_V), BF16),
                        pltpu.VMEM((MIX_T, D_CONV + D_GLA_V), BF16),
                        pltpu.VMEM((MIX_T, D_MODEL), F32),
                        pltpu.VMEM((GLA_HEADS, HEAD_K, HEAD_V), F32),
                        pltpu.VMEM((8, D_CONV), F32),
                        pltpu.VMEM((GLA_HEADS, MIX_T, HEAD_K), BF16),
                        pltpu.VMEM((GLA_HEADS, MIX_T, HEAD_K), BF16),
                        pltpu.VMEM((GLA_HEADS, MIX_T, HEAD_K), BF16),
                        pltpu.VMEM((GLA_HEADS, HEAD_K, HEAD_K), F32),
                        pltpu.VMEM((MIX_T, D_GLA_K), F32),
                        pltpu.VMEM((GLA_HEADS, MIX_T, CHUNK), BF16),
                        pltpu.VMEM((GLA_HEADS, MIX_T // CHUNK, HEAD_K, HEAD_V), BF16)],
        compiler_params=pltpu.CompilerParams(
            dimension_semantics=("arbitrary",), vmem_limit_bytes=VMEM_LIMIT),
        name="mixer",
    )(proj, z_low, x2, w_out_b, conv_w8, conv_g, wg_b, gate_bias, gla_g, ln_g, ln_b)


def _ffn_body(f, last, x_ref, up_tile, down_tile, g_ref, b_ref, o_ref):
    def hidden():
        h = jnp.dot(x_ref[...], up_tile(), preferred_element_type=F32)
        h = jnp.maximum(h, 0.0)
        return (h * h).astype(BF16)

    @pl.when(f == 0)
    def _():
        o_ref[...] = jnp.dot(hidden(), down_tile(), preferred_element_type=F32)

    @pl.when(jnp.logical_and(f > 0, f < last))
    def _():
        o_ref[...] += jnp.dot(hidden(), down_tile(), preferred_element_type=F32)

    @pl.when(f == last)
    def _():
        h = hidden()
        wd = down_tile()
        for s in range(o_ref.shape[0] // ROW_SUB):
            rows = slice(s * ROW_SUB, (s + 1) * ROW_SUB)
            ff = o_ref[rows, :] + jnp.dot(h[rows, :], wd, preferred_element_type=F32)
            o_ref[rows, :] = _layer_norm(DN_ALPHA * x_ref[rows, :].astype(F32) + ff,
                                         g_ref[...], b_ref[...])


def _ffn_head_kernel(x_ref, wu_ref, wd_ref, g_ref, b_ref, o_ref, wu_b_ref, wd_b_ref):
    def up_tile():
        wu_b_ref[...] = wu_ref[...].astype(BF16)
        return wu_b_ref[...]

    def down_tile():
        wd_b_ref[...] = wd_ref[...].astype(BF16)
        return wd_b_ref[...]

    _ffn_body(pl.program_id(0), pl.num_programs(0) - 1, x_ref, up_tile, down_tile, g_ref, b_ref, o_ref)


def _ffn_rest_kernel(x_ref, wu_ref, wd_ref, g_ref, b_ref, oa_ref, o_ref, *, head_tiles):
    i, f = pl.program_id(0), pl.program_id(1)
    last = pl.num_programs(1) - 1

    @pl.when(jnp.logical_and(i < head_tiles, f == last))
    def _():
        o_ref[...] = oa_ref[...]

    @pl.when(i >= head_tiles)
    def _():
        _ffn_body(f, last, x_ref, lambda: wu_ref[...], lambda: wd_ref[...], g_ref, b_ref, o_ref)


def _ffn_ln2(x1, w_ff_up, w_ff_down, ln_g, ln_b, layer):
    m = x1.shape[0]

    out_a, w_up_b, w_down_b = pl.pallas_call(
        _ffn_head_kernel,
        grid=(D_FF // FFN_HEAD_TF,),
        in_specs=[pl.BlockSpec((FFN_HEAD_TM, D_MODEL), lambda f: (0, 0), pipeline_mode=pl.Buffered(1)),
                  pl.BlockSpec((None, D_MODEL, FFN_HEAD_TF), lambda f: (layer, 0, f)),
                  pl.BlockSpec((None, FFN_HEAD_TF, D_MODEL), lambda f: (layer, f, 0)),
                  pl.BlockSpec((1, D_MODEL), lambda f: (0, 0)),
                  pl.BlockSpec((1, D_MODEL), lambda f: (0, 0))],
        out_specs=[pl.BlockSpec((FFN_HEAD_TM, D_MODEL), lambda f: (0, 0)),
                   pl.BlockSpec((D_MODEL, FFN_HEAD_TF), lambda f: (0, f)),
                   pl.BlockSpec((FFN_HEAD_TF, D_MODEL), lambda f: (f, 0))],
        out_shape=[jax.ShapeDtypeStruct((FFN_HEAD_TM, D_MODEL), F32),
                   jax.ShapeDtypeStruct((D_MODEL, D_FF), BF16),
                   jax.ShapeDtypeStruct((D_FF, D_MODEL), BF16)],
        compiler_params=pltpu.CompilerParams(
            dimension_semantics=("arbitrary",), vmem_limit_bytes=VMEM_LIMIT),
        name="ffn_head",
    )(x1, w_ff_up, w_ff_down, ln_g, ln_b)

    i0 = FFN_HEAD_TM // FFN_TM
    copying = lambda i: i < i0
    return pl.pallas_call(
        functools.partial(_ffn_rest_kernel, head_tiles=i0),
        grid=(m // FFN_TM, D_FF // FFN_TF),
        in_specs=[pl.BlockSpec((FFN_TM, D_MODEL), lambda i, f: (jnp.maximum(i, i0), 0)),
                  pl.BlockSpec((D_MODEL, FFN_TF), lambda i, f: (0, jnp.where(copying(i), 0, f))),
                  pl.BlockSpec((FFN_TF, D_MODEL), lambda i, f: (jnp.where(copying(i), 0, f), 0)),
                  pl.BlockSpec((1, D_MODEL), lambda i, f: (0, 0)),
                  pl.BlockSpec((1, D_MODEL), lambda i, f: (0, 0)),
                  pl.BlockSpec((FFN_TM, D_MODEL), lambda i, f: (jnp.minimum(i, i0 - 1), 0),
                               pipeline_mode=pl.Buffered(1))],
        out_specs=pl.BlockSpec((FFN_TM, D_MODEL), lambda i, f: (i, 0)),
        out_shape=jax.ShapeDtypeStruct((m, D_MODEL), F32),
        compiler_params=pltpu.CompilerParams(
            dimension_semantics=("arbitrary", "arbitrary"), vmem_limit_bytes=VMEM_LIMIT),
        name="ffn_rest",
    )(x1, w_up_b, w_down_b, ln_g, ln_b, out_a)


def kernel(x, w_in, conv_w, conv_norm_g, w_gate_up, gate_bias, gla_norm_g, w_out,
           ln1_g, ln1_b, w_ff_up, w_ff_down, ln2_g, ln2_b):
    bsz, seq, _ = x.shape
    assert seq % MIX_T == 0 and MIX_T % CHUNK == 0 and MIX_T % ROW_SUB == 0
    x2 = x.reshape(bsz * seq, D_MODEL)
    w_in_t = jnp.swapaxes(w_in, 1, 2)
    for l in range(DEPTH):
        wg_b = w_gate_up[l].astype(BF16)
        gb = gate_bias[l].reshape(1, D_GLA_K)
        conv_w8 = jnp.pad(conv_w[l], ((0, 8 - conv_w.shape[1]), (0, 0)))

        proj, z_low, w_out_b = _in_proj(x2, w_in_t, w_out, l)
        x1 = _mixer(proj, z_low, x2, w_out_b, conv_w8, conv_norm_g[l].reshape(1, D_CONV),
                    wg_b, gb, gla_norm_g[l].reshape(1, D_GLA_V),
                    ln1_g[l].reshape(1, D_MODEL), ln1_b[l].reshape(1, D_MODEL), bsz, seq)
        x2 = _ffn_ln2(x1, w_ff_up, w_ff_down,
                      ln2_g[l].reshape(1, D_MODEL), ln2_b[l].reshape(1, D_MODEL), l)
    return x2.reshape(bsz, seq, D_MODEL)
```

```python
import functools

import jax
import jax.numpy as jnp
from jax import lax
from jax.experimental import pallas as pl
from jax.experimental.pallas import tpu as pltpu

F32 = jnp.float32
BF16 = jnp.bfloat16

D_MODEL = 2048
D_CONV = 1024
CONV_GROUPS = 8
CONV_GROUP_WIDTH = D_CONV // CONV_GROUPS
GLA_HEADS = 4
HEAD_K = 128
HEAD_V = 256
D_GLA_K = GLA_HEADS * HEAD_K
D_GLA_V = GLA_HEADS * HEAD_V
GATE_RANK = 16
GATE_TAU = 16.0
CHUNK = 64
D_FF = 4 * D_MODEL
LN_EPS = 1e-5
RMS_EPS = 1e-6
DEPTH = 1
DN_ALPHA = (2.0 * DEPTH) ** 0.25

D_PROJ_MAIN = 3 * D_CONV + 2 * D_GLA_K + 2 * D_GLA_V

COL_B, COL_C, COL_U = 0, D_CONV, 2 * D_CONV
COL_Q = 3 * D_CONV
COL_K = COL_Q + D_GLA_K
COL_V = COL_K + D_GLA_K
COL_R = COL_V + D_GLA_V

VMEM_LIMIT = 60 * 1024 * 1024
SUBLANES = 8

PROJ_TM = 1024
PROJ_HEAD_TM = 1024
PROJ_HEAD_TN = 1024
PROJ_TN = 2048
CAST_SLABS = 16
MIX_T = 512
ROW_SUB = 256
OUT_TN = 512
FFN_HEAD_TM = 1024
FFN_HEAD_TF = 512
FFN_TM = 512
FFN_TF = 2048

_NT_DIMS = (((1,), (1,)), ((), ()))
_TN_DIMS = (((0,), (0,)), ((), ()))


def _layer_norm(y, g, b):
    mu = jnp.mean(y, axis=-1, keepdims=True)
    yc = y - mu
    var = jnp.mean(yc * yc, axis=-1, keepdims=True)
    return yc * lax.rsqrt(var + LN_EPS) * g + b


def _in_proj_head_kernel(x_ref, w_ref, wz_ref, o_ref, z_ref, wb_ref, wzb_ref, xb_ref):
    @pl.when(pl.program_id(0) == 0)
    def _():
        xb_ref[...] = x_ref[...].astype(BF16)
        wzb_ref[...] = wz_ref[...].astype(BF16)
        z_ref[...] = lax.dot_general(xb_ref[...], wzb_ref[...], _NT_DIMS,
                                     preferred_element_type=F32).astype(z_ref.dtype)

    wb_ref[...] = w_ref[...].astype(BF16)
    o_ref[...] = lax.dot_general(xb_ref[...], wb_ref[...], _NT_DIMS,
                                 preferred_element_type=F32).astype(o_ref.dtype)


def _in_proj_rest_kernel(x_ref, wb_ref, wzb_ref, pa_ref, za_ref, wo_ref,
                         o_ref, z_ref, wo_b_ref, xb_ref, *, head_tiles):
    i, j = pl.program_id(0), pl.program_id(1)

    wo_b_ref[...] = wo_ref[...].astype(BF16)

    @pl.when(i < head_tiles)
    def _():
        o_ref[...] = pa_ref[...]

        @pl.when(j == 0)
        def _():
            z_ref[...] = za_ref[...]

    @pl.when(i >= head_tiles)
    def _():
        @pl.when(j == 0)
        def _():
            xb_ref[...] = x_ref[...].astype(BF16)
            z_ref[...] = lax.dot_general(xb_ref[...], wzb_ref[...], _NT_DIMS,
                                         preferred_element_type=F32).astype(z_ref.dtype)

        o_ref[...] = lax.dot_general(xb_ref[...], wb_ref[...], _NT_DIMS,
                                     preferred_element_type=F32).astype(o_ref.dtype)


def _in_proj(x2, w_in_t, w_out, layer):
    m = x2.shape[0]
    i0 = PROJ_HEAD_TM // PROJ_TM
    n_i = m // PROJ_TM

    n_jh = D_PROJ_MAIN // PROJ_HEAD_TN
    proj_a, z_a, w_in_b, wz_b = pl.pallas_call(
        _in_proj_head_kernel,
        grid=(n_jh,),
        in_specs=[pl.BlockSpec((PROJ_HEAD_TM, D_MODEL), lambda j: (0, 0), pipeline_mode=pl.Buffered(1)),
                  pl.BlockSpec((None, PROJ_HEAD_TN, D_MODEL), lambda j: (layer, j, 0)),
                  pl.BlockSpec((None, GATE_RANK, D_MODEL),
                               lambda j: (layer, D_PROJ_MAIN // GATE_RANK, 0))],
        out_specs=[pl.BlockSpec((PROJ_HEAD_TM, PROJ_HEAD_TN), lambda j: (0, j)),
                   pl.BlockSpec((PROJ_HEAD_TM, GATE_RANK), lambda j: (0, 0)),
                   pl.BlockSpec((PROJ_HEAD_TN, D_MODEL), lambda j: (j, 0)),
                   pl.BlockSpec((GATE_RANK, D_MODEL), lambda j: (0, 0))],
        out_shape=[jax.ShapeDtypeStruct((PROJ_HEAD_TM, D_PROJ_MAIN), BF16),
                   jax.ShapeDtypeStruct((PROJ_HEAD_TM, GATE_RANK), BF16),
                   jax.ShapeDtypeStruct((D_PROJ_MAIN, D_MODEL), BF16),
                   jax.ShapeDtypeStruct((GATE_RANK, D_MODEL), BF16)],
        scratch_shapes=[pltpu.VMEM((PROJ_HEAD_TM, D_MODEL), BF16)],
        compiler_params=pltpu.CompilerParams(
            dimension_semantics=("arbitrary",), vmem_limit_bytes=VMEM_LIMIT),
        name="in_proj_head",
    )(x2, w_in_t, w_in_t)

    n_j = D_PROJ_MAIN // PROJ_TN
    slab_rows = D_MODEL // CAST_SLABS
    slab = lambda i, j: jnp.minimum(i * n_j + j, CAST_SLABS - 1)
    copying = lambda i: i < i0
    return pl.pallas_call(
        functools.partial(_in_proj_rest_kernel, head_tiles=i0),
        grid=(n_i, n_j),
        in_specs=[pl.BlockSpec((PROJ_TM, D_MODEL), lambda i, j: (jnp.maximum(i, i0), 0)),
                  pl.BlockSpec((PROJ_TN, D_MODEL), lambda i, j: (jnp.where(copying(i), 0, j), 0)),
                  pl.BlockSpec((GATE_RANK, D_MODEL), lambda i, j: (0, 0)),
                  pl.BlockSpec((PROJ_TM, PROJ_TN),
                               lambda i, j: (jnp.minimum(i, i0 - 1), jnp.where(copying(i), j, n_j - 1)),
                               pipeline_mode=pl.Buffered(1)),
                  pl.BlockSpec((PROJ_TM, GATE_RANK), lambda i, j: (jnp.minimum(i, i0 - 1), 0)),
                  pl.BlockSpec((None, slab_rows, D_MODEL), lambda i, j: (layer, slab(i, j), 0))],
        out_specs=[pl.BlockSpec((PROJ_TM, PROJ_TN), lambda i, j: (i, j)),
                   pl.BlockSpec((PROJ_TM, GATE_RANK), lambda i, j: (i, 0)),
                   pl.BlockSpec((slab_rows, D_MODEL), lambda i, j: (slab(i, j), 0))],
        out_shape=[jax.ShapeDtypeStruct((m, D_PROJ_MAIN), BF16),
                   jax.ShapeDtypeStruct((m, GATE_RANK), BF16),
                   jax.ShapeDtypeStruct((D_MODEL, D_MODEL), BF16)],
        scratch_shapes=[pltpu.VMEM((PROJ_TM, D_MODEL), BF16)],
        compiler_params=pltpu.CompilerParams(
            dimension_semantics=("arbitrary", "arbitrary"), vmem_limit_bytes=VMEM_LIMIT),
        name="in_proj_rest",
    )(x2, w_in_b, wz_b, proj_a, z_a, w_out)


def _shift_rows(h3, carry3, shift):
    pos = lax.broadcasted_iota(jnp.int32, h3.shape, 1)
    rolled = pltpu.roll(h3, shift, axis=1)
    rolled_prev = jnp.concatenate([pltpu.roll(carry3, shift, axis=1), rolled[:-1]], axis=0)
    return jnp.where(pos < shift, rolled_prev, rolled)


def _conv_piece(grp, proj_ref, w_ref, g_ref, carry_ref, y_ref):
    ts = proj_ref.shape[0]
    lo = grp * CONV_GROUP_WIDTH
    sl = slice(lo, lo + CONV_GROUP_WIDTH)
    h = (proj_ref[:, COL_C + lo:COL_C + lo + CONV_GROUP_WIDTH].astype(F32)
         * proj_ref[:, COL_U + lo:COL_U + lo + CONV_GROUP_WIDTH].astype(F32))
    h3 = h.reshape(ts // SUBLANES, SUBLANES, CONV_GROUP_WIDTH)
    carry3 = carry_ref[:, sl].reshape(1, SUBLANES, CONV_GROUP_WIDTH)
    h1 = _shift_rows(h3, carry3, 1).reshape(h.shape)
    h2 = _shift_rows(h3, carry3, 2).reshape(h.shape)
    carry_ref[:, sl] = h[ts - SUBLANES:, :]
    w = w_ref[:, sl]
    y = (proj_ref[:, COL_B + lo:COL_B + lo + CONV_GROUP_WIDTH].astype(F32)
         * (w[2:3, :] * h + w[1:2, :] * h1 + w[0:1, :] * h2))
    ms = jnp.mean(y * y, axis=-1, keepdims=True)
    y_ref[:, sl] = (y * lax.rsqrt(ms + RMS_EPS) * g_ref[:, sl]).astype(y_ref.dtype)


def _chunk_cumsum(x):
    t_blk, width = x.shape
    x3 = x.reshape(t_blk // SUBLANES, SUBLANES, width)
    pos = lax.broadcasted_iota(jnp.int32, x3.shape, 1)
    shift = 1
    while shift < SUBLANES:
        x3 = x3 + jnp.where(pos >= shift, pltpu.roll(x3, shift, axis=1), 0.0)
        shift *= 2
    vregs_per_chunk = CHUNK // SUBLANES
    x4 = x3.reshape(t_blk // CHUNK, vregs_per_chunk, SUBLANES, width)
    outs = [x4[:, 0]]
    for j in range(1, vregs_per_chunk):
        outs.append(x4[:, j] + outs[-1][:, SUBLANES - 1:SUBLANES, :])
    return jnp.stack(outs, axis=1).reshape(t_blk, width)


def _gla_prepare(head, proj_ref, z_ref, gb_ref, qd_ref, ki_ref, ke_ref, dct_ref):
    t_blk = proj_ref.shape[0]
    n_chunks = t_blk // CHUNK
    col_q, col_k = COL_Q + head * HEAD_K, COL_K + head * HEAD_K
    z = z_ref[:, head * HEAD_K:(head + 1) * HEAD_K] + gb_ref[:, head * HEAD_K:(head + 1) * HEAD_K]
    log_sig = jnp.minimum(z, 0.0) - jnp.log(1.0 + jnp.exp(-jnp.abs(z)))
    bcum = _chunk_cumsum(log_sig * (1.0 / GATE_TAU))
    qd_ref[head] = ((proj_ref[:, col_q:col_q + HEAD_K].astype(F32) * (HEAD_K ** -0.5))
                    * jnp.exp(bcum)).astype(BF16)
    k_inv_f = proj_ref[:, col_k:col_k + HEAD_K].astype(F32) * jnp.exp(-bcum)
    ki_ref[head] = k_inv_f.astype(BF16)
    decay = jnp.exp(bcum.reshape(n_chunks, CHUNK, HEAD_K)[:, CHUNK - 1:CHUNK, :])
    ke_ref[head] = (k_inv_f.reshape(n_chunks, CHUNK, HEAD_K) * decay).astype(BF16).reshape(
        t_blk, HEAD_K)
    pad = jnp.zeros((HEAD_K - n_chunks, HEAD_K), F32)
    dct_ref[head] = jnp.transpose(jnp.concatenate([decay.reshape(n_chunks, HEAD_K), pad], axis=0))


def _gla_scores(head, c, qd_ref, ki_ref, sc_ref):
    rows = slice(c * CHUNK, (c + 1) * CHUNK)
    causal = (lax.broadcasted_iota(jnp.int32, (CHUNK, CHUNK), 0)
              >= lax.broadcasted_iota(jnp.int32, (CHUNK, CHUNK), 1))
    scores = lax.dot_general(qd_ref[head, rows, :], ki_ref[head, rows, :], _NT_DIMS,
                             preferred_element_type=F32)
    sc_ref[head, rows, :] = jnp.where(causal, scores, 0.0).astype(sc_ref.dtype)


def _gla_state_step(head, c, proj_ref, ke_ref, dct_ref, state_ref, st_ref):
    rows = slice(c * CHUNK, (c + 1) * CHUNK)
    col_v = COL_V + head * HEAD_V
    state = state_ref[head]
    st_ref[head, c] = state.astype(st_ref.dtype)
    delta = lax.dot_general(ke_ref[head, rows, :], proj_ref[rows, col_v:col_v + HEAD_V], _TN_DIMS,
                            preferred_element_type=F32)
    state_ref[head] = dct_ref[head, :, c:c + 1] * state + delta


def _gla_output(head, c, proj_ref, g_ref, qd_ref, sc_ref, st_ref, y_ref):
    rows = slice(c * CHUNK, (c + 1) * CHUNK)
    col_v, col_r = COL_V + head * HEAD_V, COL_R + head * HEAD_V
    col_y = D_CONV + head * HEAD_V
    o = (jnp.dot(sc_ref[head, rows, :], proj_ref[rows, col_v:col_v + HEAD_V],
                 preferred_element_type=F32)
         + jnp.dot(qd_ref[head, rows, :], st_ref[head, c], preferred_element_type=F32))
    ms = jnp.mean(o * o, axis=-1, keepdims=True)
    o_n = o * lax.rsqrt(ms + RMS_EPS) * g_ref[:, head * HEAD_V:(head + 1) * HEAD_V]
    r_c = proj_ref[rows, col_r:col_r + HEAD_V].astype(F32)
    y_ref[rows, col_y:col_y + HEAD_V] = (o_n * (r_c * jax.nn.sigmoid(r_c))).astype(y_ref.dtype)


def _mixer_kernel(proj_ref, zl_ref, x_ref, wo_ref, cw_ref, cg_ref, wg_ref, gb_ref, gg_ref,
                  ln_g_ref, ln_b_ref, o_ref, y_ref, yp_ref, mix_ref, state_ref,
                  carry_ref, qd_ref, ki_ref, ke_ref, dct_ref, z_ref, sc_ref, st_ref, *, blocks_per_seq):
    s = pl.program_id(0)

    @pl.when(s == 0)
    def _():
        y_ref[...] = jnp.zeros_like(y_ref)

    @pl.when(s % blocks_per_seq == 0)
    def _():
        state_ref[...] = jnp.zeros_like(state_ref)
        carry_ref[...] = jnp.zeros_like(carry_ref)

    yp_ref[...] = y_ref[...]

    gla_refs = (qd_ref, ki_ref, ke_ref, dct_ref)
    n_chunks = proj_ref.shape[0] // CHUNK
    vector_pieces = [functools.partial(_conv_piece, grp, proj_ref, cw_ref, cg_ref, carry_ref, y_ref)
                     for grp in range(CONV_GROUPS)]
    vector_pieces += [functools.partial(_gla_prepare, head, proj_ref, z_ref, gb_ref, *gla_refs)
                      for head in range(GLA_HEADS)]
    head_chunks = [(head, c) for c in range(n_chunks) for head in range(GLA_HEADS)]
    chunk_pieces = (
        [functools.partial(_gla_scores, h, c, qd_ref, ki_ref, sc_ref) for h, c in head_chunks]
        + [functools.partial(_gla_state_step, h, c, proj_ref, ke_ref, dct_ref, state_ref, st_ref)
           for h, c in head_chunks]
        + [functools.partial(_gla_output, h, c, proj_ref, gg_ref, qd_ref, sc_ref, st_ref, y_ref)
           for h, c in head_chunks])

    def gate_preactivation():
        z_ref[...] = jnp.dot(zl_ref[...], wg_ref[...], preferred_element_type=F32)

    def out_proj_tile(sub, n):
        rows = slice(sub * ROW_SUB, (sub + 1) * ROW_SUB)
        cols = slice(n * OUT_TN, (n + 1) * OUT_TN)
        mix_ref[rows, cols] = jnp.dot(yp_ref[rows, :], wo_ref[:, cols], preferred_element_type=F32)

    def layer_norm_rows(row0, n_rows):
        rows = slice(row0, row0 + n_rows)
        o_ref[rows, :] = _layer_norm(DN_ALPHA * x_ref[rows, :] + mix_ref[rows, :],
                                     ln_g_ref[...], ln_b_ref[...]).astype(o_ref.dtype)

    tiles_per_sub = D_MODEL // OUT_TN
    n_tiles = (o_ref.shape[0] // ROW_SUB) * tiles_per_sub
    ln_rows = ROW_SUB // 2
    ln_queue = []
    pieces = [gate_preactivation]
    for t in range(n_tiles):
        sub, n = divmod(t, tiles_per_sub)
        lo = len(vector_pieces) * t // n_tiles
        hi = len(vector_pieces) * (t + 1) // n_tiles
        pieces += vector_pieces[lo:hi] + ln_queue[:1] + [functools.partial(out_proj_tile, sub, n)]
        del ln_queue[:1]
        if n == tiles_per_sub - 1:
            ln_queue += [functools.partial(layer_norm_rows, sub * ROW_SUB + r, ln_rows)
                         for r in range(0, ROW_SUB, ln_rows)]
    everything = pieces + chunk_pieces + ln_queue
    mixing = [gate_preactivation] + vector_pieces + chunk_pieces
    last = pl.num_programs(0) - 1

    @pl.when(s == 0)
    def _():
        for piece in mixing:
            piece()

    @pl.when(jnp.logical_and(s > 0, s < last))
    def _():
        for piece in everything:
            piece()

    @pl.when(s == last)
    def _():
        for piece in everything:
            if piece not in mixing:
                piece()


def _mixer(proj, z_low, x2, w_out_b, conv_w8, conv_g, wg_b, gate_bias, gla_g, ln_g, ln_b, bsz, seq):
    m = x2.shape[0]
    n_blk = m // MIX_T
    cur_blk = lambda s: (jnp.minimum(s, n_blk - 1), 0)
    prev_blk = lambda s: (jnp.maximum(s - 1, 0), 0)
    const2 = lambda s: (0, 0)
    return pl.pallas_call(
        functools.partial(_mixer_kernel, blocks_per_seq=seq // MIX_T),
        grid=(n_blk + 1,),
        in_specs=[pl.BlockSpec((MIX_T, D_PROJ_MAIN), cur_blk),
                  pl.BlockSpec((MIX_T, GATE_RANK), cur_blk),
                  pl.BlockSpec((MIX_T, D_MODEL), prev_blk),
                  pl.BlockSpec((D_CONV + D_GLA_V, D_MODEL), const2, pipeline_mode=pl.Buffered(1)),
                  pl.BlockSpec((8, D_CONV), const2),
                  pl.BlockSpec((1, D_CONV), const2),
                  pl.BlockSpec((GATE_RANK, D_GLA_K), const2),
                  pl.BlockSpec((1, D_GLA_K), const2),
                  pl.BlockSpec((1, D_GLA_V), const2),
                  pl.BlockSpec((1, D_MODEL), const2),
                  pl.BlockSpec((1, D_MODEL), const2)],
        out_specs=pl.BlockSpec((MIX_T, D_MODEL), prev_blk),
        out_shape=jax.ShapeDtypeStruct((m, D_MODEL), BF16),
        scratch_shapes=[pltpu.VMEM((MIX_T, D_CONV + D_GLA_V), BF16),
                        pltpu.VMEM((MIX_T, D_CONV + D_GLA_V), BF16),
                        pltpu.VMEM((MIX_T, D_MODEL), F32),
                        pltpu.VMEM((GLA_HEADS, HEAD_K, HEAD_V), F32),
                        pltpu.VMEM((8, D_CONV), F32),
                        pltpu.VMEM((GLA_HEADS, MIX_T, HEAD_K), BF16),
                        pltpu.VMEM((GLA_HEADS, MIX_T, HEAD_K), BF16),
                        pltpu.VMEM((GLA_HEADS, MIX_T, HEAD_K), BF16),
                        pltpu.VMEM((GLA_HEADS, HEAD_K, HEAD_K), F32),
                        pltpu.VMEM((MIX_T, D_GLA_K), F32),
                        pltpu.VMEM((GLA_HEADS, MIX_T, CHUNK), BF16),
                        pltpu.VMEM((GLA_HEADS, MIX_T // CHUNK, HEAD_K, HEAD_V), BF16)],
        compiler_params=pltpu.CompilerParams(
            dimension_semantics=("arbitrary",), vmem_limit_bytes=VMEM_LIMIT),
        name="mixer",
    )(proj, z_low, x2, w_out_b, conv_w8, conv_g, wg_b, gate_bias, gla_g, ln_g, ln_b)


def _ffn_body(f, last, x_ref, up_tile, down_tile, g_ref, b_ref, o_ref):
    def hidden():
        h = jnp.dot(x_ref[...], up_tile(), preferred_element_type=F32)
        h = jnp.maximum(h, 0.0)
        return (h * h).astype(BF16)

    @pl.when(f == 0)
    def _():
        o_ref[...] = jnp.dot(hidden(), down_tile(), preferred_element_type=F32)

    @pl.when(jnp.logical_and(f > 0, f < last))
    def _():
        o_ref[...] += jnp.dot(hidden(), down_tile(), preferred_element_type=F32)

    @pl.when(f == last)
    def _():
        h = hidden()
        wd = down_tile()
        for s in range(o_ref.shape[0] // ROW_SUB):
            rows = slice(s * ROW_SUB, (s + 1) * ROW_SUB)
            ff = o_ref[rows, :] + jnp.dot(h[rows, :], wd, preferred_element_type=F32)
            o_ref[rows, :] = _layer_norm(DN_ALPHA * x_ref[rows, :].astype(F32) + ff,
                                         g_ref[...], b_ref[...])


def _ffn_head_kernel(x_ref, wu_ref, wd_ref, g_ref, b_ref, o_ref, wu_b_ref, wd_b_ref):
    def up_tile():
        wu_b_ref[...] = wu_ref[...].astype(BF16)
        return wu_b_ref[...]

    def down_tile():
        wd_b_ref[...] = wd_ref[...].astype(BF16)
        return wd_b_ref[...]

    _ffn_body(pl.program_id(0), pl.num_programs(0) - 1, x_ref, up_tile, down_tile, g_ref, b_ref, o_ref)


def _ffn_rest_kernel(x_ref, wu_ref, wd_ref, g_ref, b_ref, oa_ref, o_ref, *, head_tiles):
    i, f = pl.program_id(0), pl.program_id(1)
    last = pl.num_programs(1) - 1

    @pl.when(jnp.logical_and(i < head_tiles, f == last))
    def _():
        o_ref[...] = oa_ref[...]

    @pl.when(i >= head_tiles)
    def _():
        _ffn_body(f, last, x_ref, lambda: wu_ref[...], lambda: wd_ref[...], g_ref, b_ref, o_ref)


def _ffn_ln2(x1, w_ff_up, w_ff_down, ln_g, ln_b, layer):
    m = x1.shape[0]

    out_a, w_up_b, w_down_b = pl.pallas_call(
        _ffn_head_kernel,
        grid=(D_FF // FFN_HEAD_TF,),
        in_specs=[pl.BlockSpec((FFN_HEAD_TM, D_MODEL), lambda f: (0, 0), pipeline_mode=pl.Buffered(1)),
                  pl.BlockSpec((None, D_MODEL, FFN_HEAD_TF), lambda f: (layer, 0, f)),
                  pl.BlockSpec((None, FFN_HEAD_TF, D_MODEL), lambda f: (layer, f, 0)),
                  pl.BlockSpec((1, D_MODEL), lambda f: (0, 0)),
                  pl.BlockSpec((1, D_MODEL), lambda f: (0, 0))],
        out_specs=[pl.BlockSpec((FFN_HEAD_TM, D_MODEL), lambda f: (0, 0)),
                   pl.BlockSpec((D_MODEL, FFN_HEAD_TF), lambda f: (0, f)),
                   pl.BlockSpec((FFN_HEAD_TF, D_MODEL), lambda f: (f, 0))],
        out_shape=[jax.ShapeDtypeStruct((FFN_HEAD_TM, D_MODEL), F32),
                   jax.ShapeDtypeStruct((D_MODEL, D_FF), BF16),
                   jax.ShapeDtypeStruct((D_FF, D_MODEL), BF16)],
        compiler_params=pltpu.CompilerParams(
            dimension_semantics=("arbitrary",), vmem_limit_bytes=VMEM_LIMIT),
        name="ffn_head",
    )(x1, w_ff_up, w_ff_down, ln_g, ln_b)

    i0 = FFN_HEAD_TM // FFN_TM
    copying = lambda i: i < i0
    return pl.pallas_call(
        functools.partial(_ffn_rest_kernel, head_tiles=i0),
        grid=(m // FFN_TM, D_FF // FFN_TF),
        in_specs=[pl.BlockSpec((FFN_TM, D_MODEL), lambda i, f: (jnp.maximum(i, i0), 0)),
                  pl.BlockSpec((D_MODEL, FFN_TF), lambda i, f: (0, jnp.where(copying(i), 0, f))),
                  pl.BlockSpec((FFN_TF, D_MODEL), lambda i, f: (jnp.where(copying(i), 0, f), 0)),
                  pl.BlockSpec((1, D_MODEL), lambda i, f: (0, 0)),
                  pl.BlockSpec((1, D_MODEL), lambda i, f: (0, 0)),
                  pl.BlockSpec((FFN_TM, D_MODEL), lambda i, f: (jnp.minimum(i, i0 - 1), 0),
                               pipeline_mode=pl.Buffered(1))],
        out_specs=pl.BlockSpec((FFN_TM, D_MODEL), lambda i, f: (i, 0)),
        out_shape=jax.ShapeDtypeStruct((m, D_MODEL), F32),
        compiler_params=pltpu.CompilerParams(
            dimension_semantics=("arbitrary", "arbitrary"), vmem_limit_bytes=VMEM_LIMIT),
        name="ffn_rest",
    )(x1, w_up_b, w_down_b, ln_g, ln_b, out_a)


def kernel(x, w_in, conv_w, conv_norm_g, w_gate_up, gate_bias, gla_norm_g, w_out,
           ln1_g, ln1_b, w_ff_up, w_ff_down, ln2_g, ln2_b):
    bsz, seq, _ = x.shape
    assert seq % MIX_T == 0 and MIX_T % CHUNK == 0 and MIX_T % ROW_SUB == 0
    x2 = x.reshape(bsz * seq, D_MODEL)
    w_in_t = jnp.swapaxes(w_in, 1, 2)
    for l in range(DEPTH):
        wg_b = w_gate_up[l].astype(BF16)
        gb = gate_bias[l].reshape(1, D_GLA_K)
        conv_w8 = jnp.pad(conv_w[l], ((0, 8 - conv_w.shape[1]), (0, 0)))

        proj, z_low, w_out_b = _in_proj(x2, w_in_t, w_out, l)
        x1 = _mixer(proj, z_low, x2, w_out_b, conv_w8, conv_norm_g[l].reshape(1, D_CONV),
                    wg_b, gb, gla_norm_g[l].reshape(1, D_GLA_V),
                    ln1_g[l].reshape(1, D_MODEL), ln1_b[l].reshape(1, D_MODEL), bsz, seq)
        x2 = _ffn_ln2(x1, w_ff_up, w_ff_down,
                      ln2_g[l].reshape(1, D_MODEL), ln2_b[l].reshape(1, D_MODEL), l)
    return x2.reshape(bsz, seq, D_MODEL)
```

```python
import functools

import jax
import jax.numpy as jnp
from jax import lax
from jax.experimental import pallas as pl
from jax.experimental.pallas import tpu as pltpu

F32 = jnp.float32
BF16 = jnp.bfloat16

D_MODEL = 2048
D_CONV = 1024
CONV_GROUPS = 8
CONV_GROUP_WIDTH = D_CONV // CONV_GROUPS
GLA_HEADS = 4
HEAD_K = 128
HEAD_V = 256
D_GLA_K = GLA_HEADS * HEAD_K
D_GLA_V = GLA_HEADS * HEAD_V
GATE_RANK = 16
GATE_TAU = 16.0
CHUNK = 64
D_FF = 4 * D_MODEL
LN_EPS = 1e-5
RMS_EPS = 1e-6
DEPTH = 1
DN_ALPHA = (2.0 * DEPTH) ** 0.25

D_PROJ_MAIN = 3 * D_CONV + 2 * D_GLA_K + 2 * D_GLA_V

COL_B, COL_C, COL_U = 0, D_CONV, 2 * D_CONV
COL_Q = 3 * D_CONV
COL_K = COL_Q + D_GLA_K
COL_V = COL_K + D_GLA_K
COL_R = COL_V + D_GLA_V

VMEM_LIMIT = 60 * 1024 * 1024
SUBLANES = 8

PROJ_TM = 1024
PROJ_HEAD_TM = 1024
PROJ_HEAD_TN = 1024
PROJ_TN = 2048
CAST_SLABS = 16
MIX_T = 512
ROW_SUB = 256
OUT_TN = 512
FFN_HEAD_TM = 1024
FFN_HEAD_TF = 512
FFN_TM = 512
FFN_TF = 2048

_NT_DIMS = (((1,), (1,)), ((), ()))
_TN_DIMS = (((0,), (0,)), ((), ()))


def _layer_norm(y, g, b):
    mu = jnp.mean(y, axis=-1, keepdims=True)
    yc = y - mu
    var = jnp.mean(yc * yc, axis=-1, keepdims=True)
    return yc * lax.rsqrt(var + LN_EPS) * g + b


def _in_proj_head_kernel(x_ref, w_ref, wz_ref, o_ref, z_ref, wb_ref, wzb_ref, xb_ref):
    @pl.when(pl.program_id(0) == 0)
    def _():
        xb_ref[...] = x_ref[...].astype(BF16)
        wzb_ref[...] = wz_ref[...].astype(BF16)
        z_ref[...] = lax.dot_general(xb_ref[...], wzb_ref[...], _NT_DIMS,
                                     preferred_element_type=F32).astype(z_ref.dtype)

    wb_ref[...] = w_ref[...].astype(BF16)
    o_ref[...] = lax.dot_general(xb_ref[...], wb_ref[...], _NT_DIMS,
                                 preferred_element_type=F32).astype(o_ref.dtype)


def _in_proj_rest_kernel(x_ref, wb_ref, wzb_ref, pa_ref, za_ref, wo_ref,
                         o_ref, z_ref, wo_b_ref, xb_ref, *, head_tiles):
    i, j = pl.program_id(0), pl.program_id(1)

    wo_b_ref[...] = wo_ref[...].astype(BF16)

    @pl.when(i < head_tiles)
    def _():
        o_ref[...] = pa_ref[...]

        @pl.when(j == 0)
        def _():
            z_ref[...] = za_ref[...]

    @pl.when(i >= head_tiles)
    def _():
        @pl.when(j == 0)
        def _():
            xb_ref[...] = x_ref[...].astype(BF16)
            z_ref[...] = lax.dot_general(xb_ref[...], wzb_ref[...], _NT_DIMS,
                                         preferred_element_type=F32).astype(z_ref.dtype)

        o_ref[...] = lax.dot_general(xb_ref[...], wb_ref[...], _NT_DIMS,
                                     preferred_element_type=F32).astype(o_ref.dtype)


def _in_proj(x2, w_in_t, w_out, layer):
    m = x2.shape[0]
    i0 = PROJ_HEAD_TM // PROJ_TM
    n_i = m // PROJ_TM

    n_jh = D_PROJ_MAIN // PROJ_HEAD_TN
    proj_a, z_a, w_in_b, wz_b = pl.pallas_call(
        _in_proj_head_kernel,
        grid=(n_jh,),
        in_specs=[pl.BlockSpec((PROJ_HEAD_TM, D_MODEL), lambda j: (0, 0), pipeline_mode=pl.Buffered(1)),
                  pl.BlockSpec((None, PROJ_HEAD_TN, D_MODEL), lambda j: (layer, j, 0)),
                  pl.BlockSpec((None, GATE_RANK, D_MODEL),
                               lambda j: (layer, D_PROJ_MAIN // GATE_RANK, 0))],
        out_specs=[pl.BlockSpec((PROJ_HEAD_TM, PROJ_HEAD_TN), lambda j: (0, j)),
                   pl.BlockSpec((PROJ_HEAD_TM, GATE_RANK), lambda j: (0, 0)),
                   pl.BlockSpec((PROJ_HEAD_TN, D_MODEL), lambda j: (j, 0)),
                   pl.BlockSpec((GATE_RANK, D_MODEL), lambda j: (0, 0))],
        out_shape=[jax.ShapeDtypeStruct((PROJ_HEAD_TM, D_PROJ_MAIN), BF16),
                   jax.ShapeDtypeStruct((PROJ_HEAD_TM, GATE_RANK), BF16),
                   jax.ShapeDtypeStruct((D_PROJ_MAIN, D_MODEL), BF16),
                   jax.ShapeDtypeStruct((GATE_RANK, D_MODEL), BF16)],
        scratch_shapes=[pltpu.VMEM((PROJ_HEAD_TM, D_MODEL), BF16)],
        compiler_params=pltpu.CompilerParams(
            dimension_semantics=("arbitrary",), vmem_limit_bytes=VMEM_LIMIT),
        name="in_proj_head",
    )(x2, w_in_t, w_in_t)

    n_j = D_PROJ_MAIN // PROJ_TN
    slab_rows = D_MODEL // CAST_SLABS
    slab = lambda i, j: jnp.minimum(i * n_j + j, CAST_SLABS - 1)
    copying = lambda i: i < i0
    return pl.pallas_call(
        functools.partial(_in_proj_rest_kernel, head_tiles=i0),
        grid=(n_i, n_j),
        in_specs=[pl.BlockSpec((PROJ_TM, D_MODEL), lambda i, j: (jnp.maximum(i, i0), 0)),
                  pl.BlockSpec((PROJ_TN, D_MODEL), lambda i, j: (jnp.where(copying(i), 0, j), 0)),
                  pl.BlockSpec((GATE_RANK, D_MODEL), lambda i, j: (0, 0)),
                  pl.BlockSpec((PROJ_TM, PROJ_TN),
                               lambda i, j: (jnp.minimum(i, i0 - 1), jnp.where(copying(i), j, n_j - 1)),
                               pipeline_mode=pl.Buffered(1)),
                  pl.BlockSpec((PROJ_TM, GATE_RANK), lambda i, j: (jnp.minimum(i, i0 - 1), 0)),
                  pl.BlockSpec((None, slab_rows, D_MODEL), lambda i, j: (layer, slab(i, j), 0))],
        out_specs=[pl.BlockSpec((PROJ_TM, PROJ_TN), lambda i, j: (i, j)),
                   pl.BlockSpec((PROJ_TM, GATE_RANK), lambda i, j: (i, 0)),
                   pl.BlockSpec((slab_rows, D_MODEL), lambda i, j: (slab(i, j), 0))],
        out_shape=[jax.ShapeDtypeStruct((m, D_PROJ_MAIN), BF16),
                   jax.ShapeDtypeStruct((m, GATE_RANK), BF16),
                   jax.ShapeDtypeStruct((D_MODEL, D_MODEL), BF16)],
        scratch_shapes=[pltpu.VMEM((PROJ_TM, D_MODEL), BF16)],
        compiler_params=pltpu.CompilerParams(
            dimension_semantics=("arbitrary", "arbitrary"), vmem_limit_bytes=VMEM_LIMIT),
        name="in_proj_rest",
    )(x2, w_in_b, wz_b, proj_a, z_a, w_out)


def _shift_rows(h3, carry3, shift):
    pos = lax.broadcasted_iota(jnp.int32, h3.shape, 1)
    rolled = pltpu.roll(h3, shift, axis=1)
    rolled_prev = jnp.concatenate([pltpu.roll(carry3, shift, axis=1), rolled[:-1]], axis=0)
    return jnp.where(pos < shift, rolled_prev, rolled)


def _conv_piece(grp, proj_ref, w_ref, g_ref, carry_ref, y_ref):
    ts = proj_ref.shape[0]
    lo = grp * CONV_GROUP_WIDTH
    sl = slice(lo, lo + CONV_GROUP_WIDTH)
    h = (proj_ref[:, COL_C + lo:COL_C + lo + CONV_GROUP_WIDTH].astype(F32)
         * proj_ref[:, COL_U + lo:COL_U + lo + CONV_GROUP_WIDTH].astype(F32))
    h3 = h.reshape(ts // SUBLANES, SUBLANES, CONV_GROUP_WIDTH)
    carry3 = carry_ref[:, sl].reshape(1, SUBLANES, CONV_GROUP_WIDTH)
    h1 = _shift_rows(h3, carry3, 1).reshape(h.shape)
    h2 = _shift_rows(h3, carry3, 2).reshape(h.shape)
    carry_ref[:, sl] = h[ts - SUBLANES:, :]
    w = w_ref[:, sl]
    y = (proj_ref[:, COL_B + lo:COL_B + lo + CONV_GROUP_WIDTH].astype(F32)
         * (w[2:3, :] * h + w[1:2, :] * h1 + w[0:1, :] * h2))
    ms = jnp.mean(y * y, axis=-1, keepdims=True)
    y_ref[:, sl] = (y * lax.rsqrt(ms + RMS_EPS) * g_ref[:, sl]).astype(y_ref.dtype)


def _chunk_cumsum(x):
    t_blk, width = x.shape
    x3 = x.reshape(t_blk // SUBLANES, SUBLANES, width)
    pos = lax.broadcasted_iota(jnp.int32, x3.shape, 1)
    shift = 1
    while shift < SUBLANES:
        x3 = x3 + jnp.where(pos >= shift, pltpu.roll(x3, shift, axis=1), 0.0)
        shift *= 2
    vregs_per_chunk = CHUNK // SUBLANES
    x4 = x3.reshape(t_blk // CHUNK, vregs_per_chunk, SUBLANES, width)
    outs = [x4[:, 0]]
    for j in range(1, vregs_per_chunk):
        outs.append(x4[:, j] + outs[-1][:, SUBLANES - 1:SUBLANES, :])
    return jnp.stack(outs, axis=1).reshape(t_blk, width)


def _gla_prepare(head, proj_ref, z_ref, gb_ref, qd_ref, ki_ref, ke_ref, dct_ref):
    t_blk = proj_ref.shape[0]
    n_chunks = t_blk // CHUNK
    col_q, col_k = COL_Q + head * HEAD_K, COL_K + head * HEAD_K
    z = z_ref[:, head * HEAD_K:(head + 1) * HEAD_K] + gb_ref[:, head * HEAD_K:(head + 1) * HEAD_K]
    log_sig = jnp.minimum(z, 0.0) - jnp.log(1.0 + jnp.exp(-jnp.abs(z)))
    bcum = _chunk_cumsum(log_sig * (1.0 / GATE_TAU))
    qd_ref[head] = ((proj_ref[:, col_q:col_q + HEAD_K].astype(F32) * (HEAD_K ** -0.5))
                    * jnp.exp(bcum)).astype(BF16)
    k_inv_f = proj_ref[:, col_k:col_k + HEAD_K].astype(F32) * jnp.exp(-bcum)
    ki_ref[head] = k_inv_f.astype(BF16)
    decay = jnp.exp(bcum.reshape(n_chunks, CHUNK, HEAD_K)[:, CHUNK - 1:CHUNK, :])
    ke_ref[head] = (k_inv_f.reshape(n_chunks, CHUNK, HEAD_K) * decay).astype(BF16).reshape(
        t_blk, HEAD_K)
    pad = jnp.zeros((HEAD_K - n_chunks, HEAD_K), F32)
    dct_ref[head] = jnp.transpose(jnp.concatenate([decay.reshape(n_chunks, HEAD_K), pad], axis=0))


def _gla_scores(head, c, qd_ref, ki_ref, sc_ref):
    rows = slice(c * CHUNK, (c + 1) * CHUNK)
    causal = (lax.broadcasted_iota(jnp.int32, (CHUNK, CHUNK), 0)
              >= lax.broadcasted_iota(jnp.int32, (CHUNK, CHUNK), 1))
    scores = lax.dot_general(qd_ref[head, rows, :], ki_ref[head, rows, :], _NT_DIMS,
                             preferred_element_type=F32)
    sc_ref[head, rows, :] = jnp.where(causal, scores, 0.0).astype(sc_ref.dtype)


def _gla_state_step(head, c, proj_ref, ke_ref, dct_ref, state_ref, st_ref):
    rows = slice(c * CHUNK, (c + 1) * CHUNK)
    col_v = COL_V + head * HEAD_V
    state = state_ref[head]
    st_ref[head, c] = state.astype(st_ref.dtype)
    delta = lax.dot_general(ke_ref[head, rows, :], proj_ref[rows, col_v:col_v + HEAD_V], _TN_DIMS,
                            preferred_element_type=F32)
    state_ref[head] = dct_ref[head, :, c:c + 1] * state + delta


def _gla_output(head, c, proj_ref, g_ref, qd_ref, sc_ref, st_ref, y_ref):
    rows = slice(c * CHUNK, (c + 1) * CHUNK)
    col_v, col_r = COL_V + head * HEAD_V, COL_R + head * HEAD_V
    col_y = D_CONV + head * HEAD_V
    o = (jnp.dot(sc_ref[head, rows, :], proj_ref[rows, col_v:col_v + HEAD_V],
                 preferred_element_type=F32)
         + jnp.dot(qd_ref[head, rows, :], st_ref[head, c], preferred_element_type=F32))
    ms = jnp.mean(o * o, axis=-1, keepdims=True)
    o_n = o * lax.rsqrt(ms + RMS_EPS) * g_ref[:, head * HEAD_V:(head + 1) * HEAD_V]
    r_c = proj_ref[rows, col_r:col_r + HEAD_V].astype(F32)
    y_ref[rows, col_y:col_y + HEAD_V] = (o_n * (r_c * jax.nn.sigmoid(r_c))).astype(y_ref.dtype)


def _mixer_kernel(proj_ref, zl_ref, x_ref, wo_ref, cw_ref, cg_ref, wg_ref, gb_ref, gg_ref,
                  ln_g_ref, ln_b_ref, o_ref, y_ref, yp_ref, mix_ref, state_ref,
                  carry_ref, qd_ref, ki_ref, ke_ref, dct_ref, z_ref, sc_ref, st_ref, *, blocks_per_seq):
    s = pl.program_id(0)

    @pl.when(s % blocks_per_seq == 0)
    def _():
        state_ref[...] = jnp.zeros_like(state_ref)
        carry_ref[...] = jnp.zeros_like(carry_ref)

    def keep_previous_y():
        yp_ref[...] = y_ref[...]

    gla_refs =(qd_ref, ki_ref, ke_ref, dct_ref)
    n_chunks = proj_ref.shape[0] // CHUNK
    vector_pieces = [functools.partial(_conv_piece, grp, proj_ref, cw_ref, cg_ref, carry_ref, y_ref)
                     for grp in range(CONV_GROUPS)]
    vector_pieces += [functools.partial(_gla_prepare, head, proj_ref, z_ref, gb_ref, *gla_refs)
                      for head in range(GLA_HEADS)]
    head_chunks = [(head, c) for c in range(n_chunks) for head in range(GLA_HEADS)]
    chunk_pieces = (
        [functools.partial(_gla_scores, h, c, qd_ref, ki_ref, sc_ref) for h, c in head_chunks]
        + [functools.partial(_gla_state_step, h, c, proj_ref, ke_ref, dct_ref, state_ref, st_ref)
           for h, c in head_chunks]
        + [functools.partial(_gla_output, h, c, proj_ref, gg_ref, qd_ref, sc_ref, st_ref, y_ref)
           for h, c in head_chunks])

    def gate_preactivation():
        z_ref[...] = jnp.dot(zl_ref[...], wg_ref[...], preferred_element_type=F32)

    def out_proj_tile(sub, n):
        rows = slice(sub * ROW_SUB, (sub + 1) * ROW_SUB)
        cols = slice(n * OUT_TN, (n + 1) * OUT_TN)
        mix_ref[rows, cols] = jnp.dot(yp_ref[rows, :], wo_ref[:, cols], preferred_element_type=F32)

    def layer_norm_rows(row0, n_rows):
        rows = slice(row0, row0 + n_rows)
        o_ref[rows, :] = _layer_norm(DN_ALPHA * x_ref[rows, :] + mix_ref[rows, :],
                                     ln_g_ref[...], ln_b_ref[...]).astype(o_ref.dtype)

    tiles_per_sub = D_MODEL // OUT_TN
    n_tiles = (o_ref.shape[0] // ROW_SUB) * tiles_per_sub
    ln_rows = ROW_SUB // 2
    ln_queue = []
    pieces = [gate_preactivation]
    for t in range(n_tiles):
        sub, n = divmod(t, tiles_per_sub)
        lo = len(vector_pieces) * t // n_tiles
        hi = len(vector_pieces) * (t + 1) // n_tiles
        pieces += vector_pieces[lo:hi] + ln_queue[:1] + [functools.partial(out_proj_tile, sub, n)]
        del ln_queue[:1]
        if n == tiles_per_sub - 1:
            ln_queue += [functools.partial(layer_norm_rows, sub * ROW_SUB + r, ln_rows)
                         for r in range(0, ROW_SUB, ln_rows)]
    everything = pieces + chunk_pieces + ln_queue
    mixing = [gate_preactivation] + vector_pieces + chunk_pieces
    last = pl.num_programs(0) - 1

    @pl.when(s == 0)
    def _():
        for piece in mixing:
            piece()

    @pl.when(jnp.logical_and(s > 0, s < last))
    def _():
        for piece in [keep_previous_y] + everything:
            piece()

    @pl.when(s == last)
    def _():
        for piece in [keep_previous_y] + everything:
            if piece not in mixing:
                piece()


def _mixer(proj, z_low, x2, w_out_b, conv_w8, conv_g, wg_b, gate_bias, gla_g, ln_g, ln_b, bsz, seq):
    m = x2.shape[0]
    n_blk = m // MIX_T
    cur_blk = lambda s: (jnp.minimum(s, n_blk - 1), 0)
    prev_blk = lambda s: (jnp.maximum(s - 1, 0), 0)
    const2 = lambda s: (0, 0)
    return pl.pallas_call(
        functools.partial(_mixer_kernel, blocks_per_seq=seq // MIX_T),
        grid=(n_blk + 1,),
        in_specs=[pl.BlockSpec((MIX_T, D_PROJ_MAIN), cur_blk),
                  pl.BlockSpec((MIX_T, GATE_RANK), cur_blk),
                  pl.BlockSpec((MIX_T, D_MODEL), prev_blk),
                  pl.BlockSpec((D_CONV + D_GLA_V, D_MODEL), const2, pipeline_mode=pl.Buffered(1)),
                  pl.BlockSpec((8, D_CONV), const2),
                  pl.BlockSpec((1, D_CONV), const2),
                  pl.BlockSpec((GATE_RANK, D_GLA_K), const2),
                  pl.BlockSpec((1, D_GLA_K), const2),
                  pl.BlockSpec((1, D_GLA_V), const2),
                  pl.BlockSpec((1, D_MODEL), const2),
                  pl.BlockSpec((1, D_MODEL), const2)],
        out_specs=pl.BlockSpec((MIX_T, D_MODEL), prev_blk),
        out_shape=jax.ShapeDtypeStruct((m, D_MODEL), BF16),
        scratch_shapes=[pltpu.VMEM((MIX_T, D_CONV + D_GLA_V), BF16),
                        pltpu.VMEM((MIX_T, D_CONV + D_GLA_V), BF16),
                        pltpu.VMEM((MIX_T, D_MODEL), F32),
                        pltpu.VMEM((GLA_HEADS, HEAD_K, HEAD_V), F32),
                        pltpu.VMEM((8, D_CONV), F32),
                        pltpu.VMEM((GLA_HEADS, MIX_T, HEAD_K), BF16),
                        pltpu.VMEM((GLA_HEADS, MIX_T, HEAD_K), BF16),
                        pltpu.VMEM((GLA_HEADS, MIX_T, HEAD_K), BF16),
                        pltpu.VMEM((GLA_HEADS, HEAD_K, HEAD_K), F32),
                        pltpu.VMEM((MIX_T, D_GLA_K), F32),
                        pltpu.VMEM((GLA_HEADS, MIX_T, CHUNK), BF16),
                        pltpu.VMEM((GLA_HEADS, MIX_T // CHUNK, HEAD_K, HEAD_V), BF16)],
        compiler_params=pltpu.CompilerParams(
            dimension_semantics=("arbitrary",), vmem_limit_bytes=VMEM_LIMIT),
        name="mixer",
    )(proj, z_low, x2, w_out_b, conv_w8, conv_g, wg_b, gate_bias, gla_g, ln_g, ln_b)


def _ffn_body(f, last, x_ref, up_tile, down_tile, g_ref, b_ref, o_ref):
    def hidden():
        h = jnp.dot(x_ref[...], up_tile(), preferred_element_type=F32)
        h = jnp.maximum(h, 0.0)
        return (h * h).astype(BF16)

    @pl.when(f == 0)
    def _():
        o_ref[...] = jnp.dot(hidden(), down_tile(), preferred_element_type=F32)

    @pl.when(jnp.logical_and(f > 0, f < last))
    def _():
        o_ref[...] += jnp.dot(hidden(), down_tile(), preferred_element_type=F32)

    @pl.when(f == last)
    def _():
        h = hidden()
        wd = down_tile()
        for s in range(o_ref.shape[0] // ROW_SUB):
            rows = slice(s * ROW_SUB, (s + 1) * ROW_SUB)
            ff = o_ref[rows, :] + jnp.dot(h[rows, :], wd, preferred_element_type=F32)
            o_ref[rows, :] = _layer_norm(DN_ALPHA * x_ref[rows, :].astype(F32) + ff,
                                         g_ref[...], b_ref[...])


def _ffn_head_kernel(x_ref, wu_ref, wd_ref, g_ref, b_ref, o_ref, wu_b_ref, wd_b_ref):
    def up_tile():
        wu_b_ref[...] = wu_ref[...].astype(BF16)
        return wu_b_ref[...]

    def down_tile():
        wd_b_ref[...] = wd_ref[...].astype(BF16)
        return wd_b_ref[...]

    _ffn_body(pl.program_id(0), pl.num_programs(0) - 1, x_ref, up_tile, down_tile, g_ref, b_ref, o_ref)


def _ffn_rest_kernel(x_ref, wu_ref, wd_ref, g_ref, b_ref, oa_ref, o_ref, *, head_tiles):
    i, f = pl.program_id(0), pl.program_id(1)
    last = pl.num_programs(1) - 1

    @pl.when(jnp.logical_and(i < head_tiles, f == last))
    def _():
        o_ref[...] = oa_ref[...]

    @pl.when(i >= head_tiles)
    def _():
        _ffn_body(f, last, x_ref, lambda: wu_ref[...], lambda: wd_ref[...], g_ref, b_ref, o_ref)


def _ffn_ln2(x1, w_ff_up, w_ff_down, ln_g, ln_b, layer):
    m = x1.shape[0]

    out_a, w_up_b, w_down_b = pl.pallas_call(
        _ffn_head_kernel,
        grid=(D_FF // FFN_HEAD_TF,),
        in_specs=[pl.BlockSpec((FFN_HEAD_TM, D_MODEL), lambda f: (0, 0), pipeline_mode=pl.Buffered(1)),
                  pl.BlockSpec((None, D_MODEL, FFN_HEAD_TF), lambda f: (layer, 0, f)),
                  pl.BlockSpec((None, FFN_HEAD_TF, D_MODEL), lambda f: (layer, f, 0)),
                  pl.BlockSpec((1, D_MODEL), lambda f: (0, 0)),
                  pl.BlockSpec((1, D_MODEL), lambda f: (0, 0))],
        out_specs=[pl.BlockSpec((FFN_HEAD_TM, D_MODEL), lambda f: (0, 0)),
                   pl.BlockSpec((D_MODEL, FFN_HEAD_TF), lambda f: (0, f)),
                   pl.BlockSpec((FFN_HEAD_TF, D_MODEL), lambda f: (f, 0))],
        out_shape=[jax.ShapeDtypeStruct((FFN_HEAD_TM, D_MODEL), F32),
                   jax.ShapeDtypeStruct((D_MODEL, D_FF), BF16),
                   jax.ShapeDtypeStruct((D_FF, D_MODEL), BF16)],
        compiler_params=pltpu.CompilerParams(
            dimension_semantics=("arbitrary",), vmem_limit_bytes=VMEM_LIMIT),
        name="ffn_head",
    )(x1, w_ff_up, w_ff_down, ln_g, ln_b)

    i0 = FFN_HEAD_TM // FFN_TM
    copying = lambda i: i < i0
    return pl.pallas_call(
        functools.partial(_ffn_rest_kernel, head_tiles=i0),
        grid=(m // FFN_TM, D_FF // FFN_TF),
        in_specs=[pl.BlockSpec((FFN_TM, D_MODEL), lambda i, f: (jnp.maximum(i, i0), 0)),
                  pl.BlockSpec((D_MODEL, FFN_TF), lambda i, f: (0, jnp.where(copying(i), 0, f))),
                  pl.BlockSpec((FFN_TF, D_MODEL), lambda i, f: (jnp.where(copying(i), 0, f), 0)),
                  pl.BlockSpec((1, D_MODEL), lambda i, f: (0, 0)),
                  pl.BlockSpec((1, D_MODEL), lambda i, f: (0, 0)),
                  pl.BlockSpec((FFN_TM, D_MODEL), lambda i, f: (jnp.minimum(i, i0 - 1), 0),
                               pipeline_mode=pl.Buffered(1))],
        out_specs=pl.BlockSpec((FFN_TM, D_MODEL), lambda i, f: (i, 0)),
        out_shape=jax.ShapeDtypeStruct((m, D_MODEL), F32),
        compiler_params=pltpu.CompilerParams(
            dimension_semantics=("arbitrary", "arbitrary"), vmem_limit_bytes=VMEM_LIMIT),
        name="ffn_rest",
    )(x1, w_up_b, w_down_b, ln_g, ln_b, out_a)


def kernel(x, w_in, conv_w, conv_norm_g, w_gate_up, gate_bias, gla_norm_g, w_out,
           ln1_g, ln1_b, w_ff_up, w_ff_down, ln2_g, ln2_b):
    bsz, seq, _ = x.shape
    assert seq % MIX_T == 0 and MIX_T % CHUNK == 0 and MIX_T % ROW_SUB == 0
    x2 = x.reshape(bsz * seq, D_MODEL)
    w_in_t = jnp.swapaxes(w_in, 1, 2)
    for l in range(DEPTH):
        wg_b = w_gate_up[l].astype(BF16)
        gb = gate_bias[l].reshape(1, D_GLA_K)
        conv_w8 = jnp.pad(conv_w[l], ((0, 8 - conv_w.shape[1]), (0, 0)))

        proj, z_low, w_out_b = _in_proj(x2, w_in_t, w_out, l)
        x1 = _mixer(proj, z_low, x2, w_out_b, conv_w8, conv_norm_g[l].reshape(1, D_CONV),
                    wg_b, gb, gla_norm_g[l].reshape(1, D_GLA_V),
                    ln1_g[l].reshape(1, D_MODEL), ln1_b[l].reshape(1, D_MODEL), bsz, seq)
        x2 = _ffn_ln2(x1, w_ff_up, w_ff_down,
                      ln2_g[l].reshape(1, D_MODEL), ln2_b[l].reshape(1, D_MODEL), l)
    return x2.reshape(bsz, seq, D_MODEL)
```

```python
import functools

import jax
import jax.numpy as jnp
from jax import lax
from jax.experimental import pallas as pl
from jax.experimental.pallas import tpu as pltpu

F32 = jnp.float32
BF16 = jnp.bfloat16

D_MODEL = 2048
D_CONV = 1024
CONV_GROUPS = 8
CONV_GROUP_WIDTH = D_CONV // CONV_GROUPS
GLA_HEADS = 4
HEAD_K = 128
HEAD_V = 256
D_GLA_K = GLA_HEADS * HEAD_K
D_GLA_V = GLA_HEADS * HEAD_V
GATE_RANK = 16
GATE_TAU = 16.0
CHUNK = 64
D_FF = 4 * D_MODEL
LN_EPS = 1e-5
RMS_EPS = 1e-6
DEPTH = 1
DN_ALPHA = (2.0 * DEPTH) ** 0.25

D_PROJ_MAIN = 3 * D_CONV + 2 * D_GLA_K + 2 * D_GLA_V

COL_B, COL_C, COL_U = 0, D_CONV, 2 * D_CONV
COL_Q = 3 * D_CONV
COL_K = COL_Q + D_GLA_K
COL_V = COL_K + D_GLA_K
COL_R = COL_V + D_GLA_V

VMEM_LIMIT = 60 * 1024 * 1024
SUBLANES = 8

PROJ_TM = 1024
PROJ_HEAD_TM = 1024
PROJ_HEAD_TN = 1024
PROJ_TN = 2048
CAST_SLABS = 16
MIX_T = 512
ROW_SUB = 256
OUT_TN = 512
FFN_HEAD_TM = 1024
FFN_HEAD_TF = 512
FFN_TM = 512
FFN_TF = 2048

_NT_DIMS = (((1,), (1,)), ((), ()))
_TN_DIMS = (((0,), (0,)), ((), ()))


def _layer_norm(y, g, b):
    mu = jnp.mean(y, axis=-1, keepdims=True)
    yc = y - mu
    var = jnp.mean(yc * yc, axis=-1, keepdims=True)
    return yc * lax.rsqrt(var + LN_EPS) * g + b


def _in_proj_head_kernel(x_ref, w_ref, wz_ref, o_ref, z_ref, wb_ref, wzb_ref, xb_ref):
    @pl.when(pl.program_id(0) == 0)
    def _():
        xb_ref[...] = x_ref[...].astype(BF16)
        wzb_ref[...] = wz_ref[...].astype(BF16)
        z_ref[...] = lax.dot_general(xb_ref[...], wzb_ref[...], _NT_DIMS,
                                     preferred_element_type=F32).astype(z_ref.dtype)

    wb_ref[...] = w_ref[...].astype(BF16)
    o_ref[...] = lax.dot_general(xb_ref[...], wb_ref[...], _NT_DIMS,
                                 preferred_element_type=F32).astype(o_ref.dtype)


def _in_proj_rest_kernel(x_ref, wb_ref, wzb_ref, pa_ref, za_ref, wo_ref,
                         o_ref, z_ref, wo_b_ref, xb_ref, *, head_tiles):
    i, j = pl.program_id(0), pl.program_id(1)

    def cast_w_out_slab():
        wo_b_ref[...] = wo_ref[...].astype(BF16)

    @pl.when(i < head_tiles)
    def _():
        @pl.when(j == 0)
        def _():
            z_ref[...] = za_ref[...]

        cast_w_out_slab()
        o_ref[...] = pa_ref[...]

    @pl.when(i >= head_tiles)
    def _():
        @pl.when(j == 0)
        def _():
            xb_ref[...] = x_ref[...].astype(BF16)
            z_ref[...] = lax.dot_general(xb_ref[...], wzb_ref[...], _NT_DIMS,
                                         preferred_element_type=F32).astype(z_ref.dtype)

        cast_w_out_slab()
        o_ref[...] = lax.dot_general(xb_ref[...], wb_ref[...], _NT_DIMS,
                                     preferred_element_type=F32).astype(o_ref.dtype)


def _in_proj(x2, w_in_t, w_out, layer):
    m = x2.shape[0]
    i0 = PROJ_HEAD_TM // PROJ_TM
    n_i = m // PROJ_TM

    n_jh = D_PROJ_MAIN // PROJ_HEAD_TN
    proj_a, z_a, w_in_b, wz_b = pl.pallas_call(
        _in_proj_head_kernel,
        grid=(n_jh,),
        in_specs=[pl.BlockSpec((PROJ_HEAD_TM, D_MODEL), lambda j: (0, 0), pipeline_mode=pl.Buffered(1)),
                  pl.BlockSpec((None, PROJ_HEAD_TN, D_MODEL), lambda j: (layer, j, 0)),
                  pl.BlockSpec((None, GATE_RANK, D_MODEL),
                               lambda j: (layer, D_PROJ_MAIN // GATE_RANK, 0))],
        out_specs=[pl.BlockSpec((PROJ_HEAD_TM, PROJ_HEAD_TN), lambda j: (0, j)),
                   pl.BlockSpec((PROJ_HEAD_TM, GATE_RANK), lambda j: (0, 0)),
                   pl.BlockSpec((PROJ_HEAD_TN, D_MODEL), lambda j: (j, 0)),
                   pl.BlockSpec((GATE_RANK, D_MODEL), lambda j: (0, 0))],
        out_shape=[jax.ShapeDtypeStruct((PROJ_HEAD_TM, D_PROJ_MAIN), BF16),
                   jax.ShapeDtypeStruct((PROJ_HEAD_TM, GATE_RANK), BF16),
                   jax.ShapeDtypeStruct((D_PROJ_MAIN, D_MODEL), BF16),
                   jax.ShapeDtypeStruct((GATE_RANK, D_MODEL), BF16)],
        scratch_shapes=[pltpu.VMEM((PROJ_HEAD_TM, D_MODEL), BF16)],
        compiler_params=pltpu.CompilerParams(
            dimension_semantics=("arbitrary",), vmem_limit_bytes=VMEM_LIMIT),
        name="in_proj_head",
    )(x2, w_in_t, w_in_t)

    n_j = D_PROJ_MAIN // PROJ_TN
    slab_rows = D_MODEL // CAST_SLABS
    slab = lambda i, j: jnp.minimum(i * n_j + j, CAST_SLABS - 1)
    copying = lambda i: i < i0
    return pl.pallas_call(
        functools.partial(_in_proj_rest_kernel, head_tiles=i0),
        grid=(n_i, n_j),
        in_specs=[pl.BlockSpec((PROJ_TM, D_MODEL), lambda i, j: (jnp.maximum(i, i0), 0)),
                  pl.BlockSpec((PROJ_TN, D_MODEL), lambda i, j: (jnp.where(copying(i), 0, j), 0)),
                  pl.BlockSpec((GATE_RANK, D_MODEL), lambda i, j: (0, 0)),
                  pl.BlockSpec((PROJ_TM, PROJ_TN),
                               lambda i, j: (jnp.minimum(i, i0 - 1), jnp.where(copying(i), j, n_j - 1)),
                               pipeline_mode=pl.Buffered(1)),
                  pl.BlockSpec((PROJ_TM, GATE_RANK), lambda i, j: (jnp.minimum(i, i0 - 1), 0)),
                  pl.BlockSpec((None, slab_rows, D_MODEL), lambda i, j: (layer, slab(i, j), 0))],
        out_specs=[pl.BlockSpec((PROJ_TM, PROJ_TN), lambda i, j: (i, j)),
                   pl.BlockSpec((PROJ_TM, GATE_RANK), lambda i, j: (i, 0)),
                   pl.BlockSpec((slab_rows, D_MODEL), lambda i, j: (slab(i, j), 0))],
        out_shape=[jax.ShapeDtypeStruct((m, D_PROJ_MAIN), BF16),
                   jax.ShapeDtypeStruct((m, GATE_RANK), BF16),
                   jax.ShapeDtypeStruct((D_MODEL, D_MODEL), BF16)],
        scratch_shapes=[pltpu.VMEM((PROJ_TM, D_MODEL), BF16)],
        compiler_params=pltpu.CompilerParams(
            dimension_semantics=("arbitrary", "arbitrary"), vmem_limit_bytes=VMEM_LIMIT),
        name="in_proj_rest",
    )(x2, w_in_b, wz_b, proj_a, z_a, w_out)


def _shift_rows(h3, carry3, shift):
    pos = lax.broadcasted_iota(jnp.int32, h3.shape, 1)
    rolled = pltpu.roll(h3, shift, axis=1)
    rolled_prev = jnp.concatenate([pltpu.roll(carry3, shift, axis=1), rolled[:-1]], axis=0)
    return jnp.where(pos < shift, rolled_prev, rolled)


def _conv_piece(grp, proj_ref, w_ref, g_ref, carry_ref, y_ref):
    ts = proj_ref.shape[0]
    lo = grp * CONV_GROUP_WIDTH
    sl = slice(lo, lo + CONV_GROUP_WIDTH)
    h = (proj_ref[:, COL_C + lo:COL_C + lo + CONV_GROUP_WIDTH].astype(F32)
         * proj_ref[:, COL_U + lo:COL_U + lo + CONV_GROUP_WIDTH].astype(F32))
    h3 = h.reshape(ts // SUBLANES, SUBLANES, CONV_GROUP_WIDTH)
    carry3 = carry_ref[:, sl].reshape(1, SUBLANES, CONV_GROUP_WIDTH)
    h1 = _shift_rows(h3, carry3, 1).reshape(h.shape)
    h2 = _shift_rows(h3, carry3, 2).reshape(h.shape)
    carry_ref[:, sl] = h[ts - SUBLANES:, :]
    w = w_ref[:, sl]
    y = (proj_ref[:, COL_B + lo:COL_B + lo + CONV_GROUP_WIDTH].astype(F32)
         * (w[2:3, :] * h + w[1:2, :] * h1 + w[0:1, :] * h2))
    ms = jnp.mean(y * y, axis=-1, keepdims=True)
    y_ref[:, sl] = (y * lax.rsqrt(ms + RMS_EPS) * g_ref[:, sl]).astype(y_ref.dtype)


def _chunk_cumsum(x):
    t_blk, width = x.shape
    x3 = x.reshape(t_blk // SUBLANES, SUBLANES, width)
    pos = lax.broadcasted_iota(jnp.int32, x3.shape, 1)
    shift = 1
    while shift < SUBLANES:
        x3 = x3 + jnp.where(pos >= shift, pltpu.roll(x3, shift, axis=1), 0.0)
        shift *= 2
    vregs_per_chunk = CHUNK // SUBLANES
    x4 = x3.reshape(t_blk // CHUNK, vregs_per_chunk, SUBLANES, width)
    outs = [x4[:, 0]]
    for j in range(1, vregs_per_chunk):
        outs.append(x4[:, j] + outs[-1][:, SUBLANES - 1:SUBLANES, :])
    return jnp.stack(outs, axis=1).reshape(t_blk, width)


def _gla_prepare(head, proj_ref, z_ref, gb_ref, qd_ref, ki_ref, ke_ref, dct_ref):
    t_blk = proj_ref.shape[0]
    n_chunks = t_blk // CHUNK
    col_q, col_k = COL_Q + head * HEAD_K, COL_K + head * HEAD_K
    z = z_ref[:, head * HEAD_K:(head + 1) * HEAD_K] + gb_ref[:, head * HEAD_K:(head + 1) * HEAD_K]
    log_sig = jnp.minimum(z, 0.0) - jnp.log(1.0 + jnp.exp(-jnp.abs(z)))
    bcum = _chunk_cumsum(log_sig * (1.0 / GATE_TAU))
    qd_ref[head] = ((proj_ref[:, col_q:col_q + HEAD_K].astype(F32) * (HEAD_K ** -0.5))
                    * jnp.exp(bcum)).astype(BF16)
    k_inv_f = proj_ref[:, col_k:col_k + HEAD_K].astype(F32) * jnp.exp(-bcum)
    ki_ref[head] = k_inv_f.astype(BF16)
    decay = jnp.exp(bcum.reshape(n_chunks, CHUNK, HEAD_K)[:, CHUNK - 1:CHUNK, :])
    ke_ref[head] = (k_inv_f.reshape(n_chunks, CHUNK, HEAD_K) * decay).astype(BF16).reshape(
        t_blk, HEAD_K)
    pad = jnp.zeros((HEAD_K - n_chunks, HEAD_K), F32)
    dct_ref[head] = jnp.transpose(jnp.concatenate([decay.reshape(n_chunks, HEAD_K), pad], axis=0))


def _gla_scores(head, c, qd_ref, ki_ref, sc_ref):
    rows = slice(c * CHUNK, (c + 1) * CHUNK)
    causal = (lax.broadcasted_iota(jnp.int32, (CHUNK, CHUNK), 0)
              >= lax.broadcasted_iota(jnp.int32, (CHUNK, CHUNK), 1))
    scores = lax.dot_general(qd_ref[head, rows, :], ki_ref[head, rows, :], _NT_DIMS,
                             preferred_element_type=F32)
    sc_ref[head, rows, :] = jnp.where(causal, scores, 0.0).astype(sc_ref.dtype)


def _gla_state_step(head, c, proj_ref, ke_ref, dct_ref, state_ref, st_ref):
    rows = slice(c * CHUNK, (c + 1) * CHUNK)
    col_v = COL_V + head * HEAD_V
    state = state_ref[head]
    st_ref[head, c] = state.astype(st_ref.dtype)
    delta = lax.dot_general(ke_ref[head, rows, :], proj_ref[rows, col_v:col_v + HEAD_V], _TN_DIMS,
                            preferred_element_type=F32)
    state_ref[head] = dct_ref[head, :, c:c + 1] * state + delta


def _gla_output(head, c, proj_ref, g_ref, qd_ref, sc_ref, st_ref, y_ref):
    rows = slice(c * CHUNK, (c + 1) * CHUNK)
    col_v, col_r = COL_V + head * HEAD_V, COL_R + head * HEAD_V
    col_y = D_CONV + head * HEAD_V
    o = (jnp.dot(sc_ref[head, rows, :], proj_ref[rows, col_v:col_v + HEAD_V],
                 preferred_element_type=F32)
         + jnp.dot(qd_ref[head, rows, :], st_ref[head, c], preferred_element_type=F32))
    ms = jnp.mean(o * o, axis=-1, keepdims=True)
    o_n = o * lax.rsqrt(ms + RMS_EPS) * g_ref[:, head * HEAD_V:(head + 1) * HEAD_V]
    r_c = proj_ref[rows, col_r:col_r + HEAD_V].astype(F32)
    y_ref[rows, col_y:col_y + HEAD_V] = (o_n * (r_c * jax.nn.sigmoid(r_c))).astype(y_ref.dtype)


def _mixer_kernel(proj_ref, zl_ref, x_ref, wo_ref, cw_ref, cg_ref, wg_ref, gb_ref, gg_ref,
                  ln_g_ref, ln_b_ref, o_ref, y_ref, yp_ref, mix_ref, state_ref,
                  carry_ref, qd_ref, ki_ref, ke_ref, dct_ref, z_ref, sc_ref, st_ref, *, blocks_per_seq):
    s = pl.program_id(0)

    @pl.when(s % blocks_per_seq == 0)
    def _():
        state_ref[...] = jnp.zeros_like(state_ref)
        carry_ref[...] = jnp.zeros_like(carry_ref)

    def keep_previous_y():
        yp_ref[...] = y_ref[...]

    gla_refs =(qd_ref, ki_ref, ke_ref, dct_ref)
    n_chunks = proj_ref.shape[0] // CHUNK
    vector_pieces = [functools.partial(_conv_piece, grp, proj_ref, cw_ref, cg_ref, carry_ref, y_ref)
                     for grp in range(CONV_GROUPS)]
    vector_pieces += [functools.partial(_gla_prepare, head, proj_ref, z_ref, gb_ref, *gla_refs)
                      for head in range(GLA_HEADS)]
    head_chunks = [(head, c) for c in range(n_chunks) for head in range(GLA_HEADS)]
    chunk_pieces = (
        [functools.partial(_gla_scores, h, c, qd_ref, ki_ref, sc_ref) for h, c in head_chunks]
        + [functools.partial(_gla_state_step, h, c, proj_ref, ke_ref, dct_ref, state_ref, st_ref)
           for h, c in head_chunks]
        + [functools.partial(_gla_output, h, c, proj_ref, gg_ref, qd_ref, sc_ref, st_ref, y_ref)
           for h, c in head_chunks])

    def gate_preactivation():
        z_ref[...] = jnp.dot(zl_ref[...], wg_ref[...], preferred_element_type=F32)

    def out_proj_tile(sub, n):
        rows = slice(sub * ROW_SUB, (sub + 1) * ROW_SUB)
        cols = slice(n * OUT_TN, (n + 1) * OUT_TN)
        mix_ref[rows, cols] = jnp.dot(yp_ref[rows, :], wo_ref[:, cols], preferred_element_type=F32)

    def layer_norm_rows(row0, n_rows):
        rows = slice(row0, row0 + n_rows)
        o_ref[rows, :] = _layer_norm(DN_ALPHA * x_ref[rows, :] + mix_ref[rows, :],
                                     ln_g_ref[...], ln_b_ref[...]).astype(o_ref.dtype)

    tiles_per_sub = D_MODEL // OUT_TN
    n_tiles = (o_ref.shape[0] // ROW_SUB) * tiles_per_sub
    ln_rows = ROW_SUB // 2
    ln_queue = []
    pieces = [gate_preactivation]
    for t in range(n_tiles):
        sub, n = divmod(t, tiles_per_sub)
        lo = len(vector_pieces) * t // n_tiles
        hi = len(vector_pieces) * (t + 1) // n_tiles
        pieces += vector_pieces[lo:hi] + ln_queue[:1] + [functools.partial(out_proj_tile, sub, n)]
        del ln_queue[:1]
        if n == tiles_per_sub - 1:
            ln_queue += [functools.partial(layer_norm_rows, sub * ROW_SUB + r, ln_rows)
                         for r in range(0, ROW_SUB, ln_rows)]
    everything = pieces + chunk_pieces + ln_queue
    mixing = [gate_preactivation] + vector_pieces + chunk_pieces
    last = pl.num_programs(0) - 1

    @pl.when(s == 0)
    def _():
        for piece in mixing:
            piece()

    @pl.when(jnp.logical_and(s > 0, s < last))
    def _():
        for piece in [keep_previous_y] + everything:
            piece()

    @pl.when(s == last)
    def _():
        for piece in [keep_previous_y] + everything:
            if piece not in mixing:
                piece()


def _mixer(proj, z_low, x2, w_out_b, conv_w8, conv_g, wg_b, gate_bias, gla_g, ln_g, ln_b, bsz, seq):
    m = x2.shape[0]
    n_blk = m // MIX_T
    cur_blk = lambda s: (jnp.minimum(s, n_blk - 1), 0)
    prev_blk = lambda s: (jnp.maximum(s - 1, 0), 0)
    const2 = lambda s: (0, 0)
    return pl.pallas_call(
        functools.partial(_mixer_kernel, blocks_per_seq=seq // MIX_T),
        grid=(n_blk + 1,),
        in_specs=[pl.BlockSpec((MIX_T, D_PROJ_MAIN), cur_blk),
                  pl.BlockSpec((MIX_T, GATE_RANK), cur_blk),
                  pl.BlockSpec((MIX_T, D_MODEL), prev_blk),
                  pl.BlockSpec((D_CONV + D_GLA_V, D_MODEL), const2, pipeline_mode=pl.Buffered(1)),
                  pl.BlockSpec((8, D_CONV), const2),
                  pl.BlockSpec((1, D_CONV), const2),
                  pl.BlockSpec((GATE_RANK, D_GLA_K), const2),
                  pl.BlockSpec((1, D_GLA_K), const2),
                  pl.BlockSpec((1, D_GLA_V), const2),
                  pl.BlockSpec((1, D_MODEL), const2),
                  pl.BlockSpec((1, D_MODEL), const2)],
        out_specs=pl.BlockSpec((MIX_T, D_MODEL), prev_blk),
        out_shape=jax.ShapeDtypeStruct((m, D_MODEL), BF16),
        scratch_shapes=[pltpu.VMEM((MIX_T, D_CONV + D_GLA_V), BF16),
                        pltpu.VMEM((MIX_T, D_CONV + D_GLA_V), BF16),
                        pltpu.VMEM((MIX_T, D_MODEL), F32),
                        pltpu.VMEM((GLA_HEADS, HEAD_K, HEAD_V), F32),
                        pltpu.VMEM((8, D_CONV), F32),
                        pltpu.VMEM((GLA_HEADS, MIX_T, HEAD_K), BF16),
                        pltpu.VMEM((GLA_HEADS, MIX_T, HEAD_K), BF16),
                        pltpu.VMEM((GLA_HEADS, MIX_T, HEAD_K), BF16),
                        pltpu.VMEM((GLA_HEADS, HEAD_K, HEAD_K), F32),
                        pltpu.VMEM((MIX_T, D_GLA_K), F32),
                        pltpu.VMEM((GLA_HEADS, MIX_T, CHUNK), BF16),
                        pltpu.VMEM((GLA_HEADS, MIX_T // CHUNK, HEAD_K, HEAD_V), BF16)],
        compiler_params=pltpu.CompilerParams(
            dimension_semantics=("arbitrary",), vmem_limit_bytes=VMEM_LIMIT),
        name="mixer",
    )(proj, z_low, x2, w_out_b, conv_w8, conv_g, wg_b, gate_bias, gla_g, ln_g, ln_b)


def _ffn_body(f, last, x_ref, up_tile, down_tile, g_ref, b_ref, o_ref):
    def hidden():
        h = jnp.dot(x_ref[...], up_tile(), preferred_element_type=F32)
        h = jnp.maximum(h, 0.0)
        return (h * h).astype(BF16)

    @pl.when(f == 0)
    def _():
        o_ref[...] = jnp.dot(hidden(), down_tile(), preferred_element_type=F32)

    @pl.when(jnp.logical_and(f > 0, f < last))
    def _():
        o_ref[...] += jnp.dot(hidden(), down_tile(), preferred_element_type=F32)

    @pl.when(f == last)
    def _():
        h = hidden()
        wd = down_tile()
        for s in range(o_ref.shape[0] // ROW_SUB):
            rows = slice(s * ROW_SUB, (s + 1) * ROW_SUB)
            ff = o_ref[rows, :] + jnp.dot(h[rows, :], wd, preferred_element_type=F32)
            o_ref[rows, :] = _layer_norm(DN_ALPHA * x_ref[rows, :].astype(F32) + ff,
                                         g_ref[...], b_ref[...])


def _ffn_head_kernel(x_ref, wu_ref, wd_ref, g_ref, b_ref, o_ref, wu_b_ref, wd_b_ref):
    def up_tile():
        wu_b_ref[...] = wu_ref[...].astype(BF16)
        return wu_b_ref[...]

    def down_tile():
        wd_b_ref[...] = wd_ref[...].astype(BF16)
        return wd_b_ref[...]

    _ffn_body(pl.program_id(0), pl.num_programs(0) - 1, x_ref, up_tile, down_tile, g_ref, b_ref, o_ref)


def _ffn_rest_kernel(x_ref, wu_ref, wd_ref, g_ref, b_ref, oa_ref, o_ref, *, head_tiles):
    i, f = pl.program_id(0), pl.program_id(1)
    last = pl.num_programs(1) - 1

    @pl.when(jnp.logical_and(i < head_tiles, f == last))
    def _():
        o_ref[...] = oa_ref[...]

    @pl.when(i >= head_tiles)
    def _():
        _ffn_body(f, last, x_ref, lambda: wu_ref[...], lambda: wd_ref[...], g_ref, b_ref, o_ref)


def _ffn_ln2(x1, w_ff_up, w_ff_down, ln_g, ln_b, layer):
    m = x1.shape[0]

    out_a, w_up_b, w_down_b = pl.pallas_call(
        _ffn_head_kernel,
        grid=(D_FF // FFN_HEAD_TF,),
        in_specs=[pl.BlockSpec((FFN_HEAD_TM, D_MODEL), lambda f: (0, 0), pipeline_mode=pl.Buffered(1)),
                  pl.BlockSpec((None, D_MODEL, FFN_HEAD_TF), lambda f: (layer, 0, f)),
                  pl.BlockSpec((None, FFN_HEAD_TF, D_MODEL), lambda f: (layer, f, 0)),
                  pl.BlockSpec((1, D_MODEL), lambda f: (0, 0)),
                  pl.BlockSpec((1, D_MODEL), lambda f: (0, 0))],
        out_specs=[pl.BlockSpec((FFN_HEAD_TM, D_MODEL), lambda f: (0, 0)),
                   pl.BlockSpec((D_MODEL, FFN_HEAD_TF), lambda f: (0, f)),
                   pl.BlockSpec((FFN_HEAD_TF, D_MODEL), lambda f: (f, 0))],
        out_shape=[jax.ShapeDtypeStruct((FFN_HEAD_TM, D_MODEL), F32),
                   jax.ShapeDtypeStruct((D_MODEL, D_FF), BF16),
                   jax.ShapeDtypeStruct((D_FF, D_MODEL), BF16)],
        compiler_params=pltpu.CompilerParams(
            dimension_semantics=("arbitrary",), vmem_limit_bytes=VMEM_LIMIT),
        name="ffn_head",
    )(x1, w_ff_up, w_ff_down, ln_g, ln_b)

    i0 = FFN_HEAD_TM // FFN_TM
    copying = lambda i: i < i0
    return pl.pallas_call(
        functools.partial(_ffn_rest_kernel, head_tiles=i0),
        grid=(m // FFN_TM, D_FF // FFN_TF),
        in_specs=[pl.BlockSpec((FFN_TM, D_MODEL), lambda i, f: (jnp.maximum(i, i0), 0)),
                  pl.BlockSpec((D_MODEL, FFN_TF), lambda i, f: (0, jnp.where(copying(i), 0, f))),
                  pl.BlockSpec((FFN_TF, D_MODEL), lambda i, f: (jnp.where(copying(i), 0, f), 0)),
                  pl.BlockSpec((1, D_MODEL), lambda i, f: (0, 0)),
                  pl.BlockSpec((1, D_MODEL), lambda i, f: (0, 0)),
                  pl.BlockSpec((FFN_TM, D_MODEL), lambda i, f: (jnp.minimum(i, i0 - 1), 0),
                               pipeline_mode=pl.Buffered(1))],
        out_specs=pl.BlockSpec((FFN_TM, D_MODEL), lambda i, f: (i, 0)),
        out_shape=jax.ShapeDtypeStruct((m, D_MODEL), F32),
        compiler_params=pltpu.CompilerParams(
            dimension_semantics=("arbitrary", "arbitrary"), vmem_limit_bytes=VMEM_LIMIT),
        name="ffn_rest",
    )(x1, w_up_b, w_down_b, ln_g, ln_b, out_a)


def kernel(x, w_in, conv_w, conv_norm_g, w_gate_up, gate_bias, gla_norm_g, w_out,
           ln1_g, ln1_b, w_ff_up, w_ff_down, ln2_g, ln2_b):
    bsz, seq, _ = x.shape
    assert seq % MIX_T == 0 and MIX_T % CHUNK == 0 and MIX_T % ROW_SUB == 0
    x2 = x.reshape(bsz * seq, D_MODEL)
    w_in_t = jnp.swapaxes(w_in, 1, 2)
    for l in range(DEPTH):
        wg_b = w_gate_up[l].astype(BF16)
        gb = gate_bias[l].reshape(1, D_GLA_K)
        conv_w8 = jnp.pad(conv_w[l], ((0, 8 - conv_w.shape[1]), (0, 0)))

        proj, z_low, w_out_b = _in_proj(x2, w_in_t, w_out, l)
        x1 = _mixer(proj, z_low, x2, w_out_b, conv_w8, conv_norm_g[l].reshape(1, D_CONV),
                    wg_b, gb, gla_norm_g[l].reshape(1, D_GLA_V),
                    ln1_g[l].reshape(1, D_MODEL), ln1_b[l].reshape(1, D_MODEL), bsz, seq)
        x2 = _ffn_ln2(x1, w_ff_up, w_ff_down,
                      ln2_g[l].reshape(1, D_MODEL), ln2_b[l].reshape(1, D_MODEL), l)
    return x2.reshape(bsz, seq, D_MODEL)
```

```python
import functools

import jax
import jax.numpy as jnp
from jax import lax
from jax.experimental import pallas as pl
from jax.experimental.pallas import tpu as pltpu

F32 = jnp.float32
BF16 = jnp.bfloat16

D_MODEL = 2048
D_CONV = 1024
CONV_GROUPS = 8
CONV_GROUP_WIDTH = D_CONV // CONV_GROUPS
GLA_HEADS = 4
HEAD_K = 128
HEAD_V = 256
D_GLA_K = GLA_HEADS * HEAD_K
D_GLA_V = GLA_HEADS * HEAD_V
GATE_RANK = 16
GATE_TAU = 16.0
CHUNK = 64
D_FF = 4 * D_MODEL
LN_EPS = 1e-5
RMS_EPS = 1e-6
DEPTH = 1
DN_ALPHA = (2.0 * DEPTH) ** 0.25

D_PROJ_MAIN = 3 * D_CONV + 2 * D_GLA_K + 2 * D_GLA_V

COL_B, COL_C, COL_U = 0, D_CONV, 2 * D_CONV
COL_Q = 3 * D_CONV
COL_K = COL_Q + D_GLA_K
COL_V = COL_K + D_GLA_K
COL_R = COL_V + D_GLA_V

VMEM_LIMIT = 60 * 1024 * 1024
SUBLANES = 8

PROJ_TM = 1024
PROJ_HEAD_TM = 1024
PROJ_HEAD_TN = 1024
PROJ_TN = 1536
WEIGHT_RING_SLOTS = 3
CAST_SLABS = 16
MIX_T = 512
ROW_SUB = 256
OUT_TN = 512
FFN_HEAD_TM = 1024
FFN_HEAD_TF = 512
FFN_TM = 512
FFN_TF = 2048

_NT_DIMS = (((1,), (1,)), ((), ()))
_TN_DIMS = (((0,), (0,)), ((), ()))


def _layer_norm(y, g, b):
    mu = jnp.mean(y, axis=-1, keepdims=True)
    yc = y - mu
    var = jnp.mean(yc * yc, axis=-1, keepdims=True)
    return yc * lax.rsqrt(var + LN_EPS) * g + b


def _in_proj_head_kernel(x_ref, w_ref, wz_ref, o_ref, z_ref, wb_ref, wzb_ref, xb_ref):
    @pl.when(pl.program_id(0) == 0)
    def _():
        xb_ref[...] = x_ref[...].astype(BF16)
        wzb_ref[...] = wz_ref[...].astype(BF16)
        z_ref[...] = lax.dot_general(xb_ref[...], wzb_ref[...], _NT_DIMS,
                                     preferred_element_type=F32).astype(z_ref.dtype)

    wb_ref[...] = w_ref[...].astype(BF16)
    o_ref[...] = lax.dot_general(xb_ref[...], wb_ref[...], _NT_DIMS,
                                 preferred_element_type=F32).astype(o_ref.dtype)


def _in_proj_rest_kernel(x_ref, wb_hbm_ref, wzb_ref, pa_ref, za_ref, wo_ref,
                         o_ref, z_ref, wo_b_ref, xb_ref, ring_ref, ring_sem,
                         *, head_tiles, n_tiles, n_j):
    i, j = pl.program_id(0), pl.program_id(1)
    n_matmul_steps = (n_tiles - head_tiles) * n_j

    def weight_copy(c):
        rows = pl.ds(pl.multiple_of((c % n_j) * PROJ_TN, PROJ_TN), PROJ_TN)
        slot = c % WEIGHT_RING_SLOTS
        return pltpu.make_async_copy(wb_hbm_ref.at[rows, :], ring_ref.at[slot], ring_sem.at[slot])

    wo_b_ref[...] = wo_ref[...].astype(BF16)

    @pl.when(i < head_tiles)
    def _():
        @pl.when(jnp.logical_and(i == 0, j == 0))
        def _():
            for c in range(WEIGHT_RING_SLOTS - 1):
                weight_copy(c).start()

        o_ref[...] = pa_ref[...]

        @pl.when(j == 0)
        def _():
            z_ref[...] = za_ref[...]

    @pl.when(i >= head_tiles)
    def _():
        c = (i - head_tiles) * n_j + j

        @pl.when(c + (WEIGHT_RING_SLOTS - 1) < n_matmul_steps)
        def _():
            weight_copy(c + (WEIGHT_RING_SLOTS - 1)).start()

        @pl.when(j == 0)
        def _():
            xb_ref[...] = x_ref[...].astype(BF16)
            z_ref[...] = lax.dot_general(xb_ref[...], wzb_ref[...], _NT_DIMS,
                                         preferred_element_type=F32).astype(z_ref.dtype)

        weight_copy(c).wait()
        o_ref[...] = lax.dot_general(xb_ref[...], ring_ref[c % WEIGHT_RING_SLOTS], _NT_DIMS,
                                     preferred_element_type=F32).astype(o_ref.dtype)


def _in_proj(x2, w_in_t, w_out, layer):
    m = x2.shape[0]
    i0 = PROJ_HEAD_TM // PROJ_TM
    n_i = m // PROJ_TM

    n_jh = D_PROJ_MAIN // PROJ_HEAD_TN
    proj_a, z_a, w_in_b, wz_b = pl.pallas_call(
        _in_proj_head_kernel,
        grid=(n_jh,),
        in_specs=[pl.BlockSpec((PROJ_HEAD_TM, D_MODEL), lambda j: (0, 0), pipeline_mode=pl.Buffered(1)),
                  pl.BlockSpec((None, PROJ_HEAD_TN, D_MODEL), lambda j: (layer, j, 0)),
                  pl.BlockSpec((None, GATE_RANK, D_MODEL),
                               lambda j: (layer, D_PROJ_MAIN // GATE_RANK, 0))],
        out_specs=[pl.BlockSpec((PROJ_HEAD_TM, PROJ_HEAD_TN), lambda j: (0, j)),
                   pl.BlockSpec((PROJ_HEAD_TM, GATE_RANK), lambda j: (0, 0)),
                   pl.BlockSpec((PROJ_HEAD_TN, D_MODEL), lambda j: (j, 0)),
                   pl.BlockSpec((GATE_RANK, D_MODEL), lambda j: (0, 0))],
        out_shape=[jax.ShapeDtypeStruct((PROJ_HEAD_TM, D_PROJ_MAIN), BF16),
                   jax.ShapeDtypeStruct((PROJ_HEAD_TM, GATE_RANK), BF16),
                   jax.ShapeDtypeStruct((D_PROJ_MAIN, D_MODEL), BF16),
                   jax.ShapeDtypeStruct((GATE_RANK, D_MODEL), BF16)],
        scratch_shapes=[pltpu.VMEM((PROJ_HEAD_TM, D_MODEL), BF16)],
        compiler_params=pltpu.CompilerParams(
            dimension_semantics=("arbitrary",), vmem_limit_bytes=VMEM_LIMIT),
        name="in_proj_head",
    )(x2, w_in_t, w_in_t)

    n_j = D_PROJ_MAIN // PROJ_TN
    slab_rows = D_MODEL // CAST_SLABS
    slab = lambda i, j: jnp.minimum(i * n_j + j, CAST_SLABS - 1)
    copying = lambda i: i < i0
    assert i0 >= 1
    return pl.pallas_call(
        functools.partial(_in_proj_rest_kernel, head_tiles=i0, n_tiles=n_i, n_j=n_j),
        grid=(n_i, n_j),
        in_specs=[pl.BlockSpec((PROJ_TM, D_MODEL), lambda i, j: (jnp.maximum(i, i0), 0)),
                  pl.BlockSpec(memory_space=pl.ANY),
                  pl.BlockSpec((GATE_RANK, D_MODEL), lambda i, j: (0, 0)),
                  pl.BlockSpec((PROJ_TM, PROJ_TN),
                               lambda i, j: (jnp.minimum(i, i0 - 1), jnp.where(copying(i), j, n_j - 1)),
                               pipeline_mode=pl.Buffered(1)),
                  pl.BlockSpec((PROJ_TM, GATE_RANK), lambda i, j: (jnp.minimum(i, i0 - 1), 0)),
                  pl.BlockSpec((None, slab_rows, D_MODEL), lambda i, j: (layer, slab(i, j), 0))],
        out_specs=[pl.BlockSpec((PROJ_TM, PROJ_TN), lambda i, j: (i, j)),
                   pl.BlockSpec((PROJ_TM, GATE_RANK), lambda i, j: (i, 0)),
                   pl.BlockSpec((slab_rows, D_MODEL), lambda i, j: (slab(i, j), 0))],
        out_shape=[jax.ShapeDtypeStruct((m, D_PROJ_MAIN), BF16),
                   jax.ShapeDtypeStruct((m, GATE_RANK), BF16),
                   jax.ShapeDtypeStruct((D_MODEL, D_MODEL), BF16)],
        scratch_shapes=[pltpu.VMEM((PROJ_TM, D_MODEL), BF16),
                        pltpu.VMEM((WEIGHT_RING_SLOTS, PROJ_TN, D_MODEL), BF16),
                        pltpu.SemaphoreType.DMA((WEIGHT_RING_SLOTS,))],
        compiler_params=pltpu.CompilerParams(
            dimension_semantics=("arbitrary", "arbitrary"), vmem_limit_bytes=VMEM_LIMIT),
        name="in_proj_rest",
    )(x2, w_in_b, wz_b, proj_a, z_a, w_out)


def _shift_rows(h3, carry3, shift):
    pos = lax.broadcasted_iota(jnp.int32, h3.shape, 1)
    rolled = pltpu.roll(h3, shift, axis=1)
    rolled_prev = jnp.concatenate([pltpu.roll(carry3, shift, axis=1), rolled[:-1]], axis=0)
    return jnp.where(pos < shift, rolled_prev, rolled)


def _conv_piece(grp, proj_ref, w_ref, g_ref, carry_ref, y_ref):
    ts = proj_ref.shape[0]
    lo = grp * CONV_GROUP_WIDTH
    sl = slice(lo, lo + CONV_GROUP_WIDTH)
    h = (proj_ref[:, COL_C + lo:COL_C + lo + CONV_GROUP_WIDTH].astype(F32)
         * proj_ref[:, COL_U + lo:COL_U + lo + CONV_GROUP_WIDTH].astype(F32))
    h3 = h.reshape(ts // SUBLANES, SUBLANES, CONV_GROUP_WIDTH)
    carry3 = carry_ref[:, sl].reshape(1, SUBLANES, CONV_GROUP_WIDTH)
    h1 = _shift_rows(h3, carry3, 1).reshape(h.shape)
    h2 = _shift_rows(h3, carry3, 2).reshape(h.shape)
    carry_ref[:, sl] = h[ts - SUBLANES:, :]
    w = w_ref[:, sl]
    y = (proj_ref[:, COL_B + lo:COL_B + lo + CONV_GROUP_WIDTH].astype(F32)
         * (w[2:3, :] * h + w[1:2, :] * h1 + w[0:1, :] * h2))
    ms = jnp.mean(y * y, axis=-1, keepdims=True)
    y_ref[:, sl] = (y * lax.rsqrt(ms + RMS_EPS) * g_ref[:, sl]).astype(y_ref.dtype)


def _chunk_cumsum(x):
    t_blk, width = x.shape
    x3 = x.reshape(t_blk // SUBLANES, SUBLANES, width)
    pos = lax.broadcasted_iota(jnp.int32, x3.shape, 1)
    shift = 1
    while shift < SUBLANES:
        x3 = x3 + jnp.where(pos >= shift, pltpu.roll(x3, shift, axis=1), 0.0)
        shift *= 2
    vregs_per_chunk = CHUNK // SUBLANES
    x4 = x3.reshape(t_blk // CHUNK, vregs_per_chunk, SUBLANES, width)
    outs = [x4[:, 0]]
    for j in range(1, vregs_per_chunk):
        outs.append(x4[:, j] + outs[-1][:, SUBLANES - 1:SUBLANES, :])
    return jnp.stack(outs, axis=1).reshape(t_blk, width)


def _gla_prepare(head, proj_ref, z_ref, gb_ref, qd_ref, ki_ref, ke_ref, dct_ref):
    t_blk = proj_ref.shape[0]
    n_chunks = t_blk // CHUNK
    col_q, col_k = COL_Q + head * HEAD_K, COL_K + head * HEAD_K
    z = z_ref[:, head * HEAD_K:(head + 1) * HEAD_K] + gb_ref[:, head * HEAD_K:(head + 1) * HEAD_K]
    log_sig = jnp.minimum(z, 0.0) - jnp.log(1.0 + jnp.exp(-jnp.abs(z)))
    bcum = _chunk_cumsum(log_sig * (1.0 / GATE_TAU))
    qd_ref[head] = ((proj_ref[:, col_q:col_q + HEAD_K].astype(F32) * (HEAD_K ** -0.5))
                    * jnp.exp(bcum)).astype(BF16)
    k_inv_f = proj_ref[:, col_k:col_k + HEAD_K].astype(F32) * jnp.exp(-bcum)
    ki_ref[head] = k_inv_f.astype(BF16)
    decay = jnp.exp(bcum.reshape(n_chunks, CHUNK, HEAD_K)[:, CHUNK - 1:CHUNK, :])
    ke_ref[head] = (k_inv_f.reshape(n_chunks, CHUNK, HEAD_K) * decay).astype(BF16).reshape(
        t_blk, HEAD_K)
    pad = jnp.zeros((HEAD_K - n_chunks, HEAD_K), F32)
    dct_ref[head] = jnp.transpose(jnp.concatenate([decay.reshape(n_chunks, HEAD_K), pad], axis=0))


def _gla_scores(head, c, qd_ref, ki_ref, sc_ref):
    rows = slice(c * CHUNK, (c + 1) * CHUNK)
    causal = (lax.broadcasted_iota(jnp.int32, (CHUNK, CHUNK), 0)
              >= lax.broadcasted_iota(jnp.int32, (CHUNK, CHUNK), 1))
    scores = lax.dot_general(qd_ref[head, rows, :], ki_ref[head, rows, :], _NT_DIMS,
                             preferred_element_type=F32)
    sc_ref[head, rows, :] = jnp.where(causal, scores, 0.0).astype(sc_ref.dtype)


def _gla_state_step(head, c, proj_ref, ke_ref, dct_ref, state_ref, st_ref):
    rows = slice(c * CHUNK, (c + 1) * CHUNK)
    col_v = COL_V + head * HEAD_V
    state = state_ref[head]
    st_ref[head, c] = state.astype(st_ref.dtype)
    delta = lax.dot_general(ke_ref[head, rows, :], proj_ref[rows, col_v:col_v + HEAD_V], _TN_DIMS,
                            preferred_element_type=F32)
    state_ref[head] = dct_ref[head, :, c:c + 1] * state + delta


def _gla_output(head, c, proj_ref, g_ref, qd_ref, sc_ref, st_ref, y_ref):
    rows = slice(c * CHUNK, (c + 1) * CHUNK)
    col_v, col_r = COL_V + head * HEAD_V, COL_R + head * HEAD_V
    col_y = D_CONV + head * HEAD_V
    o = (jnp.dot(sc_ref[head, rows, :], proj_ref[rows, col_v:col_v + HEAD_V],
                 preferred_element_type=F32)
         + jnp.dot(qd_ref[head, rows, :], st_ref[head, c], preferred_element_type=F32))
    ms = jnp.mean(o * o, axis=-1, keepdims=True)
    o_n = o * lax.rsqrt(ms + RMS_EPS) * g_ref[:, head * HEAD_V:(head + 1) * HEAD_V]
    r_c = proj_ref[rows, col_r:col_r + HEAD_V].astype(F32)
    y_ref[rows, col_y:col_y + HEAD_V] = (o_n * (r_c * jax.nn.sigmoid(r_c))).astype(y_ref.dtype)


def _mixer_kernel(proj_ref, zl_ref, x_ref, wo_ref, cw_ref, cg_ref, wg_ref, gb_ref, gg_ref,
                  ln_g_ref, ln_b_ref, o_ref, y_ref, yp_ref, mix_ref, state_ref,
                  carry_ref, qd_ref, ki_ref, ke_ref, dct_ref, z_ref, sc_ref, st_ref, *, blocks_per_seq):
    s = pl.program_id(0)

    @pl.when(s % blocks_per_seq == 0)
    def _():
        state_ref[...] = jnp.zeros_like(state_ref)
        carry_ref[...] = jnp.zeros_like(carry_ref)

    def keep_previous_y():
        yp_ref[...] = y_ref[...]

    gla_refs =(qd_ref, ki_ref, ke_ref, dct_ref)
    n_chunks = proj_ref.shape[0] // CHUNK
    vector_pieces = [functools.partial(_conv_piece, grp, proj_ref, cw_ref, cg_ref, carry_ref, y_ref)
                     for grp in range(CONV_GROUPS)]
    vector_pieces += [functools.partial(_gla_prepare, head, proj_ref, z_ref, gb_ref, *gla_refs)
                      for head in range(GLA_HEADS)]
    head_chunks = [(head, c) for c in range(n_chunks) for head in range(GLA_HEADS)]
    chunk_pieces = (
        [functools.partial(_gla_scores, h, c, qd_ref, ki_ref, sc_ref) for h, c in head_chunks]
        + [functools.partial(_gla_state_step, h, c, proj_ref, ke_ref, dct_ref, state_ref, st_ref)
           for h, c in head_chunks]
        + [functools.partial(_gla_output, h, c, proj_ref, gg_ref, qd_ref, sc_ref, st_ref, y_ref)
           for h, c in head_chunks])

    def gate_preactivation():
        z_ref[...] = jnp.dot(zl_ref[...], wg_ref[...], preferred_element_type=F32)

    def out_proj_tile(sub, n):
        rows = slice(sub * ROW_SUB, (sub + 1) * ROW_SUB)
        cols = slice(n * OUT_TN, (n + 1) * OUT_TN)
        mix_ref[rows, cols] = jnp.dot(yp_ref[rows, :], wo_ref[:, cols], preferred_element_type=F32)

    def layer_norm_rows(row0, n_rows):
        rows = slice(row0, row0 + n_rows)
        o_ref[rows, :] = _layer_norm(DN_ALPHA * x_ref[rows, :] + mix_ref[rows, :],
                                     ln_g_ref[...], ln_b_ref[...]).astype(o_ref.dtype)

    tiles_per_sub = D_MODEL // OUT_TN
    n_tiles = (o_ref.shape[0] // ROW_SUB) * tiles_per_sub
    ln_rows = ROW_SUB // 2
    ln_queue = []
    pieces = [gate_preactivation]
    for t in range(n_tiles):
        sub, n = divmod(t, tiles_per_sub)
        lo = len(vector_pieces) * t // n_tiles
        hi = len(vector_pieces) * (t + 1) // n_tiles
        pieces += vector_pieces[lo:hi] + ln_queue[:1] + [functools.partial(out_proj_tile, sub, n)]
        del ln_queue[:1]
        if n == tiles_per_sub - 1:
            ln_queue += [functools.partial(layer_norm_rows, sub * ROW_SUB + r, ln_rows)
                         for r in range(0, ROW_SUB, ln_rows)]
    everything = pieces + chunk_pieces + ln_queue
    mixing = [gate_preactivation] + vector_pieces + chunk_pieces
    last = pl.num_programs(0) - 1

    @pl.when(s == 0)
    def _():
        for piece in mixing:
            piece()

    @pl.when(jnp.logical_and(s > 0, s < last))
    def _():
        for piece in [keep_previous_y] + everything:
            piece()

    @pl.when(s == last)
    def _():
        for piece in [keep_previous_y] + everything:
            if piece not in mixing:
                piece()


def _mixer(proj, z_low, x2, w_out_b, conv_w8, conv_g, wg_b, gate_bias, gla_g, ln_g, ln_b, bsz, seq):
    m = x2.shape[0]
    n_blk = m // MIX_T
    cur_blk = lambda s: (jnp.minimum(s, n_blk - 1), 0)
    prev_blk = lambda s: (jnp.maximum(s - 1, 0), 0)
    const2 = lambda s: (0, 0)
    return pl.pallas_call(
        functools.partial(_mixer_kernel, blocks_per_seq=seq // MIX_T),
        grid=(n_blk + 1,),
        in_specs=[pl.BlockSpec((MIX_T, D_PROJ_MAIN), cur_blk),
                  pl.BlockSpec((MIX_T, GATE_RANK), cur_blk),
                  pl.BlockSpec((MIX_T, D_MODEL), prev_blk),
                  pl.BlockSpec((D_CONV + D_GLA_V, D_MODEL), const2, pipeline_mode=pl.Buffered(1)),
                  pl.BlockSpec((8, D_CONV), const2),
                  pl.BlockSpec((1, D_CONV), const2),
                  pl.BlockSpec((GATE_RANK, D_GLA_K), const2),
                  pl.BlockSpec((1, D_GLA_K), const2),
                  pl.BlockSpec((1, D_GLA_V), const2),
                  pl.BlockSpec((1, D_MODEL), const2),
                  pl.BlockSpec((1, D_MODEL), const2)],
        out_specs=pl.BlockSpec((MIX_T, D_MODEL), prev_blk),
        out_shape=jax.ShapeDtypeStruct((m, D_MODEL), BF16),
        scratch_shapes=[pltpu.VMEM((MIX_T, D_CONV + D_GLA_V), BF16),
                        pltpu.VMEM((MIX_T, D_CONV + D_GLA_V), BF16),
                        pltpu.VMEM((MIX_T, D_MODEL), F32),
                        pltpu.VMEM((GLA_HEADS, HEAD_K, HEAD_V), F32),
                        pltpu.VMEM((8, D_CONV), F32),
                        pltpu.VMEM((GLA_HEADS, MIX_T, HEAD_K), BF16),
                        pltpu.VMEM((GLA_HEADS, MIX_T, HEAD_K), BF16),
                        pltpu.VMEM((GLA_HEADS, MIX_T, HEAD_K), BF16),
                        pltpu.VMEM((GLA_HEADS, HEAD_K, HEAD_K), F32),
                        pltpu.VMEM((MIX_T, D_GLA_K), F32),
                        pltpu.VMEM((GLA_HEADS, MIX_T, CHUNK), BF16),
                        pltpu.VMEM((GLA_HEADS, MIX_T // CHUNK, HEAD_K, HEAD_V), BF16)],
        compiler_params=pltpu.CompilerParams(
            dimension_semantics=("arbitrary",), vmem_limit_bytes=VMEM_LIMIT),
        name="mixer",
    )(proj, z_low, x2, w_out_b, conv_w8, conv_g, wg_b, gate_bias, gla_g, ln_g, ln_b)


def _ffn_body(f, last, x_ref, up_tile, down_tile, g_ref, b_ref, o_ref):
    def hidden():
        h = jnp.dot(x_ref[...], up_tile(), preferred_element_type=F32)
        h = jnp.maximum(h, 0.0)
        return (h * h).astype(BF16)

    @pl.when(f == 0)
    def _():
        o_ref[...] = jnp.dot(hidden(), down_tile(), preferred_element_type=F32)

    @pl.when(jnp.logical_and(f > 0, f < last))
    def _():
        o_ref[...] += jnp.dot(hidden(), down_tile(), preferred_element_type=F32)

    @pl.when(f == last)
    def _():
        h = hidden()
        wd = down_tile()
        for s in range(o_ref.shape[0] // ROW_SUB):
            rows = slice(s * ROW_SUB, (s + 1) * ROW_SUB)
            ff = o_ref[rows, :] + jnp.dot(h[rows, :], wd, preferred_element_type=F32)
            o_ref[rows, :] = _layer_norm(DN_ALPHA * x_ref[rows, :].astype(F32) + ff,
                                         g_ref[...], b_ref[...])


def _ffn_head_kernel(x_ref, wu_ref, wd_ref, g_ref, b_ref, o_ref, wu_b_ref, wd_b_ref):
    def up_tile():
        wu_b_ref[...] = wu_ref[...].astype(BF16)
        return wu_b_ref[...]

    def down_tile():
        wd_b_ref[...] = wd_ref[...].astype(BF16)
        return wd_b_ref[...]

    _ffn_body(pl.program_id(0), pl.num_programs(0) - 1, x_ref, up_tile, down_tile, g_ref, b_ref, o_ref)


def _ffn_rest_kernel(x_ref, wu_ref, wd_ref, g_ref, b_ref, oa_ref, o_ref, *, head_tiles):
    i, f = pl.program_id(0), pl.program_id(1)
    last = pl.num_programs(1) - 1

    @pl.when(jnp.logical_and(i < head_tiles, f == last))
    def _():
        o_ref[...] = oa_ref[...]

    @pl.when(i >= head_tiles)
    def _():
        _ffn_body(f, last, x_ref, lambda: wu_ref[...], lambda: wd_ref[...], g_ref, b_ref, o_ref)


def _ffn_ln2(x1, w_ff_up, w_ff_down, ln_g, ln_b, layer):
    m = x1.shape[0]

    out_a, w_up_b, w_down_b = pl.pallas_call(
        _ffn_head_kernel,
        grid=(D_FF // FFN_HEAD_TF,),
        in_specs=[pl.BlockSpec((FFN_HEAD_TM, D_MODEL), lambda f: (0, 0), pipeline_mode=pl.Buffered(1)),
                  pl.BlockSpec((None, D_MODEL, FFN_HEAD_TF), lambda f: (layer, 0, f)),
                  pl.BlockSpec((None, FFN_HEAD_TF, D_MODEL), lambda f: (layer, f, 0)),
                  pl.BlockSpec((1, D_MODEL), lambda f: (0, 0)),
                  pl.BlockSpec((1, D_MODEL), lambda f: (0, 0))],
        out_specs=[pl.BlockSpec((FFN_HEAD_TM, D_MODEL), lambda f: (0, 0)),
                   pl.BlockSpec((D_MODEL, FFN_HEAD_TF), lambda f: (0, f)),
                   pl.BlockSpec((FFN_HEAD_TF, D_MODEL), lambda f: (f, 0))],
        out_shape=[jax.ShapeDtypeStruct((FFN_HEAD_TM, D_MODEL), F32),
                   jax.ShapeDtypeStruct((D_MODEL, D_FF), BF16),
                   jax.ShapeDtypeStruct((D_FF, D_MODEL), BF16)],
        compiler_params=pltpu.CompilerParams(
            dimension_semantics=("arbitrary",), vmem_limit_bytes=VMEM_LIMIT),
        name="ffn_head",
    )(x1, w_ff_up, w_ff_down, ln_g, ln_b)

    i0 = FFN_HEAD_TM // FFN_TM
    copying = lambda i: i < i0
    return pl.pallas_call(
        functools.partial(_ffn_rest_kernel, head_tiles=i0),
        grid=(m // FFN_TM, D_FF // FFN_TF),
        in_specs=[pl.BlockSpec((FFN_TM, D_MODEL), lambda i, f: (jnp.maximum(i, i0), 0)),
                  pl.BlockSpec((D_MODEL, FFN_TF), lambda i, f: (0, jnp.where(copying(i), 0, f))),
                  pl.BlockSpec((FFN_TF, D_MODEL), lambda i, f: (jnp.where(copying(i), 0, f), 0)),
                  pl.BlockSpec((1, D_MODEL), lambda i, f: (0, 0)),
                  pl.BlockSpec((1, D_MODEL), lambda i, f: (0, 0)),
                  pl.BlockSpec((FFN_TM, D_MODEL), lambda i, f: (jnp.minimum(i, i0 - 1), 0),
                               pipeline_mode=pl.Buffered(1))],
        out_specs=pl.BlockSpec((FFN_TM, D_MODEL), lambda i, f: (i, 0)),
        out_shape=jax.ShapeDtypeStruct((m, D_MODEL), F32),
        compiler_params=pltpu.CompilerParams(
            dimension_semantics=("arbitrary", "arbitrary"), vmem_limit_bytes=VMEM_LIMIT),
        name="ffn_rest",
    )(x1, w_up_b, w_down_b, ln_g, ln_b, out_a)


def kernel(x, w_in, conv_w, conv_norm_g, w_gate_up, gate_bias, gla_norm_g, w_out,
           ln1_g, ln1_b, w_ff_up, w_ff_down, ln2_g, ln2_b):
    bsz, seq, _ = x.shape
    assert seq % MIX_T == 0 and MIX_T % CHUNK == 0 and MIX_T % ROW_SUB == 0
    x2 = x.reshape(bsz * seq, D_MODEL)
    w_in_t = jnp.swapaxes(w_in, 1, 2)
    for l in range(DEPTH):
        wg_b = w_gate_up[l].astype(BF16)
        gb = gate_bias[l].reshape(1, D_GLA_K)
        conv_w8 = jnp.pad(conv_w[l], ((0, 8 - conv_w.shape[1]), (0, 0)))

        proj, z_low, w_out_b = _in_proj(x2, w_in_t, w_out, l)
        x1 = _mixer(proj, z_low, x2, w_out_b, conv_w8, conv_norm_g[l].reshape(1, D_CONV),
                    wg_b, gb, gla_norm_g[l].reshape(1, D_GLA_V),
                    ln1_g[l].reshape(1, D_MODEL), ln1_b[l].reshape(1, D_MODEL), bsz, seq)
        x2 = _ffn_ln2(x1, w_ff_up, w_ff_down,
                      ln2_g[l].reshape(1, D_MODEL), ln2_b[l].reshape(1, D_MODEL), l)
    return x2.reshape(bsz, seq, D_MODEL)
```
